```python
import jax, jax.numpy as jnp
from jax import lax
import numpy as np

D_MODEL = 1024
BATCH = 8
SEQ = 2048
DEPTH = 2

CHUNK = 64
MIX_WIDTH = D_MODEL
ATT_HEADS = 8
HEAD_DIM = 64
ATT_WIDTH = ATT_HEADS * HEAD_DIM
LEFT_CHUNKS = 8
BAND = (LEFT_CHUNKS + 1) * CHUNK
MAX_REL = 2 * CHUNK
GMLP_WIDTH = MIX_WIDTH - ATT_WIDTH
GMLP_GROUPS = 8
GMLP_GROUP_DIM = GMLP_WIDTH // GMLP_GROUPS
GMLP_BLOCK = 128
D_FF = ((8 * D_MODEL // 3 + 255) // 256) * 256
IN_WIDTH = 3 * ATT_WIDTH + 2 * GMLP_WIDTH
EPS = 1e-6
NEG_INF = -1e30

kernel_name = "hybrid_bandattn_gmlp_streaming_block"


def rmsnorm(x, g):
    x32 = x.astype(jnp.float32)
    y = x32 * lax.rsqrt(jnp.mean(x32 * x32, axis=-1, keepdims=True) + EPS)
    return (y * g.astype(jnp.float32)).astype(x.dtype)


def band_attention(q, k, v, q_g, k_g, rel_table):
    B, S, _ = q.shape
    nc = S // CHUNK
    shp = (B, nc, CHUNK, ATT_HEADS, HEAD_DIM)
    q = rmsnorm(q.reshape(shp), q_g)
    k = rmsnorm(k.reshape(shp), k_g)
    v = v.reshape(shp)
    pad = ((0, 0), (LEFT_CHUNKS, 0), (0, 0), (0, 0), (0, 0))
    kp = jnp.pad(k, pad)
    vp = jnp.pad(v, pad)
    k_band = jnp.concatenate([kp[:, j:j + nc] for j in range(LEFT_CHUNKS + 1)], axis=2)
    v_band = jnp.concatenate([vp[:, j:j + nc] for j in range(LEFT_CHUNKS + 1)], axis=2)
    scores = jnp.einsum('bnqhd,bnkhd->bhnqk', q, k_band).astype(jnp.float32) * (HEAD_DIM ** -0.5)
    q_pos = LEFT_CHUNKS * CHUNK + jnp.arange(CHUNK)[:, None]
    k_pos = jnp.arange(BAND)[None, :]
    rel_idx = jnp.clip(q_pos - k_pos, -MAX_REL, MAX_REL) + MAX_REL
    bias = rel_table[:, rel_idx].astype(jnp.float32)
    scores = scores + bias[None, :, None, :, :]
    key_chunk = jnp.arange(nc)[:, None] - LEFT_CHUNKS + (jnp.arange(BAND) // CHUNK)[None, :]
    valid = key_chunk >= 0
    scores = jnp.where(valid[None, None, :, None, :], scores, NEG_INF)
    p = jax.nn.softmax(scores, axis=-1).astype(v.dtype)
    out = jnp.einsum('bhnqk,bnkhd->bnqhd', p, v_band)
    return out.reshape(B, S, ATT_WIDTH)


def spatial_gating(u, vg, norm_g, w_s, b_s):
    B, S, _ = u.shape
    nb = S // GMLP_BLOCK
    u = jax.nn.gelu(u)
    vg = rmsnorm(jax.nn.gelu(vg), norm_g)
    vg = vg.reshape(B, nb, GMLP_BLOCK, GMLP_GROUPS, GMLP_GROUP_DIM)
    t = jnp.arange(GMLP_BLOCK)
    mask = (t[:, None] // CHUNK) >= (t[None, :] // CHUNK)
    w = jnp.where(mask[None], w_s, jnp.zeros_like(w_s))
    mixed = jnp.einsum('gts,bnsgc->bntgc', w, vg) + b_s.T[None, None, :, :, None]
    return u * mixed.reshape(B, S, GMLP_WIDTH)


def setup_inputs(seed: int = 0) -> dict:
    key = jax.random.key(seed)
    ks = jax.random.split(key, 16)
    f32 = jnp.float32
    nrm = lambda k, shp, s: jax.random.normal(k, shp, f32) * s
    return {
        "x": nrm(ks[0], (BATCH, SEQ, D_MODEL), 1.0),
        "mix_norm_g": 1.0 + nrm(ks[1], (DEPTH, D_MODEL), 0.02),
        "w_in": nrm(ks[2], (DEPTH, D_MODEL, IN_WIDTH), D_MODEL ** -0.5),
        "q_norm_g": 1.0 + nrm(ks[3], (DEPTH, HEAD_DIM), 0.02),
        "k_norm_g": 1.0 + nrm(ks[4], (DEPTH, HEAD_DIM), 0.02),
        "rel_bias": nrm(ks[5], (DEPTH, ATT_HEADS, 2 * MAX_REL + 1), 0.1),
        "sgu_norm_g": 1.0 + nrm(ks[6], (DEPTH, GMLP_WIDTH), 0.02),
        "w_spatial": nrm(ks[7], (DEPTH, GMLP_GROUPS, GMLP_BLOCK, GMLP_BLOCK), 0.5 * GMLP_BLOCK ** -0.5),
        "b_spatial": 1.0 + nrm(ks[8], (DEPTH, GMLP_GROUPS, GMLP_BLOCK), 0.01),
        "att_out_norm_g": 1.0 + nrm(ks[9], (DEPTH, ATT_WIDTH), 0.02),
        "gmlp_out_norm_g": 1.0 + nrm(ks[10], (DEPTH, GMLP_WIDTH), 0.02),
        "w_out": nrm(ks[11], (DEPTH, MIX_WIDTH, D_MODEL), MIX_WIDTH ** -0.5),
        "ffn_norm_g": 1.0 + nrm(ks[12], (DEPTH, D_MODEL), 0.02),
        "w_ffn_in": nrm(ks[13], (DEPTH, D_MODEL, 2 * D_FF), D_MODEL ** -0.5),
        "w_ffn_out": nrm(ks[14], (DEPTH, D_FF, D_MODEL), D_FF ** -0.5),
    }


def reference(x, mix_norm_g, w_in, q_norm_g, k_norm_g, rel_bias, sgu_norm_g, w_spatial,
              b_spatial, att_out_norm_g, gmlp_out_norm_g, w_out, ffn_norm_g, w_ffn_in,
              w_ffn_out):
    split_pts = [ATT_WIDTH, 2 * ATT_WIDTH, 3 * ATT_WIDTH, 3 * ATT_WIDTH + GMLP_WIDTH]
    for l in range(DEPTH):
        h = rmsnorm(x, mix_norm_g[l])
        proj = jnp.einsum('bsd,de->bse', h, w_in[l])
        q, k, v, u, vg = jnp.split(proj, split_pts, axis=-1)
        a = band_attention(q, k, v, q_norm_g[l], k_norm_g[l], rel_bias[l])
        g = spatial_gating(u, vg, sgu_norm_g[l], w_spatial[l], b_spatial[l])
        mix = jnp.concatenate([rmsnorm(a, att_out_norm_g[l]),
                               rmsnorm(g, gmlp_out_norm_g[l])], axis=-1)
        x = x + jnp.einsum('bse,ed->bsd', mix, w_out[l])
        h = rmsnorm(x, ffn_norm_g[l])
        gate, up = jnp.split(jnp.einsum('bsd,df->bsf', h, w_ffn_in[l]), 2, axis=-1)
        x = x + jnp.einsum('bsf,fd->bsd', jax.nn.silu(gate) * up, w_ffn_out[l])
    return x
```

```python
import functools

import jax
import jax.numpy as jnp
from jax import lax
from jax.experimental import pallas as pl
from jax.experimental.pallas import tpu as pltpu

D_MODEL = 1024
CHUNK = 64
ATT_HEADS = 8
HEAD_DIM = 64
ATT_WIDTH = ATT_HEADS * HEAD_DIM
LEFT_CHUNKS = 8
LEFT = LEFT_CHUNKS * CHUNK
BAND = LEFT + CHUNK
MAX_REL = 2 * CHUNK
GMLP_WIDTH = 512
GMLP_GROUPS = 8
GMLP_GROUP_DIM = GMLP_WIDTH // GMLP_GROUPS
GMLP_BLOCK = 128
D_FF = 2816
IN_WIDTH = 3 * ATT_WIDTH + 2 * GMLP_WIDTH
EPS = 1e-6
NEG_INF = -1e30

LANES = 128
MXU_COLS = 256
TQ = 256
TK = LEFT + TQ
TM = 256
FF_CHUNK = 256
VMEM_LIMIT_BYTES = 56 * 1024 * 1024


def _rms(x, g):
    ms = jnp.mean(x * x, axis=-1, keepdims=True)
    return (x * lax.rsqrt(ms + EPS)) * g


def _head_rms(z, gain, post_scale):
    rows = z.shape[0]
    lo = lax.broadcasted_iota(jnp.int32, (rows, LANES), 1) < HEAD_DIM
    outs = []
    for c in range(ATT_WIDTH // LANES):
        zc = z[:, c * LANES:(c + 1) * LANES]
        sq = zc * zc
        ms_lo = jnp.sum(jnp.where(lo, sq, 0.0), axis=-1, keepdims=True) * (1.0 / HEAD_DIM)
        ms_hi = jnp.sum(jnp.where(lo, 0.0, sq), axis=-1, keepdims=True) * (1.0 / HEAD_DIM)
        r = jnp.where(lo, lax.rsqrt(ms_lo + EPS), lax.rsqrt(ms_hi + EPS))
        y = (zc * r) * gain[:, c * LANES:(c + 1) * LANES]
        if post_scale != 1.0:
            y = y * post_scale
        outs.append(y)
    return jnp.concatenate(outs, axis=-1)


def _mix_kernel(x_ref, mixg_ref, w_in_ref, qg_ref, kg_ref, bias_ref, sgug_ref, wsp_ref,
                bsp_ref, ag_ref, gg_ref, w_out_ref, o_ref, k_sc, v_sc):
    t = pl.program_id(1)
    bf16 = jnp.bfloat16
    f32 = jnp.float32

    @pl.when(t == 0)
    def _():
        k_sc[0:LEFT, :] = jnp.zeros((LEFT, ATT_WIDTH), bf16)
        v_sc[0:LEFT, :] = jnp.zeros((LEFT, ATT_WIDTH), bf16)

    x = x_ref[...]
    h = _rms(x, mixg_ref[...]).astype(bf16)

    def proj(lo, width):
        return jnp.dot(h, w_in_ref[:, lo:lo + width], preferred_element_type=f32)

    q = proj(0, ATT_WIDTH)
    k = proj(ATT_WIDTH, ATT_WIDTH)
    v = proj(2 * ATT_WIDTH, ATT_WIDTH)
    u = proj(3 * ATT_WIDTH, GMLP_WIDTH)
    vg = proj(3 * ATT_WIDTH + GMLP_WIDTH, GMLP_WIDTH)

    qn = _head_rms(q, qg_ref[...], HEAD_DIM ** -0.5).astype(bf16)
    kn = _head_rms(k, kg_ref[...], 1.0).astype(bf16)
    row0 = pl.multiple_of(t * TQ, TQ)
    k_sc[pl.ds(row0 + LEFT, TQ), :] = kn
    v_sc[pl.ds(row0 + LEFT, TQ), :] = v.astype(bf16)
    kw = k_sc[pl.ds(row0, TK), :]
    vw = v_sc[pl.ds(row0, TK), :]

    lane_q = lax.broadcasted_iota(jnp.int32, (TQ, LANES), 1)
    key_valid = lax.broadcasted_iota(jnp.int32, (TQ, TK), 1) >= LEFT - row0
    lane_o = lax.broadcasted_iota(jnp.int32, (TQ, MXU_COLS), 1)

    quads = []
    for c4 in range(ATT_WIDTH // MXU_COLS):
        acc = None
        for hh in range(MXU_COLS // HEAD_DIM):
            head = c4 * (MXU_COLS // HEAD_DIM) + hh
            c = head // 2
            in_head = (lane_q < HEAD_DIM) if head % 2 == 0 else (lane_q >= HEAD_DIM)
            qm = jnp.where(in_head, qn[:, c * LANES:(c + 1) * LANES], jnp.zeros((), bf16))
            s = lax.dot_general(qm, kw[:, c * LANES:(c + 1) * LANES],
                                (((1,), (1,)), ((), ())), preferred_element_type=f32)
            s = s + bias_ref[head]
            s = jnp.where(key_valid, s, NEG_INF)
            m = jnp.max(s, axis=-1, keepdims=True)
            p = jnp.exp(s - m)
            l = jnp.sum(p, axis=-1, keepdims=True)
            pv = jnp.dot(p.astype(bf16), vw[:, c4 * MXU_COLS:(c4 + 1) * MXU_COLS],
                         preferred_element_type=f32)
            pv = pv * (1.0 / l)
            if acc is None:
                acc = pv
            else:
                acc = jnp.where(lane_o >= hh * HEAD_DIM, pv, acc)
        quads.append(acc)
    a = jnp.concatenate(quads, axis=-1)
    a_n = _rms(a, ag_ref[...]).astype(bf16)

    u_act = jax.nn.gelu(u)
    vgn = _rms(jax.nn.gelu(vg), sgug_ref[...]).astype(bf16)
    ti = lax.broadcasted_iota(jnp.int32, (GMLP_GROUPS, GMLP_BLOCK, GMLP_BLOCK), 1)
    si = lax.broadcasted_iota(jnp.int32, (GMLP_GROUPS, GMLP_BLOCK, GMLP_BLOCK), 2)
    causal = (ti // CHUNK) >= (si // CHUNK)
    wsp = jnp.where(causal, wsp_ref[...], 0.0).astype(bf16)
    wsp = wsp.reshape(GMLP_GROUPS * GMLP_BLOCK, GMLP_BLOCK)
    lane_g = lax.broadcasted_iota(jnp.int32, (GMLP_BLOCK, MXU_COLS), 1)
    groups_per_quad = MXU_COLS // GMLP_GROUP_DIM
    blocks = []
    for blk in range(TQ // GMLP_BLOCK):
        cols = []
        for c4 in range(GMLP_WIDTH // MXU_COLS):
            lhs = wsp[c4 * groups_per_quad * GMLP_BLOCK:(c4 + 1) * groups_per_quad * GMLP_BLOCK, :]
            rhs = vgn[blk * GMLP_BLOCK:(blk + 1) * GMLP_BLOCK, c4 * MXU_COLS:(c4 + 1) * MXU_COLS]
            res = jnp.dot(lhs, rhs, preferred_element_type=f32)
            acc = res[0:GMLP_BLOCK, :]
            for gg in range(1, groups_per_quad):
                acc = jnp.where(lane_g >= gg * GMLP_GROUP_DIM,
                                res[gg * GMLP_BLOCK:(gg + 1) * GMLP_BLOCK, :], acc)
            cols.append(acc)
        blocks.append(jnp.concatenate(cols, axis=-1) + bsp_ref[...])
    mixed = jnp.concatenate(blocks, axis=0)
    g_n = _rms(u_act * mixed, gg_ref[...]).astype(bf16)

    mix = jnp.concatenate([a_n, g_n], axis=-1)
    o_ref[...] = x + jnp.dot(mix, w_out_ref[...], preferred_element_type=f32)


def _ffn_kernel(x_ref, g_ref, w1_ref, w2_ref, o_ref, act_sc):
    bf16 = jnp.bfloat16
    f32 = jnp.float32
    x = x_ref[...]
    h = _rms(x, g_ref[...]).astype(bf16)
    for j in range(D_FF // FF_CHUNK):
        gate = jnp.dot(h, w1_ref[:, j * FF_CHUNK:(j + 1) * FF_CHUNK], preferred_element_type=f32)
        up = jnp.dot(h, w1_ref[:, D_FF + j * FF_CHUNK:D_FF + (j + 1) * FF_CHUNK],
                     preferred_element_type=f32)
        act_sc[:, j * FF_CHUNK:(j + 1) * FF_CHUNK] = (jax.nn.silu(gate) * up).astype(bf16)
    o_ref[...] = x + jnp.dot(act_sc[...], w2_ref[...], preferred_element_type=f32)


def _const_spec(shape):
    zeros = (0,) * len(shape)
    return pl.BlockSpec(shape, lambda *_: zeros, pipeline_mode=pl.Buffered(1))


def _mix_layer(x, mixg, w_in, qg, kg, bias, sgug, wsp, bsp, ag, gg, w_out):
    B, S, D = x.shape
    grid = (B, S // TQ)
    row_spec = pl.BlockSpec((None, TQ, D), lambda b, t: (b, t, 0))
    return pl.pallas_call(
        _mix_kernel,
        grid=grid,
        in_specs=[
            row_spec,
            _const_spec((1, D)),
            _const_spec((D, IN_WIDTH)),
            _const_spec((1, ATT_WIDTH)),
            _const_spec((1, ATT_WIDTH)),
            _const_spec((ATT_HEADS, TQ, TK)),
            _const_spec((1, GMLP_WIDTH)),
            _const_spec((GMLP_GROUPS, GMLP_BLOCK, GMLP_BLOCK)),
            _const_spec((GMLP_BLOCK, GMLP_WIDTH)),
            _const_spec((1, ATT_WIDTH)),
            _const_spec((1, GMLP_WIDTH)),
            _const_spec((D, D)),
        ],
        out_specs=row_spec,
        out_shape=jax.ShapeDtypeStruct(x.shape, x.dtype),
        scratch_shapes=[
            pltpu.VMEM((LEFT + S, ATT_WIDTH), jnp.bfloat16),
            pltpu.VMEM((LEFT + S, ATT_WIDTH), jnp.bfloat16),
        ],
        compiler_params=pltpu.CompilerParams(
            dimension_semantics=("arbitrary", "arbitrary"),
            vmem_limit_bytes=VMEM_LIMIT_BYTES),
        name="mix_layer",
    )(x, mixg, w_in, qg, kg, bias, sgug, wsp, bsp, ag, gg, w_out)


def _ffn_layer(x2d, g, w1, w2):
    T, D = x2d.shape
    row_spec = pl.BlockSpec((TM, D), lambda i: (i, 0))
    return pl.pallas_call(
        _ffn_kernel,
        grid=(T // TM,),
        in_specs=[
            row_spec,
            _const_spec((1, D)),
            _const_spec((D, 2 * D_FF)),
            _const_spec((D_FF, D)),
        ],
        out_specs=row_spec,
        out_shape=jax.ShapeDtypeStruct(x2d.shape, x2d.dtype),
        scratch_shapes=[pltpu.VMEM((TM, D_FF), jnp.bfloat16)],
        compiler_params=pltpu.CompilerParams(
            dimension_semantics=("arbitrary",),
            vmem_limit_bytes=VMEM_LIMIT_BYTES),
        name="ffn_layer",
    )(x2d, g, w1, w2)


def _bias_tile(rel_table):
    q_pos = LEFT + jnp.arange(TQ)[:, None]
    k_pos = jnp.arange(TK)[None, :]
    rel_idx = jnp.clip(q_pos - k_pos, -MAX_REL, MAX_REL) + MAX_REL
    bias = rel_table[:, rel_idx].astype(jnp.float32)
    q_chunk = q_pos // CHUNK
    k_chunk = k_pos // CHUNK
    in_band = (k_chunk <= q_chunk) & (k_chunk >= q_chunk - LEFT_CHUNKS)
    return jnp.where(in_band[None], bias, NEG_INF)


def kernel(x, mix_norm_g, w_in, q_norm_g, k_norm_g, rel_bias, sgu_norm_g, w_spatial, b_spatial,
           att_out_norm_g, gmlp_out_norm_g, w_out, ffn_norm_g, w_ffn_in, w_ffn_out):
    B, S, D = x.shape
    depth = w_in.shape[0]
    bf16 = jnp.bfloat16
    for l in range(depth):
        bsp = jnp.repeat(b_spatial[l].T, GMLP_GROUP_DIM, axis=1)
        x = _mix_layer(
            x, mix_norm_g[l][None], w_in[l].astype(bf16),
            jnp.tile(q_norm_g[l], ATT_HEADS)[None], jnp.tile(k_norm_g[l], ATT_HEADS)[None],
            _bias_tile(rel_bias[l]), sgu_norm_g[l][None], w_spatial[l], bsp,
            att_out_norm_g[l][None], gmlp_out_norm_g[l][None], w_out[l].astype(bf16))
        x = _ffn_layer(x.reshape(B * S, D), ffn_norm_g[l][None], w_ffn_in[l].astype(bf16),
                       w_ffn_out[l].astype(bf16)).reshape(B, S, D)
    return x
```

```python
import functools

import jax
import jax.numpy as jnp
from jax import lax
from jax.experimental import pallas as pl
from jax.experimental.pallas import tpu as pltpu

D_MODEL = 1024
CHUNK = 64
ATT_HEADS = 8
HEAD_DIM = 64
ATT_WIDTH = ATT_HEADS * HEAD_DIM
LEFT_CHUNKS = 8
LEFT = LEFT_CHUNKS * CHUNK
BAND = LEFT + CHUNK
MAX_REL = 2 * CHUNK
GMLP_WIDTH = 512
GMLP_GROUPS = 8
GMLP_GROUP_DIM = GMLP_WIDTH // GMLP_GROUPS
GMLP_BLOCK = 128
D_FF = 2816
IN_WIDTH = 3 * ATT_WIDTH + 2 * GMLP_WIDTH
EPS = 1e-6
NEG_INF = -1e30

LANES = 128
MXU_COLS = 256
TQ = 256
TK = LEFT + TQ
TM = 256
FF_CHUNK = 256
VMEM_LIMIT_BYTES = 56 * 1024 * 1024


def _rms(x, g):
    ms = jnp.mean(x * x, axis=-1, keepdims=True)
    return (x * lax.rsqrt(ms + EPS)) * g


def _head_rms(z, gain, post_scale):
    rows = z.shape[0]
    lo = lax.broadcasted_iota(jnp.int32, (rows, LANES), 1) < HEAD_DIM
    outs = []
    for c in range(ATT_WIDTH // LANES):
        zc = z[:, c * LANES:(c + 1) * LANES]
        sq = zc * zc
        ms_lo = jnp.sum(jnp.where(lo, sq, 0.0), axis=-1, keepdims=True) * (1.0 / HEAD_DIM)
        ms_hi = jnp.sum(jnp.where(lo, 0.0, sq), axis=-1, keepdims=True) * (1.0 / HEAD_DIM)
        r = jnp.where(lo, lax.rsqrt(ms_lo + EPS), lax.rsqrt(ms_hi + EPS))
        y = (zc * r) * gain[:, c * LANES:(c + 1) * LANES]
        if post_scale != 1.0:
            y = y * post_scale
        outs.append(y)
    return jnp.concatenate(outs, axis=-1)


def _build_bias(rtab_ref, bias_sc):
    f32 = jnp.float32
    row = lax.broadcasted_iota(jnp.int32, (CHUNK, MXU_COLS), 0)
    lo_half = lax.broadcasted_iota(jnp.int32, (CHUNK, LANES), 1) < CHUNK
    n_kc = TK // CHUNK
    for head in range(ATT_HEADS):
        r = rtab_ref[head:head + 1, :]
        far = rtab_ref[head:head + 1, MXU_COLS - 1:MXU_COLS]
        x0 = jnp.broadcast_to(r, (CHUNK, MXU_COLS))
        for b in range(CHUNK.bit_length() - 1):
            x0 = jnp.where(((row >> b) & 1) == 1, pltpu.roll(x0, 1 << b, axis=1), x0)
        x1 = pltpu.roll(x0, CHUNK, axis=1)
        far_piece = jnp.broadcast_to(far, (CHUNK, LANES))
        neg_piece = jnp.full((CHUNK, LANES), NEG_INF, f32)

        def half(qc, kc):
            rel = kc - qc
            if rel < 0 or rel > LEFT_CHUNKS:
                return neg_piece
            if rel < LEFT_CHUNKS - 2:
                return far_piece
            col = (rel - (LEFT_CHUNKS - 2)) * CHUNK + (qc % 2) * CHUNK
            src = x1 if qc % 2 else x0
            return src[:, (col // LANES) * LANES:(col // LANES + 1) * LANES]

        for qc in range(TQ // CHUNK):
            for vcol in range(n_kc // 2):
                piece = jnp.where(lo_half, half(qc, 2 * vcol), half(qc, 2 * vcol + 1))
                bias_sc[0, head, qc * CHUNK:(qc + 1) * CHUNK,
                        vcol * LANES:(vcol + 1) * LANES] = piece


def _mix_kernel(x_ref, mixg_ref, w_in_ref, qg_ref, kg_ref, rtab_ref, sgug_ref, wsp_ref,
                bsp_ref, ag_ref, gg_ref, w_out_ref, o_ref, k_sc, v_sc, bias_sc):
    b = pl.program_id(0)
    t = pl.program_id(1)
    bf16 = jnp.bfloat16
    f32 = jnp.float32
    row0 = pl.multiple_of(t * TQ, TQ)
    n_pad_tiles = LEFT // TQ

    @pl.when((b == 0) & (t == 0))
    def _():
        _build_bias(rtab_ref, bias_sc)

    @pl.when(t == 0)
    def _():
        k_sc[0:LEFT, :] = jnp.zeros((LEFT, ATT_WIDTH), bf16)
        v_sc[0:LEFT, :] = jnp.zeros((LEFT, ATT_WIDTH), bf16)

    @pl.when(t < n_pad_tiles)
    def _():
        key_valid = lax.broadcasted_iota(jnp.int32, (TQ, TK), 1) >= LEFT - row0
        for head in range(ATT_HEADS):
            bias_sc[1, head] = jnp.where(key_valid, bias_sc[0, head], NEG_INF)

    bias_slot = (t < n_pad_tiles).astype(jnp.int32)

    x = x_ref[...]
    h = _rms(x, mixg_ref[...]).astype(bf16)

    def proj(lo, width):
        return jnp.dot(h, w_in_ref[:, lo:lo + width], preferred_element_type=f32)

    q = proj(0, ATT_WIDTH)
    k = proj(ATT_WIDTH, ATT_WIDTH)
    v = proj(2 * ATT_WIDTH, ATT_WIDTH)
    u = proj(3 * ATT_WIDTH, GMLP_WIDTH)
    vg = proj(3 * ATT_WIDTH + GMLP_WIDTH, GMLP_WIDTH)

    qn = _head_rms(q, qg_ref[...], HEAD_DIM ** -0.5).astype(bf16)
    kn = _head_rms(k, kg_ref[...], 1.0).astype(bf16)
    k_sc[pl.ds(row0 + LEFT, TQ), :] = kn
    v_sc[pl.ds(row0 + LEFT, TQ), :] = v.astype(bf16)
    kw = k_sc[pl.ds(row0, TK), :]
    vw = v_sc[pl.ds(row0, TK), :]

    lane_q = lax.broadcasted_iota(jnp.int32, (TQ, LANES), 1)
    lane_o = lax.broadcasted_iota(jnp.int32, (TQ, MXU_COLS), 1)

    quads = []
    for c4 in range(ATT_WIDTH // MXU_COLS):
        acc = None
        for hh in range(MXU_COLS // HEAD_DIM):
            head = c4 * (MXU_COLS // HEAD_DIM) + hh
            c = head // 2
            in_head = (lane_q < HEAD_DIM) if head % 2 == 0 else (lane_q >= HEAD_DIM)
            qm = jnp.where(in_head, qn[:, c * LANES:(c + 1) * LANES], jnp.zeros((), bf16))
            s = lax.dot_general(qm, kw[:, c * LANES:(c + 1) * LANES],
                                (((1,), (1,)), ((), ())), preferred_element_type=f32)
            s = s + bias_sc[bias_slot, head]
            m = jnp.max(s, axis=-1, keepdims=True)
            p = jnp.exp(s - m)
            l = jnp.sum(p, axis=-1, keepdims=True)
            pv = jnp.dot(p.astype(bf16), vw[:, c4 * MXU_COLS:(c4 + 1) * MXU_COLS],
                         preferred_element_type=f32)
            pv = pv * (1.0 / l)
            if acc is None:
                acc = pv
            else:
                acc = jnp.where(lane_o >= hh * HEAD_DIM, pv, acc)
        quads.append(acc)
    a = jnp.concatenate(quads, axis=-1)
    a_n = _rms(a, ag_ref[...]).astype(bf16)

    u_act = jax.nn.gelu(u)
    vgn = _rms(jax.nn.gelu(vg), sgug_ref[...]).astype(bf16)
    ti = lax.broadcasted_iota(jnp.int32, (GMLP_GROUPS, GMLP_BLOCK, GMLP_BLOCK), 1)
    si = lax.broadcasted_iota(jnp.int32, (GMLP_GROUPS, GMLP_BLOCK, GMLP_BLOCK), 2)
    causal = (ti // CHUNK) >= (si // CHUNK)
    wsp = jnp.where(causal, wsp_ref[...], 0.0).astype(bf16)
    wsp = wsp.reshape(GMLP_GROUPS * GMLP_BLOCK, GMLP_BLOCK)
    lane_g = lax.broadcasted_iota(jnp.int32, (GMLP_BLOCK, MXU_COLS), 1)
    groups_per_quad = MXU_COLS // GMLP_GROUP_DIM
    blocks = []
    for blk in range(TQ // GMLP_BLOCK):
        cols = []
        for c4 in range(GMLP_WIDTH // MXU_COLS):
            lhs = wsp[c4 * groups_per_quad * GMLP_BLOCK:(c4 + 1) * groups_per_quad * GMLP_BLOCK, :]
            rhs = vgn[blk * GMLP_BLOCK:(blk + 1) * GMLP_BLOCK, c4 * MXU_COLS:(c4 + 1) * MXU_COLS]
            res = jnp.dot(lhs, rhs, preferred_element_type=f32)
            acc = res[0:GMLP_BLOCK, :]
            for gg in range(1, groups_per_quad):
                acc = jnp.where(lane_g >= gg * GMLP_GROUP_DIM,
                                res[gg * GMLP_BLOCK:(gg + 1) * GMLP_BLOCK, :], acc)
            cols.append(acc)
        blocks.append(jnp.concatenate(cols, axis=-1) + bsp_ref[...])
    mixed = jnp.concatenate(blocks, axis=0)
    g_n = _rms(u_act * mixed, gg_ref[...]).astype(bf16)

    mix = jnp.concatenate([a_n, g_n], axis=-1)
    o_ref[...] = x + jnp.dot(mix, w_out_ref[...], preferred_element_type=f32)


def _ffn_kernel(x_ref, g_ref, w1_ref, w2_ref, o_ref, act_sc):
    bf16 = jnp.bfloat16
    f32 = jnp.float32
    x = x_ref[...]
    h = _rms(x, g_ref[...]).astype(bf16)
    for j in range(D_FF // FF_CHUNK):
        gate = jnp.dot(h, w1_ref[:, j * FF_CHUNK:(j + 1) * FF_CHUNK], preferred_element_type=f32)
        up = jnp.dot(h, w1_ref[:, D_FF + j * FF_CHUNK:D_FF + (j + 1) * FF_CHUNK],
                     preferred_element_type=f32)
        act_sc[:, j * FF_CHUNK:(j + 1) * FF_CHUNK] = (jax.nn.silu(gate) * up).astype(bf16)
    o_ref[...] = x + jnp.dot(act_sc[...], w2_ref[...], preferred_element_type=f32)


def _const_spec(shape):
    zeros = (0,) * len(shape)
    return pl.BlockSpec(shape, lambda *_: zeros, pipeline_mode=pl.Buffered(1))


def _mix_layer(x, mixg, w_in, qg, kg, bias, sgug, wsp, bsp, ag, gg, w_out):
    B, S, D = x.shape
    grid = (B, S // TQ)
    row_spec = pl.BlockSpec((None, TQ, D), lambda b, t: (b, t, 0))
    return pl.pallas_call(
        _mix_kernel,
        grid=grid,
        in_specs=[
            row_spec,
            _const_spec((1, D)),
            _const_spec((D, IN_WIDTH)),
            _const_spec((1, ATT_WIDTH)),
            _const_spec((1, ATT_WIDTH)),
            _const_spec((ATT_HEADS, MXU_COLS)),
            _const_spec((1, GMLP_WIDTH)),
            _const_spec((GMLP_GROUPS, GMLP_BLOCK, GMLP_BLOCK)),
            _const_spec((GMLP_BLOCK, GMLP_WIDTH)),
            _const_spec((1, ATT_WIDTH)),
            _const_spec((1, GMLP_WIDTH)),
            _const_spec((D, D)),
        ],
        out_specs=row_spec,
        out_shape=jax.ShapeDtypeStruct(x.shape, x.dtype),
        scratch_shapes=[
            pltpu.VMEM((LEFT + S, ATT_WIDTH), jnp.bfloat16),
            pltpu.VMEM((LEFT + S, ATT_WIDTH), jnp.bfloat16),
            pltpu.VMEM((2, ATT_HEADS, TQ, TK), jnp.float32),
        ],
        compiler_params=pltpu.CompilerParams(
            dimension_semantics=("arbitrary", "arbitrary"),
            vmem_limit_bytes=VMEM_LIMIT_BYTES),
        name="mix_layer",
    )(x, mixg, w_in, qg, kg, bias, sgug, wsp, bsp, ag, gg, w_out)


def _ffn_layer(x2d, g, w1, w2):
    T, D = x2d.shape
    row_spec = pl.BlockSpec((TM, D), lambda i: (i, 0))
    return pl.pallas_call(
        _ffn_kernel,
        grid=(T // TM,),
        in_specs=[
            row_spec,
            _const_spec((1, D)),
            _const_spec((D, 2 * D_FF)),
            _const_spec((D_FF, D)),
        ],
        out_specs=row_spec,
        out_shape=jax.ShapeDtypeStruct(x2d.shape, x2d.dtype),
        scratch_shapes=[pltpu.VMEM((TM, D_FF), jnp.bfloat16)],
        compiler_params=pltpu.CompilerParams(
            dimension_semantics=("arbitrary",),
            vmem_limit_bytes=VMEM_LIMIT_BYTES),
        name="ffn_layer",
    )(x2d, g, w1, w2)


NEAR = MAX_REL + CHUNK


def _reversed_rel_table(rel_table):
    near = rel_table[:, ::-1][:, :NEAR]
    far = jnp.broadcast_to(rel_table[:, 2 * MAX_REL:], (ATT_HEADS, MXU_COLS - NEAR))
    return jnp.concatenate([near, far], axis=1).astype(jnp.float32)


def kernel(x, mix_norm_g, w_in, q_norm_g, k_norm_g, rel_bias, sgu_norm_g, w_spatial, b_spatial,
           att_out_norm_g, gmlp_out_norm_g, w_out, ffn_norm_g, w_ffn_in, w_ffn_out):
    B, S, D = x.shape
    depth = w_in.shape[0]
    bf16 = jnp.bfloat16
    for l in range(depth):
        bsp = jnp.repeat(b_spatial[l].T, GMLP_GROUP_DIM, axis=1)
        x = _mix_layer(
            x, mix_norm_g[l][None], w_in[l].astype(bf16),
            jnp.tile(q_norm_g[l], ATT_HEADS)[None], jnp.tile(k_norm_g[l], ATT_HEADS)[None],
            _reversed_rel_table(rel_bias[l]), sgu_norm_g[l][None], w_spatial[l], bsp,
            att_out_norm_g[l][None], gmlp_out_norm_g[l][None], w_out[l].astype(bf16))
        x = _ffn_layer(x.reshape(B * S, D), ffn_norm_g[l][None], w_ffn_in[l].astype(bf16),
                       w_ffn_out[l].astype(bf16)).reshape(B, S, D)
    return x
```

```python
import functools

import jax
import jax.numpy as jnp
from jax import lax
from jax.experimental import pallas as pl
from jax.experimental.pallas import tpu as pltpu

D_MODEL = 1024
CHUNK = 64
ATT_HEADS = 8
HEAD_DIM = 64
ATT_WIDTH = ATT_HEADS * HEAD_DIM
LEFT_CHUNKS = 8
LEFT = LEFT_CHUNKS * CHUNK
MAX_REL = 2 * CHUNK
NEAR = MAX_REL + CHUNK
GMLP_WIDTH = 512
GMLP_GROUPS = 8
GMLP_GROUP_DIM = GMLP_WIDTH // GMLP_GROUPS
GMLP_BLOCK = 128
D_FF = 2816
IN_WIDTH = 3 * ATT_WIDTH + 2 * GMLP_WIDTH
EPS = 1e-6
NEG_INF = -1e30

LANES = 128
MXU_COLS = 256
TQ = 256
TK = LEFT + TQ
FF_CHUNK = 256
VMEM_LIMIT_BYTES = 60 * 1024 * 1024


def _rms(x, g):
    ms = jnp.mean(x * x, axis=-1, keepdims=True)
    return (x * lax.rsqrt(ms + EPS)) * g


def _head_rms(z, gain, post_scale):
    rows = z.shape[0]
    lo = lax.broadcasted_iota(jnp.int32, (rows, LANES), 1) < HEAD_DIM
    outs = []
    for c in range(ATT_WIDTH // LANES):
        zc = z[:, c * LANES:(c + 1) * LANES]
        sq = zc * zc
        ms_lo = jnp.sum(jnp.where(lo, sq, 0.0), axis=-1, keepdims=True) * (1.0 / HEAD_DIM)
        ms_hi = jnp.sum(jnp.where(lo, 0.0, sq), axis=-1, keepdims=True) * (1.0 / HEAD_DIM)
        r = jnp.where(lo, lax.rsqrt(ms_lo + EPS), lax.rsqrt(ms_hi + EPS))
        y = (zc * r) * gain[:, c * LANES:(c + 1) * LANES]
        if post_scale != 1.0:
            y = y * post_scale
        outs.append(y)
    return jnp.concatenate(outs, axis=-1)


def _build_bias(rtab_ref, bias_sc):
    f32 = jnp.float32
    row = lax.broadcasted_iota(jnp.int32, (CHUNK, MXU_COLS), 0)
    lo_half = lax.broadcasted_iota(jnp.int32, (CHUNK, LANES), 1) < CHUNK
    n_kc = TK // CHUNK
    for head in range(ATT_HEADS):
        r = rtab_ref[head:head + 1, :]
        far = rtab_ref[head:head + 1, MXU_COLS - 1:MXU_COLS]
        x0 = jnp.broadcast_to(r, (CHUNK, MXU_COLS))
        for b in range(CHUNK.bit_length() - 1):
            x0 = jnp.where(((row >> b) & 1) == 1, pltpu.roll(x0, 1 << b, axis=1), x0)
        x1 = pltpu.roll(x0, CHUNK, axis=1)
        far_piece = jnp.broadcast_to(far, (CHUNK, LANES))
        neg_piece = jnp.full((CHUNK, LANES), NEG_INF, f32)

        def half(qc, kc):
            rel = kc - qc
            if rel < 0 or rel > LEFT_CHUNKS:
                return neg_piece
            if rel < LEFT_CHUNKS - 2:
                return far_piece
            col = (rel - (LEFT_CHUNKS - 2)) * CHUNK + (qc % 2) * CHUNK
            src = x1 if qc % 2 else x0
            return src[:, (col // LANES) * LANES:(col // LANES + 1) * LANES]

        for qc in range(TQ // CHUNK):
            for vcol in range(n_kc // 2):
                piece = jnp.where(lo_half, half(qc, 2 * vcol), half(qc, 2 * vcol + 1))
                bias_sc[head, qc * CHUNK:(qc + 1) * CHUNK,
                        vcol * LANES:(vcol + 1) * LANES] = piece


def _mix_tile(x, row0, mixg_ref, w_in_ref, qg_ref, kg_ref, sgug_ref, wsp_ref, bsp_ref,
              ag_ref, gg_ref, w_out_ref, k_sc, v_sc, bias_sc):
    bf16 = jnp.bfloat16
    f32 = jnp.float32
    h = _rms(x, mixg_ref[...]).astype(bf16)

    def proj(lo, width):
        return jnp.dot(h, w_in_ref[:, lo:lo + width], preferred_element_type=f32)

    q = proj(0, ATT_WIDTH)
    k = proj(ATT_WIDTH, ATT_WIDTH)
    v = proj(2 * ATT_WIDTH, ATT_WIDTH)
    u = proj(3 * ATT_WIDTH, GMLP_WIDTH)
    vg = proj(3 * ATT_WIDTH + GMLP_WIDTH, GMLP_WIDTH)

    qn = _head_rms(q, qg_ref[...], HEAD_DIM ** -0.5).astype(bf16)
    kn = _head_rms(k, kg_ref[...], 1.0).astype(bf16)
    k_sc[pl.ds(row0 + LEFT, TQ), :] = kn
    v_sc[pl.ds(row0 + LEFT, TQ), :] = v.astype(bf16)
    kw = k_sc[pl.ds(row0, TK), :]
    vw = v_sc[pl.ds(row0, TK), :]

    pad_mask = jnp.where(lax.broadcasted_iota(jnp.int32, (1, TK), 1) >= LEFT - row0,
                         0.0, NEG_INF)
    lane_q = lax.broadcasted_iota(jnp.int32, (TQ, LANES), 1)
    lane_o = lax.broadcasted_iota(jnp.int32, (TQ, MXU_COLS), 1)

    quads = []
    for c4 in range(ATT_WIDTH // MXU_COLS):
        acc = None
        for hh in range(MXU_COLS // HEAD_DIM):
            head = c4 * (MXU_COLS // HEAD_DIM) + hh
            c = head // 2
            in_head = (lane_q < HEAD_DIM) if head % 2 == 0 else (lane_q >= HEAD_DIM)
            qm = jnp.where(in_head, qn[:, c * LANES:(c + 1) * LANES], jnp.zeros((), bf16))
            s = lax.dot_general(qm, kw[:, c * LANES:(c + 1) * LANES],
                                (((1,), (1,)), ((), ())), preferred_element_type=f32)
            s = (s + bias_sc[head]) + pad_mask
            m = jnp.max(s, axis=-1, keepdims=True)
            p = jnp.exp(s - m)
            l = jnp.sum(p, axis=-1, keepdims=True)
            pv = jnp.dot(p.astype(bf16), vw[:, c4 * MXU_COLS:(c4 + 1) * MXU_COLS],
                         preferred_element_type=f32)
            pv = pv * (1.0 / l)
            if acc is None:
                acc = pv
            else:
                acc = jnp.where(lane_o >= hh * HEAD_DIM, pv, acc)
        quads.append(acc)
    a = jnp.concatenate(quads, axis=-1)
    a_n = _rms(a, ag_ref[...]).astype(bf16)

    u_act = jax.nn.gelu(u)
    vgn = _rms(jax.nn.gelu(vg), sgug_ref[...]).astype(bf16)
    ti = lax.broadcasted_iota(jnp.int32, (GMLP_GROUPS, GMLP_BLOCK, GMLP_BLOCK), 1)
    si = lax.broadcasted_iota(jnp.int32, (GMLP_GROUPS, GMLP_BLOCK, GMLP_BLOCK), 2)
    causal = (ti // CHUNK) >= (si // CHUNK)
    wsp = jnp.where(causal, wsp_ref[...], 0.0).astype(bf16)
    wsp = wsp.reshape(GMLP_GROUPS * GMLP_BLOCK, GMLP_BLOCK)
    lane_g = lax.broadcasted_iota(jnp.int32, (GMLP_BLOCK, MXU_COLS), 1)
    groups_per_quad = MXU_COLS // GMLP_GROUP_DIM
    blocks = []
    for blk in range(TQ // GMLP_BLOCK):
        cols = []
        for c4 in range(GMLP_WIDTH // MXU_COLS):
            lhs = wsp[c4 * groups_per_quad * GMLP_BLOCK:(c4 + 1) * groups_per_quad * GMLP_BLOCK, :]
            rhs = vgn[blk * GMLP_BLOCK:(blk + 1) * GMLP_BLOCK, c4 * MXU_COLS:(c4 + 1) * MXU_COLS]
            res = jnp.dot(lhs, rhs, preferred_element_type=f32)
            acc = res[0:GMLP_BLOCK, :]
            for gg in range(1, groups_per_quad):
                acc = jnp.where(lane_g >= gg * GMLP_GROUP_DIM,
                                res[gg * GMLP_BLOCK:(gg + 1) * GMLP_BLOCK, :], acc)
            cols.append(acc)
        blocks.append(jnp.concatenate(cols, axis=-1) + bsp_ref[...])
    mixed = jnp.concatenate(blocks, axis=0)
    g_n = _rms(u_act * mixed, gg_ref[...]).astype(bf16)

    mix = jnp.concatenate([a_n, g_n], axis=-1)
    return x + jnp.dot(mix, w_out_ref[...], preferred_element_type=f32)


def _ffn_tile(y, g_ref, w1_ref, w2_ref, act_sc):
    bf16 = jnp.bfloat16
    f32 = jnp.float32
    h = _rms(y, g_ref[...]).astype(bf16)
    for j in range(D_FF // FF_CHUNK):
        gate = jnp.dot(h, w1_ref[:, j * FF_CHUNK:(j + 1) * FF_CHUNK], preferred_element_type=f32)
        up = jnp.dot(h, w1_ref[:, D_FF + j * FF_CHUNK:D_FF + (j + 1) * FF_CHUNK],
                     preferred_element_type=f32)
        act_sc[:, j * FF_CHUNK:(j + 1) * FF_CHUNK] = (jax.nn.silu(gate) * up).astype(bf16)
    return y + jnp.dot(act_sc[...], w2_ref[...], preferred_element_type=f32)


def _layer_kernel(tiles_per_row, n_tiles,
                  x_ref, mixg_ref, w_in_ref, qg_ref, kg_ref, rtab_ref, sgug_ref, wsp_ref,
                  bsp_ref, ag_ref, gg_ref, w_out_ref, ffng_ref, w1_ref, w2_ref,
                  o_ref, k_sc, v_sc, bias_sc, y_sc, act_sc):
    g = pl.program_id(0)
    t = lax.rem(jnp.minimum(g, n_tiles - 1), tiles_per_row)
    row0 = pl.multiple_of(t * TQ, TQ)

    @pl.when(g == 0)
    def _():
        _build_bias(rtab_ref, bias_sc)
        y_sc[...] = jnp.zeros(y_sc.shape, y_sc.dtype)

    @pl.when(t == 0)
    def _():
        k_sc[0:LEFT, :] = jnp.zeros((LEFT, ATT_WIDTH), k_sc.dtype)
        v_sc[0:LEFT, :] = jnp.zeros((LEFT, ATT_WIDTH), v_sc.dtype)

    y_prev = y_sc[...]
    o_ref[...] = _ffn_tile(y_prev, ffng_ref, w1_ref, w2_ref, act_sc)
    y_sc[...] = _mix_tile(x_ref[...], row0, mixg_ref, w_in_ref, qg_ref, kg_ref, sgug_ref,
                          wsp_ref, bsp_ref, ag_ref, gg_ref, w_out_ref, k_sc, v_sc, bias_sc)


def _const_spec(shape):
    zeros = (0,) * len(shape)
    return pl.BlockSpec(shape, lambda g: zeros, pipeline_mode=pl.Buffered(1))


def _layer_spec(layer, shape):
    zeros = (0,) * len(shape)
    return pl.BlockSpec((None,) + shape, lambda g: (layer,) + zeros,
                        pipeline_mode=pl.Buffered(1))


def _layer(layer, x, mixg, w_in, qg, kg, rtab, sgug, wsp, bsp, ag, gg, w_out, ffng, w1, w2):
    B, S, D = x.shape
    tiles_per_row = S // TQ
    n_tiles = B * tiles_per_row

    def in_tile(g):
        gi = jnp.minimum(g, n_tiles - 1)
        return (gi // tiles_per_row, gi % tiles_per_row, 0)

    def out_tile(g):
        go = jnp.maximum(g - 1, 0)
        return (go // tiles_per_row, go % tiles_per_row, 0)

    return pl.pallas_call(
        functools.partial(_layer_kernel, tiles_per_row, n_tiles),
        grid=(n_tiles + 1,),
        in_specs=[
            pl.BlockSpec((None, TQ, D), in_tile),
            _layer_spec(layer, (1, D)),
            _layer_spec(layer, (D, IN_WIDTH)),
            _layer_spec(layer, (1, ATT_WIDTH)),
            _layer_spec(layer, (1, ATT_WIDTH)),
            _layer_spec(layer, (ATT_HEADS, MXU_COLS)),
            _layer_spec(layer, (1, GMLP_WIDTH)),
            _layer_spec(layer, (GMLP_GROUPS, GMLP_BLOCK, GMLP_BLOCK)),
            _layer_spec(layer, (GMLP_BLOCK, GMLP_WIDTH)),
            _layer_spec(layer, (1, ATT_WIDTH)),
            _layer_spec(layer, (1, GMLP_WIDTH)),
            _layer_spec(layer, (D, D)),
            _layer_spec(layer, (1, D)),
            _layer_spec(layer, (D, 2 * D_FF)),
            _layer_spec(layer, (D_FF, D)),
        ],
        out_specs=pl.BlockSpec((None, TQ, D), out_tile),
        out_shape=jax.ShapeDtypeStruct(x.shape, x.dtype),
        scratch_shapes=[
            pltpu.VMEM((LEFT + S, ATT_WIDTH), jnp.bfloat16),
            pltpu.VMEM((LEFT + S, ATT_WIDTH), jnp.bfloat16),
            pltpu.VMEM((ATT_HEADS, TQ, TK), jnp.float32),
            pltpu.VMEM((TQ, D), jnp.float32),
            pltpu.VMEM((TQ, D_FF), jnp.bfloat16),
        ],
        compiler_params=pltpu.CompilerParams(
            dimension_semantics=("arbitrary",),
            vmem_limit_bytes=VMEM_LIMIT_BYTES),
        name="layer",
    )(x, mixg, w_in, qg, kg, rtab, sgug, wsp, bsp, ag, gg, w_out, ffng, w1, w2)


def _reversed_rel_table(rel_bias):
    near = rel_bias[:, :, ::-1][:, :, :NEAR]
    far = jnp.broadcast_to(rel_bias[:, :, 2 * MAX_REL:], near.shape[:2] + (MXU_COLS - NEAR,))
    return jnp.concatenate([near, far], axis=2).astype(jnp.float32)


def kernel(x, mix_norm_g, w_in, q_norm_g, k_norm_g, rel_bias, sgu_norm_g, w_spatial, b_spatial,
           att_out_norm_g, gmlp_out_norm_g, w_out, ffn_norm_g, w_ffn_in, w_ffn_out):
    depth = w_in.shape[0]
    bf16 = jnp.bfloat16
    row = lambda p: p[:, None, :]
    params = (
        row(mix_norm_g), w_in.astype(bf16),
        row(jnp.tile(q_norm_g, (1, ATT_HEADS))), row(jnp.tile(k_norm_g, (1, ATT_HEADS))),
        _reversed_rel_table(rel_bias), row(sgu_norm_g), w_spatial,
        jnp.repeat(jnp.swapaxes(b_spatial, 1, 2), GMLP_GROUP_DIM, axis=2),
        row(att_out_norm_g), row(gmlp_out_norm_g), w_out.astype(bf16),
        row(ffn_norm_g), w_ffn_in.astype(bf16), w_ffn_out.astype(bf16))
    for l in range(depth):
        x = _layer(l, x, *params)
    return x
```

```python
import functools

import jax
import jax.numpy as jnp
from jax import lax
from jax.experimental import pallas as pl
from jax.experimental.pallas import tpu as pltpu

D_MODEL = 1024
CHUNK = 64
ATT_HEADS = 8
HEAD_DIM = 64
ATT_WIDTH = ATT_HEADS * HEAD_DIM
LEFT_CHUNKS = 8
LEFT = LEFT_CHUNKS * CHUNK
MAX_REL = 2 * CHUNK
NEAR = MAX_REL + CHUNK
GMLP_WIDTH = 512
GMLP_GROUPS = 8
GMLP_GROUP_DIM = GMLP_WIDTH // GMLP_GROUPS
GMLP_BLOCK = 128
D_FF = 2816
IN_WIDTH = 3 * ATT_WIDTH + 2 * GMLP_WIDTH
EPS = 1e-6
NEG_INF = -1e30

LANES = 128
MXU_COLS = 256
TQ = 256
TK = LEFT + TQ
FF_CHUNK = 256
VMEM_LIMIT_BYTES = 60 * 1024 * 1024


def _rms(x, g):
    ms = jnp.mean(x * x, axis=-1, keepdims=True)
    return (x * lax.rsqrt(ms + EPS)) * g


def _head_rms(z, gain, post_scale):
    rows = z.shape[0]
    lo = lax.broadcasted_iota(jnp.int32, (rows, LANES), 1) < HEAD_DIM
    outs = []
    for c in range(ATT_WIDTH // LANES):
        zc = z[:, c * LANES:(c + 1) * LANES]
        sq = zc * zc
        ms_lo = jnp.sum(jnp.where(lo, sq, 0.0), axis=-1, keepdims=True) * (1.0 / HEAD_DIM)
        ms_hi = jnp.sum(jnp.where(lo, 0.0, sq), axis=-1, keepdims=True) * (1.0 / HEAD_DIM)
        r = jnp.where(lo, lax.rsqrt(ms_lo + EPS), lax.rsqrt(ms_hi + EPS))
        y = (zc * r) * gain[:, c * LANES:(c + 1) * LANES]
        if post_scale != 1.0:
            y = y * post_scale
        outs.append(y)
    return jnp.concatenate(outs, axis=-1)


def _build_bias(rtab_ref, bias_sc):
    f32 = jnp.float32
    row = lax.broadcasted_iota(jnp.int32, (CHUNK, MXU_COLS), 0)
    lo_half = lax.broadcasted_iota(jnp.int32, (CHUNK, LANES), 1) < CHUNK
    n_kc = TK // CHUNK
    for head in range(ATT_HEADS):
        r = rtab_ref[head:head + 1, :]
        far = rtab_ref[head:head + 1, MXU_COLS - 1:MXU_COLS]
        x0 = jnp.broadcast_to(r, (CHUNK, MXU_COLS))
        for b in range(CHUNK.bit_length() - 1):
            x0 = jnp.where(((row >> b) & 1) == 1, pltpu.roll(x0, 1 << b, axis=1), x0)
        x1 = pltpu.roll(x0, CHUNK, axis=1)
        far_piece = jnp.broadcast_to(far, (CHUNK, LANES))
        neg_piece = jnp.full((CHUNK, LANES), NEG_INF, f32)

        def half(qc, kc):
            rel = kc - qc
            if rel < 0 or rel > LEFT_CHUNKS:
                return neg_piece
            if rel < LEFT_CHUNKS - 2:
                return far_piece
            col = (rel - (LEFT_CHUNKS - 2)) * CHUNK + (qc % 2) * CHUNK
            src = x1 if qc % 2 else x0
            return src[:, (col // LANES) * LANES:(col // LANES + 1) * LANES]

        for qc in range(TQ // CHUNK):
            for vcol in range(n_kc // 2):
                piece = jnp.where(lo_half, half(qc, 2 * vcol), half(qc, 2 * vcol + 1))
                bias_sc[head, qc * CHUNK:(qc + 1) * CHUNK,
                        vcol * LANES:(vcol + 1) * LANES] = piece


def _step(x, y_prev, row0, mixg_ref, w_in_ref, qg_ref, kg_ref, sgug_ref, wsp_ref, bsp_ref,
          ag_ref, gg_ref, w_out_ref, ffng_ref, w1_ref, w2_ref, k_sc, v_sc, bias_sc, act_sc):
    bf16 = jnp.bfloat16
    f32 = jnp.float32
    h = _rms(x, mixg_ref[...]).astype(bf16)
    hf = _rms(y_prev, ffng_ref[...]).astype(bf16)

    def proj(lo, width):
        return jnp.dot(h, w_in_ref[:, lo:lo + width], preferred_element_type=f32)

    def ffn_chunk(j):
        gate = jnp.dot(hf, w1_ref[:, j * FF_CHUNK:(j + 1) * FF_CHUNK], preferred_element_type=f32)
        up = jnp.dot(hf, w1_ref[:, D_FF + j * FF_CHUNK:D_FF + (j + 1) * FF_CHUNK],
                     preferred_element_type=f32)
        act_sc[:, j * FF_CHUNK:(j + 1) * FF_CHUNK] = (jax.nn.silu(gate) * up).astype(bf16)

    q = proj(0, ATT_WIDTH)
    k = proj(ATT_WIDTH, ATT_WIDTH)
    v = proj(2 * ATT_WIDTH, ATT_WIDTH)
    u = proj(3 * ATT_WIDTH, GMLP_WIDTH)
    vg = proj(3 * ATT_WIDTH + GMLP_WIDTH, GMLP_WIDTH)

    qn = _head_rms(q, qg_ref[...], HEAD_DIM ** -0.5).astype(bf16)
    kn = _head_rms(k, kg_ref[...], 1.0).astype(bf16)
    k_sc[pl.ds(row0 + LEFT, TQ), :] = kn
    v_sc[pl.ds(row0 + LEFT, TQ), :] = v.astype(bf16)
    kw = k_sc[pl.ds(row0, TK), :]
    vw = v_sc[pl.ds(row0, TK), :]

    pad_mask = jnp.where(lax.broadcasted_iota(jnp.int32, (1, TK), 1) >= LEFT - row0,
                         0.0, NEG_INF)
    lane_q = lax.broadcasted_iota(jnp.int32, (TQ, LANES), 1)
    lane_o = lax.broadcasted_iota(jnp.int32, (TQ, MXU_COLS), 1)
    heads_per_quad = MXU_COLS // HEAD_DIM

    def scores(head):
        c = head // 2
        in_head = (lane_q < HEAD_DIM) if head % 2 == 0 else (lane_q >= HEAD_DIM)
        qm = jnp.where(in_head, qn[:, c * LANES:(c + 1) * LANES], jnp.zeros((), bf16))
        s = lax.dot_general(qm, kw[:, c * LANES:(c + 1) * LANES],
                            (((1,), (1,)), ((), ())), preferred_element_type=f32)
        return (s + bias_sc[head]) + pad_mask

    def attend(head, s):
        c4 = head // heads_per_quad
        m = jnp.max(s, axis=-1, keepdims=True)
        p = jnp.exp(s - m)
        l = jnp.sum(p, axis=-1, keepdims=True)
        pv = jnp.dot(p.astype(bf16), vw[:, c4 * MXU_COLS:(c4 + 1) * MXU_COLS],
                     preferred_element_type=f32)
        return pv * (1.0 / l)

    quads = []
    acc = None
    s_next = scores(0)
    for head in range(ATT_HEADS):
        s_cur = s_next
        if head + 1 < ATT_HEADS:
            s_next = scores(head + 1)
        ffn_chunk(head)
        pv = attend(head, s_cur)
        hh = head % heads_per_quad
        acc = pv if hh == 0 else jnp.where(lane_o >= hh * HEAD_DIM, pv, acc)
        if hh == heads_per_quad - 1:
            quads.append(acc)
    a = jnp.concatenate(quads, axis=-1)
    a_n = _rms(a, ag_ref[...]).astype(bf16)

    u_act = jax.nn.gelu(u)
    vgn = _rms(jax.nn.gelu(vg), sgug_ref[...]).astype(bf16)
    ti = lax.broadcasted_iota(jnp.int32, (GMLP_GROUPS, GMLP_BLOCK, GMLP_BLOCK), 1)
    si = lax.broadcasted_iota(jnp.int32, (GMLP_GROUPS, GMLP_BLOCK, GMLP_BLOCK), 2)
    causal = (ti // CHUNK) >= (si // CHUNK)
    wsp = jnp.where(causal, wsp_ref[...], 0.0).astype(bf16)
    wsp = wsp.reshape(GMLP_GROUPS * GMLP_BLOCK, GMLP_BLOCK)
    lane_g = lax.broadcasted_iota(jnp.int32, (GMLP_BLOCK, MXU_COLS), 1)
    groups_per_quad = MXU_COLS // GMLP_GROUP_DIM
    next_chunk = ATT_HEADS
    blocks = []
    for blk in range(TQ // GMLP_BLOCK):
        cols = []
        for c4 in range(GMLP_WIDTH // MXU_COLS):
            lhs = wsp[c4 * groups_per_quad * GMLP_BLOCK:(c4 + 1) * groups_per_quad * GMLP_BLOCK, :]
            rhs = vgn[blk * GMLP_BLOCK:(blk + 1) * GMLP_BLOCK, c4 * MXU_COLS:(c4 + 1) * MXU_COLS]
            res = jnp.dot(lhs, rhs, preferred_element_type=f32)
            acc = res[0:GMLP_BLOCK, :]
            for gg in range(1, groups_per_quad):
                acc = jnp.where(lane_g >= gg * GMLP_GROUP_DIM,
                                res[gg * GMLP_BLOCK:(gg + 1) * GMLP_BLOCK, :], acc)
            cols.append(acc)
            if next_chunk < D_FF // FF_CHUNK:
                ffn_chunk(next_chunk)
                next_chunk += 1
        blocks.append(jnp.concatenate(cols, axis=-1) + bsp_ref[...])
    while next_chunk < D_FF // FF_CHUNK:
        ffn_chunk(next_chunk)
        next_chunk += 1
    mixed = jnp.concatenate(blocks, axis=0)
    g_n = _rms(u_act * mixed, gg_ref[...]).astype(bf16)

    ffn_out = y_prev + jnp.dot(act_sc[...], w2_ref[...], preferred_element_type=f32)
    mix = jnp.concatenate([a_n, g_n], axis=-1)
    mix_out = x + jnp.dot(mix, w_out_ref[...], preferred_element_type=f32)
    return mix_out, ffn_out


def _layer_kernel(tiles_per_row, n_tiles,
                  x_ref, mixg_ref, w_in_ref, qg_ref, kg_ref, rtab_ref, sgug_ref, wsp_ref,
                  bsp_ref, ag_ref, gg_ref, w_out_ref, ffng_ref, w1_ref, w2_ref,
                  o_ref, k_sc, v_sc, bias_sc, y_sc, act_sc):
    g = pl.program_id(0)
    t = lax.rem(jnp.minimum(g, n_tiles - 1), tiles_per_row)
    row0 = pl.multiple_of(t * TQ, TQ)

    @pl.when(g == 0)
    def _():
        _build_bias(rtab_ref, bias_sc)
        y_sc[...] = jnp.zeros(y_sc.shape, y_sc.dtype)

    @pl.when(t == 0)
    def _():
        k_sc[0:LEFT, :] = jnp.zeros((LEFT, ATT_WIDTH), k_sc.dtype)
        v_sc[0:LEFT, :] = jnp.zeros((LEFT, ATT_WIDTH), v_sc.dtype)

    mix_out, ffn_out = _step(
        x_ref[...], y_sc[...], row0, mixg_ref, w_in_ref, qg_ref, kg_ref, sgug_ref, wsp_ref,
        bsp_ref, ag_ref, gg_ref, w_out_ref, ffng_ref, w1_ref, w2_ref, k_sc, v_sc, bias_sc, act_sc)
    o_ref[...] = ffn_out
    y_sc[...] = mix_out


def _layer_spec(layer, shape):
    zeros = (0,) * len(shape)
    return pl.BlockSpec((None,) + shape, lambda g: (layer,) + zeros,
                        pipeline_mode=pl.Buffered(1))


def _layer(layer, x, mixg, w_in, qg, kg, rtab, sgug, wsp, bsp, ag, gg, w_out, ffng, w1, w2):
    B, S, D = x.shape
    tiles_per_row = S // TQ
    n_tiles = B * tiles_per_row

    def in_tile(g):
        gi = jnp.minimum(g, n_tiles - 1)
        return (gi // tiles_per_row, gi % tiles_per_row, 0)

    def out_tile(g):
        go = jnp.maximum(g - 1, 0)
        return (go // tiles_per_row, go % tiles_per_row, 0)

    return pl.pallas_call(
        functools.partial(_layer_kernel, tiles_per_row, n_tiles),
        grid=(n_tiles + 1,),
        in_specs=[
            pl.BlockSpec((None, TQ, D), in_tile),
            _layer_spec(layer, (1, D)),
            _layer_spec(layer, (D, IN_WIDTH)),
            _layer_spec(layer, (1, ATT_WIDTH)),
            _layer_spec(layer, (1, ATT_WIDTH)),
            _layer_spec(layer, (ATT_HEADS, MXU_COLS)),
            _layer_spec(layer, (1, GMLP_WIDTH)),
            _layer_spec(layer, (GMLP_GROUPS, GMLP_BLOCK, GMLP_BLOCK)),
            _layer_spec(layer, (GMLP_BLOCK, GMLP_WIDTH)),
            _layer_spec(layer, (1, ATT_WIDTH)),
            _layer_spec(layer, (1, GMLP_WIDTH)),
            _layer_spec(layer, (D, D)),
            _layer_spec(layer, (1, D)),
            _layer_spec(layer, (D, 2 * D_FF)),
            _layer_spec(layer, (D_FF, D)),
        ],
        out_specs=pl.BlockSpec((None, TQ, D), out_tile),
        out_shape=jax.ShapeDtypeStruct(x.shape, x.dtype),
        scratch_shapes=[
            pltpu.VMEM((LEFT + S, ATT_WIDTH), jnp.bfloat16),
            pltpu.VMEM((LEFT + S, ATT_WIDTH), jnp.bfloat16),
            pltpu.VMEM((ATT_HEADS, TQ, TK), jnp.float32),
            pltpu.VMEM((TQ, D), jnp.float32),
            pltpu.VMEM((TQ, D_FF), jnp.bfloat16),
        ],
        compiler_params=pltpu.CompilerParams(
            dimension_semantics=("arbitrary",),
            vmem_limit_bytes=VMEM_LIMIT_BYTES),
        name="layer",
    )(x, mixg, w_in, qg, kg, rtab, sgug, wsp, bsp, ag, gg, w_out, ffng, w1, w2)


def _reversed_rel_table(rel_bias):
    near = rel_bias[:, :, ::-1][:, :, :NEAR]
    far = jnp.broadcast_to(rel_bias[:, :, 2 * MAX_REL:], near.shape[:2] + (MXU_COLS - NEAR,))
    return jnp.concatenate([near, far], axis=2).astype(jnp.float32)


def kernel(x, mix_norm_g, w_in, q_norm_g, k_norm_g, rel_bias, sgu_norm_g, w_spatial, b_spatial,
           att_out_norm_g, gmlp_out_norm_g, w_out, ffn_norm_g, w_ffn_in, w_ffn_out):
    depth = w_in.shape[0]
    bf16 = jnp.bfloat16
    row = lambda p: p[:, None, :]
    params = (
        row(mix_norm_g), w_in.astype(bf16),
        row(jnp.tile(q_norm_g, (1, ATT_HEADS))), row(jnp.tile(k_norm_g, (1, ATT_HEADS))),
        _reversed_rel_table(rel_bias), row(sgu_norm_g), w_spatial,
        jnp.repeat(jnp.swapaxes(b_spatial, 1, 2), GMLP_GROUP_DIM, axis=2),
        row(att_out_norm_g), row(gmlp_out_norm_g), w_out.astype(bf16),
        row(ffn_norm_g), w_ffn_in.astype(bf16), w_ffn_out.astype(bf16))
    for l in range(depth):
        x = _layer(l, x, *params)
    return x
```

```python
import functools

import jax
import jax.numpy as jnp
from jax import lax
from jax.experimental import pallas as pl
from jax.experimental.pallas import tpu as pltpu

D_MODEL = 1024
CHUNK = 64
ATT_HEADS = 8
HEAD_DIM = 64
ATT_WIDTH = ATT_HEADS * HEAD_DIM
LEFT_CHUNKS = 8
LEFT = LEFT_CHUNKS * CHUNK
MAX_REL = 2 * CHUNK
NEAR = MAX_REL + CHUNK
GMLP_WIDTH = 512
GMLP_GROUPS = 8
GMLP_GROUP_DIM = GMLP_WIDTH // GMLP_GROUPS
GMLP_BLOCK = 128
D_FF = 2816
IN_WIDTH = 3 * ATT_WIDTH + 2 * GMLP_WIDTH
EPS = 1e-6
NEG_INF = -1e30

LANES = 128
MXU_COLS = 256
TQ = 256
TK = LEFT + TQ
FF_CHUNK = 256
FFN_CHUNKS_FIRST = 2
VMEM_LIMIT_BYTES = 60 * 1024 * 1024


def _rms(x, g):
    ms = jnp.mean(x * x, axis=-1, keepdims=True)
    return (x * lax.rsqrt(ms + EPS)) * g


def _head_rms(z, gain, post_scale):
    rows = z.shape[0]
    lo = lax.broadcasted_iota(jnp.int32, (rows, LANES), 1) < HEAD_DIM
    outs = []
    for c in range(ATT_WIDTH // LANES):
        zc = z[:, c * LANES:(c + 1) * LANES]
        sq = zc * zc
        ms_lo = jnp.sum(jnp.where(lo, sq, 0.0), axis=-1, keepdims=True) * (1.0 / HEAD_DIM)
        ms_hi = jnp.sum(jnp.where(lo, 0.0, sq), axis=-1, keepdims=True) * (1.0 / HEAD_DIM)
        r = jnp.where(lo, lax.rsqrt(ms_lo + EPS), lax.rsqrt(ms_hi + EPS))
        y = (zc * r) * gain[:, c * LANES:(c + 1) * LANES]
        if post_scale != 1.0:
            y = y * post_scale
        outs.append(y)
    return jnp.concatenate(outs, axis=-1)


def _build_bias(rtab_ref, bias_sc):
    f32 = jnp.float32
    row = lax.broadcasted_iota(jnp.int32, (CHUNK, MXU_COLS), 0)
    lo_half = lax.broadcasted_iota(jnp.int32, (CHUNK, LANES), 1) < CHUNK
    n_kc = TK // CHUNK
    for head in range(ATT_HEADS):
        r = rtab_ref[head:head + 1, :]
        far = rtab_ref[head:head + 1, MXU_COLS - 1:MXU_COLS]
        x0 = jnp.broadcast_to(r, (CHUNK, MXU_COLS))
        for b in range(CHUNK.bit_length() - 1):
            x0 = jnp.where(((row >> b) & 1) == 1, pltpu.roll(x0, 1 << b, axis=1), x0)
        x1 = pltpu.roll(x0, CHUNK, axis=1)
        far_piece = jnp.broadcast_to(far, (CHUNK, LANES))
        neg_piece = jnp.full((CHUNK, LANES), NEG_INF, f32)

        def half(qc, kc):
            rel = kc - qc
            if rel < 0 or rel > LEFT_CHUNKS:
                return neg_piece
            if rel < LEFT_CHUNKS - 2:
                return far_piece
            col = (rel - (LEFT_CHUNKS - 2)) * CHUNK + (qc % 2) * CHUNK
            src = x1 if qc % 2 else x0
            return src[:, (col // LANES) * LANES:(col // LANES + 1) * LANES]

        for qc in range(TQ // CHUNK):
            for vcol in range(n_kc // 2):
                piece = jnp.where(lo_half, half(qc, 2 * vcol), half(qc, 2 * vcol + 1))
                bias_sc[head, qc * CHUNK:(qc + 1) * CHUNK,
                        vcol * LANES:(vcol + 1) * LANES] = piece


def _step(x, y_prev, hf, row0, mixg_ref, w_in_ref, qg_ref, kg_ref, sgug_ref, wsp_ref, bsp_ref,
          ag_ref, gg_ref, w_out_ref, ffng_ref, w1_ref, w2_ref, k_sc, v_sc, bias_sc, act_sc):
    bf16 = jnp.bfloat16
    f32 = jnp.float32
    h = _rms(x, mixg_ref[...]).astype(bf16)

    def proj(lo, width):
        return jnp.dot(h, w_in_ref[:, lo:lo + width], preferred_element_type=f32)

    pending = list(range(D_FF // FF_CHUNK))

    def ffn_chunks(n):
        for j in pending[:n]:
            gate = jnp.dot(hf, w1_ref[:, j * FF_CHUNK:(j + 1) * FF_CHUNK],
                           preferred_element_type=f32)
            up = jnp.dot(hf, w1_ref[:, D_FF + j * FF_CHUNK:D_FF + (j + 1) * FF_CHUNK],
                         preferred_element_type=f32)
            act_sc[:, j * FF_CHUNK:(j + 1) * FF_CHUNK] = (jax.nn.silu(gate) * up).astype(bf16)
        del pending[:n]

    ffn_chunks(FFN_CHUNKS_FIRST)
    q = proj(0, ATT_WIDTH)
    k = proj(ATT_WIDTH, ATT_WIDTH)
    v = proj(2 * ATT_WIDTH, ATT_WIDTH)
    u = proj(3 * ATT_WIDTH, GMLP_WIDTH)
    vg = proj(3 * ATT_WIDTH + GMLP_WIDTH, GMLP_WIDTH)

    qn = _head_rms(q, qg_ref[...], HEAD_DIM ** -0.5).astype(bf16)
    kn = _head_rms(k, kg_ref[...], 1.0).astype(bf16)
    k_sc[pl.ds(row0 + LEFT, TQ), :] = kn
    v_sc[pl.ds(row0 + LEFT, TQ), :] = v.astype(bf16)
    kw = k_sc[pl.ds(row0, TK), :]
    vw = v_sc[pl.ds(row0, TK), :]

    pad_mask = jnp.where(lax.broadcasted_iota(jnp.int32, (1, TK), 1) >= LEFT - row0,
                         0.0, NEG_INF)
    lane_q = lax.broadcasted_iota(jnp.int32, (TQ, LANES), 1)
    lane_o = lax.broadcasted_iota(jnp.int32, (TQ, MXU_COLS), 1)
    heads_per_quad = MXU_COLS // HEAD_DIM

    def scores(head):
        c = head // 2
        in_head = (lane_q < HEAD_DIM) if head % 2 == 0 else (lane_q >= HEAD_DIM)
        qm = jnp.where(in_head, qn[:, c * LANES:(c + 1) * LANES], jnp.zeros((), bf16))
        s = lax.dot_general(qm, kw[:, c * LANES:(c + 1) * LANES],
                            (((1,), (1,)), ((), ())), preferred_element_type=f32)
        return (s + bias_sc[head]) + pad_mask

    def attend(head, s):
        c4 = head // heads_per_quad
        m = jnp.max(s, axis=-1, keepdims=True)
        p = jnp.exp(s - m)
        l = jnp.sum(p, axis=-1, keepdims=True)
        pv = jnp.dot(p.astype(bf16), vw[:, c4 * MXU_COLS:(c4 + 1) * MXU_COLS],
                     preferred_element_type=f32)
        return pv * (1.0 / l)

    quads = []
    acc = None
    s_next = scores(0)
    for head in range(ATT_HEADS):
        s_cur = s_next
        if head + 1 < ATT_HEADS:
            s_next = scores(head + 1)
        ffn_chunks(1)
        pv = attend(head, s_cur)
        hh = head % heads_per_quad
        acc = pv if hh == 0 else jnp.where(lane_o >= hh * HEAD_DIM, pv, acc)
        if hh == heads_per_quad - 1:
            quads.append(acc)
    a = jnp.concatenate(quads, axis=-1)
    a_n = _rms(a, ag_ref[...]).astype(bf16)

    u_act = jax.nn.gelu(u)
    vgn = _rms(jax.nn.gelu(vg), sgug_ref[...]).astype(bf16)
    ti = lax.broadcasted_iota(jnp.int32, (GMLP_GROUPS, GMLP_BLOCK, GMLP_BLOCK), 1)
    si = lax.broadcasted_iota(jnp.int32, (GMLP_GROUPS, GMLP_BLOCK, GMLP_BLOCK), 2)
    causal = (ti // CHUNK) >= (si // CHUNK)
    wsp = jnp.where(causal, wsp_ref[...], 0.0).astype(bf16)
    wsp = wsp.reshape(GMLP_GROUPS * GMLP_BLOCK, GMLP_BLOCK)
    lane_g = lax.broadcasted_iota(jnp.int32, (GMLP_BLOCK, MXU_COLS), 1)
    groups_per_quad = MXU_COLS // GMLP_GROUP_DIM
    blocks = []
    for blk in range(TQ // GMLP_BLOCK):
        cols = []
        for c4 in range(GMLP_WIDTH // MXU_COLS):
            lhs = wsp[c4 * groups_per_quad * GMLP_BLOCK:(c4 + 1) * groups_per_quad * GMLP_BLOCK, :]
            rhs = vgn[blk * GMLP_BLOCK:(blk + 1) * GMLP_BLOCK, c4 * MXU_COLS:(c4 + 1) * MXU_COLS]
            res = jnp.dot(lhs, rhs, preferred_element_type=f32)
            acc = res[0:GMLP_BLOCK, :]
            for gg in range(1, groups_per_quad):
                acc = jnp.where(lane_g >= gg * GMLP_GROUP_DIM,
                                res[gg * GMLP_BLOCK:(gg + 1) * GMLP_BLOCK, :], acc)
            cols.append(acc)
        blocks.append(jnp.concatenate(cols, axis=-1) + bsp_ref[...])
    ffn_chunks(len(pending))
    mixed = jnp.concatenate(blocks, axis=0)
    g_n = _rms(u_act * mixed, gg_ref[...]).astype(bf16)

    mix = jnp.concatenate([a_n, g_n], axis=-1)
    mix_out = x + jnp.dot(mix, w_out_ref[...], preferred_element_type=f32)
    ffn_out = y_prev + jnp.dot(act_sc[...], w2_ref[...], preferred_element_type=f32)
    hf_next = _rms(mix_out, ffng_ref[...]).astype(bf16)
    return mix_out, hf_next, ffn_out


def _layer_kernel(tiles_per_row, n_tiles,
                  x_ref, mixg_ref, w_in_ref, qg_ref, kg_ref, rtab_ref, sgug_ref, wsp_ref,
                  bsp_ref, ag_ref, gg_ref, w_out_ref, ffng_ref, w1_ref, w2_ref,
                  o_ref, k_sc, v_sc, bias_sc, y_sc, hf_sc, act_sc):
    g = pl.program_id(0)
    t = lax.rem(jnp.minimum(g, n_tiles - 1), tiles_per_row)
    row0 = pl.multiple_of(t * TQ, TQ)

    @pl.when(g == 0)
    def _():
        _build_bias(rtab_ref, bias_sc)
        y_sc[...] = jnp.zeros(y_sc.shape, y_sc.dtype)
        hf_sc[...] = jnp.zeros(hf_sc.shape, hf_sc.dtype)

    @pl.when(t == 0)
    def _():
        k_sc[0:LEFT, :] = jnp.zeros((LEFT, ATT_WIDTH), k_sc.dtype)
        v_sc[0:LEFT, :] = jnp.zeros((LEFT, ATT_WIDTH), v_sc.dtype)

    mix_out, hf_next, ffn_out = _step(
        x_ref[...], y_sc[...], hf_sc[...], row0, mixg_ref, w_in_ref, qg_ref, kg_ref, sgug_ref,
        wsp_ref, bsp_ref, ag_ref, gg_ref, w_out_ref, ffng_ref, w1_ref, w2_ref, k_sc, v_sc,
        bias_sc, act_sc)
    o_ref[...] = ffn_out
    y_sc[...] = mix_out
    hf_sc[...] = hf_next


def _layer_spec(layer, shape):
    zeros = (0,) * len(shape)
    return pl.BlockSpec((None,) + shape, lambda g: (layer,) + zeros,
                        pipeline_mode=pl.Buffered(1))


def _layer(layer, x, mixg, w_in, qg, kg, rtab, sgug, wsp, bsp, ag, gg, w_out, ffng, w1, w2):
    B, S, D = x.shape
    tiles_per_row = S // TQ
    n_tiles = B * tiles_per_row

    def in_tile(g):
        gi = jnp.minimum(g, n_tiles - 1)
        return (gi // tiles_per_row, gi % tiles_per_row, 0)

    def out_tile(g):
        go = jnp.maximum(g - 1, 0)
        return (go // tiles_per_row, go % tiles_per_row, 0)

    return pl.pallas_call(
        functools.partial(_layer_kernel, tiles_per_row, n_tiles),
        grid=(n_tiles + 1,),
        in_specs=[
            pl.BlockSpec((None, TQ, D), in_tile),
            _layer_spec(layer, (1, D)),
            _layer_spec(layer, (D, IN_WIDTH)),
            _layer_spec(layer, (1, ATT_WIDTH)),
            _layer_spec(layer, (1, ATT_WIDTH)),
            _layer_spec(layer, (ATT_HEADS, MXU_COLS)),
            _layer_spec(layer, (1, GMLP_WIDTH)),
            _layer_spec(layer, (GMLP_GROUPS, GMLP_BLOCK, GMLP_BLOCK)),
            _layer_spec(layer, (GMLP_BLOCK, GMLP_WIDTH)),
            _layer_spec(layer, (1, ATT_WIDTH)),
            _layer_spec(layer, (1, GMLP_WIDTH)),
            _layer_spec(layer, (D, D)),
            _layer_spec(layer, (1, D)),
            _layer_spec(layer, (D, 2 * D_FF)),
            _layer_spec(layer, (D_FF, D)),
        ],
        out_specs=pl.BlockSpec((None, TQ, D), out_tile),
        out_shape=jax.ShapeDtypeStruct(x.shape, x.dtype),
        scratch_shapes=[
            pltpu.VMEM((LEFT + S, ATT_WIDTH), jnp.bfloat16),
            pltpu.VMEM((LEFT + S, ATT_WIDTH), jnp.bfloat16),
            pltpu.VMEM((ATT_HEADS, TQ, TK), jnp.float32),
            pltpu.VMEM((TQ, D), jnp.float32),
            pltpu.VMEM((TQ, D), jnp.bfloat16),
            pltpu.VMEM((TQ, D_FF), jnp.bfloat16),
        ],
        compiler_params=pltpu.CompilerParams(
            dimension_semantics=("arbitrary",),
            vmem_limit_bytes=VMEM_LIMIT_BYTES),
        name="layer",
    )(x, mixg, w_in, qg, kg, rtab, sgug, wsp, bsp, ag, gg, w_out, ffng, w1, w2)


def _reversed_rel_table(rel_bias):
    near = rel_bias[:, :, ::-1][:, :, :NEAR]
    far = jnp.broadcast_to(rel_bias[:, :, 2 * MAX_REL:], near.shape[:2] + (MXU_COLS - NEAR,))
    return jnp.concatenate([near, far], axis=2).astype(jnp.float32)


def kernel(x, mix_norm_g, w_in, q_norm_g, k_norm_g, rel_bias, sgu_norm_g, w_spatial, b_spatial,
           att_out_norm_g, gmlp_out_norm_g, w_out, ffn_norm_g, w_ffn_in, w_ffn_out):
    depth = w_in.shape[0]
    bf16 = jnp.bfloat16
    row = lambda p: p[:, None, :]
    params = (
        row(mix_norm_g), w_in.astype(bf16),
        row(jnp.tile(q_norm_g, (1, ATT_HEADS))), row(jnp.tile(k_norm_g, (1, ATT_HEADS))),
        _reversed_rel_table(rel_bias), row(sgu_norm_g), w_spatial,
        jnp.repeat(jnp.swapaxes(b_spatial, 1, 2), GMLP_GROUP_DIM, axis=2),
        row(att_out_norm_g), row(gmlp_out_norm_g), w_out.astype(bf16),
        row(ffn_norm_g), w_ffn_in.astype(bf16), w_ffn_out.astype(bf16))
    for l in range(depth):
        x = _layer(l, x, *params)
    return x
```

```python
import functools

import jax
import jax.numpy as jnp
from jax import lax
from jax.experimental import pallas as pl
from jax.experimental.pallas import tpu as pltpu

D_MODEL = 1024
CHUNK = 64
ATT_HEADS = 8
HEAD_DIM = 64
ATT_WIDTH = ATT_HEADS * HEAD_DIM
LEFT_CHUNKS = 8
LEFT = LEFT_CHUNKS * CHUNK
MAX_REL = 2 * CHUNK
NEAR = MAX_REL + CHUNK
GMLP_WIDTH = 512
GMLP_GROUPS = 8
GMLP_GROUP_DIM = GMLP_WIDTH // GMLP_GROUPS
GMLP_BLOCK = 128
D_FF = 2816
KUG_WIDTH = ATT_WIDTH + 2 * GMLP_WIDTH
EPS = 1e-6
NEG_INF = -1e30

LANES = 128
BF16_ROWS = 16
MXU_COLS = 256
TQ = 256
TK = LEFT + TQ
PAD_TILES = LEFT // TQ
WIN_TILES = TK // TQ
FF_CHUNK = 256
VMEM_LIMIT_BYTES = 60 * 1024 * 1024


def _rms(x, g):
    ms = jnp.mean(x * x, axis=-1, keepdims=True)
    return (x * lax.rsqrt(ms + EPS)) * g


def _head_rms(z, gain):
    rows = z.shape[0]
    lo = lax.broadcasted_iota(jnp.int32, (rows, LANES), 1) < HEAD_DIM
    outs = []
    for c in range(ATT_WIDTH // LANES):
        zc = z[:, c * LANES:(c + 1) * LANES]
        sq = zc * zc
        ms_lo = jnp.sum(jnp.where(lo, sq, 0.0), axis=-1, keepdims=True) * (1.0 / HEAD_DIM)
        ms_hi = jnp.sum(jnp.where(lo, 0.0, sq), axis=-1, keepdims=True) * (1.0 / HEAD_DIM)
        r = jnp.where(lo, lax.rsqrt(ms_lo + EPS), lax.rsqrt(ms_hi + EPS))
        outs.append((zc * r) * gain[:, c * LANES:(c + 1) * LANES])
    return jnp.concatenate(outs, axis=-1)


def _build_bias(rtab_ref, bias_sc):
    f32 = jnp.float32
    row = lax.broadcasted_iota(jnp.int32, (CHUNK, MXU_COLS), 0)
    lo_half = lax.broadcasted_iota(jnp.int32, (CHUNK, LANES), 1) < CHUNK
    n_kc = TK // CHUNK
    for head in range(ATT_HEADS):
        r = rtab_ref[head:head + 1, :]
        far = rtab_ref[head:head + 1, MXU_COLS - 1:MXU_COLS]
        x0 = jnp.broadcast_to(r, (CHUNK, MXU_COLS))
        for b in range(CHUNK.bit_length() - 1):
            x0 = jnp.where(((row >> b) & 1) == 1, pltpu.roll(x0, 1 << b, axis=1), x0)
        x1 = pltpu.roll(x0, CHUNK, axis=1)
        far_piece = jnp.broadcast_to(far, (CHUNK, LANES))
        neg_piece = jnp.full((CHUNK, LANES), NEG_INF, f32)

        def half(qc, kc):
            rel = kc - qc
            if rel < 0 or rel > LEFT_CHUNKS:
                return neg_piece
            if rel < LEFT_CHUNKS - 2:
                return far_piece
            col = (rel - (LEFT_CHUNKS - 2)) * CHUNK + (qc % 2) * CHUNK
            src = x1 if qc % 2 else x0
            return src[:, (col // LANES) * LANES:(col // LANES + 1) * LANES]

        row_blocks = []
        for qc in range(TQ // CHUNK):
            pieces = [jnp.where(lo_half, half(qc, 2 * vcol), half(qc, 2 * vcol + 1))
                      for vcol in range(n_kc // 2)]
            row_blocks.append(jnp.concatenate(pieces, axis=1))
        bias_sc[head] = jnp.concatenate(row_blocks, axis=0).T


def _step(x, y_prev, t, row0, mixg_ref, w_qv_ref, w_kug_ref, qg_ref, kg_ref, sgug_ref, wsp_ref,
          bsp_ref, ag_ref, gg_ref, w_out_ref, ffng_ref, w1_ref, w2_ref,
          k_sc, v_sc, km_sc, bias_sc, act_sc):
    bf16 = jnp.bfloat16
    f32 = jnp.float32
    nt = (((1,), (1,)), ((), ()))
    h = _rms(x, mixg_ref[...]).astype(bf16)
    hf = _rms(y_prev, ffng_ref[...]).astype(bf16)

    def ffn_chunk(j):
        gate = jnp.dot(hf, w1_ref[:, j * FF_CHUNK:(j + 1) * FF_CHUNK], preferred_element_type=f32)
        up = jnp.dot(hf, w1_ref[:, D_FF + j * FF_CHUNK:D_FF + (j + 1) * FF_CHUNK],
                     preferred_element_type=f32)
        act_sc[:, j * FF_CHUNK:(j + 1) * FF_CHUNK] = (jax.nn.silu(gate) * up).astype(bf16)

    qv_t = lax.dot_general(w_qv_ref[...], h, nt, preferred_element_type=f32)
    k = jnp.dot(h, w_kug_ref[:, 0:ATT_WIDTH], preferred_element_type=f32)
    u = jnp.dot(h, w_kug_ref[:, ATT_WIDTH:ATT_WIDTH + GMLP_WIDTH], preferred_element_type=f32)
    vg = jnp.dot(h, w_kug_ref[:, ATT_WIDTH + GMLP_WIDTH:KUG_WIDTH], preferred_element_type=f32)

    q3 = qv_t[0:ATT_WIDTH].reshape(ATT_HEADS, HEAD_DIM, TQ)
    q_ms = jnp.mean(q3 * q3, axis=1, keepdims=True)
    q3 = (q3 * lax.rsqrt(q_ms + EPS)) * qg_ref[...].reshape(ATT_HEADS, HEAD_DIM, TQ)
    qn_t = (q3 * HEAD_DIM ** -0.5).reshape(ATT_WIDTH, TQ).astype(bf16)
    kn = _head_rms(k, kg_ref[...]).astype(bf16)
    k_sc[pl.ds(row0 + LEFT, TQ), :] = kn
    v_sc[t + PAD_TILES] = qv_t[ATT_WIDTH:2 * ATT_WIDTH].astype(bf16)
    kw = k_sc[pl.ds(row0, TK), :]
    km = km_sc[pl.ds(row0, TK), :]
    pad_rows = jnp.where(lax.broadcasted_iota(jnp.int32, (LANES, TQ), 0) == 0,
                         1.0, 0.0).astype(bf16)
    zero_rows = jnp.zeros((HEAD_DIM, TQ), bf16)
    ones_rows = jnp.ones((BF16_ROWS, TK), bf16)

    def scores(head):
        c = head // 2
        qh = qn_t[head * HEAD_DIM:(head + 1) * HEAD_DIM]
        pair = [qh, zero_rows] if head % 2 == 0 else [zero_rows, qh]
        rhs = jnp.concatenate(pair + [pad_rows], axis=0)
        lhs = jnp.concatenate([kw[:, c * LANES:(c + 1) * LANES], km], axis=1)
        s = jnp.dot(lhs, rhs, preferred_element_type=f32)
        return s + bias_sc[head]

    def attend(head, s):
        m = jnp.max(s, axis=0, keepdims=True)
        p = jnp.exp(s - m).astype(bf16)
        v_t = jnp.concatenate(
            [v_sc[t + w, head * HEAD_DIM:(head + 1) * HEAD_DIM, :] for w in range(WIN_TILES)],
            axis=1)
        pv = jnp.dot(jnp.concatenate([v_t, ones_rows], axis=0), p,
                     preferred_element_type=f32)
        return pv[0:HEAD_DIM] * (1.0 / pv[HEAD_DIM:HEAD_DIM + 1])

    outs = []
    s_next = scores(0)
    for head in range(ATT_HEADS):
        s_cur = s_next
        if head + 1 < ATT_HEADS:
            s_next = scores(head + 1)
        ffn_chunk(head)
        outs.append(attend(head, s_cur))
    a_t = jnp.concatenate(outs, axis=0)
    a_ms = jnp.mean(a_t * a_t, axis=0, keepdims=True)
    a_n = ((a_t * lax.rsqrt(a_ms + EPS)) * ag_ref[...]).T.astype(bf16)

    u_act = jax.nn.gelu(u)
    vgn = _rms(jax.nn.gelu(vg), sgug_ref[...]).astype(bf16)
    ti = lax.broadcasted_iota(jnp.int32, (GMLP_GROUPS, GMLP_BLOCK, GMLP_BLOCK), 1)
    si = lax.broadcasted_iota(jnp.int32, (GMLP_GROUPS, GMLP_BLOCK, GMLP_BLOCK), 2)
    causal = (ti // CHUNK) >= (si // CHUNK)
    wsp = jnp.where(causal, wsp_ref[...], 0.0).astype(bf16)
    wsp = wsp.reshape(GMLP_GROUPS * GMLP_BLOCK, GMLP_BLOCK)
    lane_g = lax.broadcasted_iota(jnp.int32, (GMLP_BLOCK, MXU_COLS), 1)
    groups_per_quad = MXU_COLS // GMLP_GROUP_DIM
    next_chunk = ATT_HEADS
    blocks = []
    for blk in range(TQ // GMLP_BLOCK):
        cols = []
        for c4 in range(GMLP_WIDTH // MXU_COLS):
            lhs = wsp[c4 * groups_per_quad * GMLP_BLOCK:(c4 + 1) * groups_per_quad * GMLP_BLOCK, :]
            rhs = vgn[blk * GMLP_BLOCK:(blk + 1) * GMLP_BLOCK, c4 * MXU_COLS:(c4 + 1) * MXU_COLS]
            res = jnp.dot(lhs, rhs, preferred_element_type=f32)
            acc = res[0:GMLP_BLOCK, :]
            for gg in range(1, groups_per_quad):
                acc = jnp.where(lane_g >= gg * GMLP_GROUP_DIM,
                                res[gg * GMLP_BLOCK:(gg + 1) * GMLP_BLOCK, :], acc)
            cols.append(acc)
            if next_chunk < D_FF // FF_CHUNK:
                ffn_chunk(next_chunk)
                next_chunk += 1
        blocks.append(jnp.concatenate(cols, axis=-1) + bsp_ref[...])
    while next_chunk < D_FF // FF_CHUNK:
        ffn_chunk(next_chunk)
        next_chunk += 1
    mixed = jnp.concatenate(blocks, axis=0)
    g_n = _rms(u_act * mixed, gg_ref[...]).astype(bf16)

    ffn_out = y_prev + jnp.dot(act_sc[...], w2_ref[...], preferred_element_type=f32)
    mix = jnp.concatenate([a_n, g_n], axis=-1)
    mix_out = x + jnp.dot(mix, w_out_ref[...], preferred_element_type=f32)
    return mix_out, ffn_out


def _layer_kernel(tiles_per_row, n_tiles,
                  x_ref, mixg_ref, w_qv_ref, w_kug_ref, qg_ref, kg_ref, rtab_ref, sgug_ref,
                  wsp_ref, bsp_ref, ag_ref, gg_ref, w_out_ref, ffng_ref, w1_ref, w2_ref,
                  o_ref, k_sc, v_sc, km_sc, bias_sc, y_sc, act_sc):
    g = pl.program_id(0)
    t = lax.rem(jnp.minimum(g, n_tiles - 1), tiles_per_row)
    row0 = pl.multiple_of(t * TQ, TQ)

    @pl.when(g == 0)
    def _():
        _build_bias(rtab_ref, bias_sc)
        y_sc[...] = jnp.zeros(y_sc.shape, y_sc.dtype)
        is_pad = ((lax.broadcasted_iota(jnp.int32, km_sc.shape, 0) < LEFT)
                  & (lax.broadcasted_iota(jnp.int32, km_sc.shape, 1) == 0))
        km_sc[...] = jnp.where(is_pad, NEG_INF, 0.0).astype(km_sc.dtype)

    @pl.when(t == 0)
    def _():
        k_sc[0:LEFT, :] = jnp.zeros((LEFT, ATT_WIDTH), k_sc.dtype)
        v_sc[0:PAD_TILES] = jnp.zeros((PAD_TILES, ATT_WIDTH, TQ), v_sc.dtype)

    mix_out, ffn_out = _step(
        x_ref[...], y_sc[...], t, row0, mixg_ref, w_qv_ref, w_kug_ref, qg_ref, kg_ref, sgug_ref,
        wsp_ref, bsp_ref, ag_ref, gg_ref, w_out_ref, ffng_ref, w1_ref, w2_ref,
        k_sc, v_sc, km_sc, bias_sc, act_sc)
    o_ref[...] = ffn_out
    y_sc[...] = mix_out


def _layer_spec(layer, shape):
    zeros = (0,) * len(shape)
    return pl.BlockSpec((None,) + shape, lambda g: (layer,) + zeros,
                        pipeline_mode=pl.Buffered(1))


def _layer(layer, x, mixg, w_qv, w_kug, qg, kg, rtab, sgug, wsp, bsp, ag, gg, w_out, ffng, w1, w2):
    B, S, D = x.shape
    tiles_per_row = S // TQ
    n_tiles = B * tiles_per_row

    def in_tile(g):
        gi = jnp.minimum(g, n_tiles - 1)
        return (gi // tiles_per_row, gi % tiles_per_row, 0)

    def out_tile(g):
        go = jnp.maximum(g - 1, 0)
        return (go // tiles_per_row, go % tiles_per_row, 0)

    return pl.pallas_call(
        functools.partial(_layer_kernel, tiles_per_row, n_tiles),
        grid=(n_tiles + 1,),
        in_specs=[
            pl.BlockSpec((None, TQ, D), in_tile),
            _layer_spec(layer, (1, D)),
            _layer_spec(layer, (2 * ATT_WIDTH, D)),
            _layer_spec(layer, (D, KUG_WIDTH)),
            _layer_spec(layer, (ATT_WIDTH, TQ)),
            _layer_spec(layer, (1, ATT_WIDTH)),
            _layer_spec(layer, (ATT_HEADS, MXU_COLS)),
            _layer_spec(layer, (1, GMLP_WIDTH)),
            _layer_spec(layer, (GMLP_GROUPS, GMLP_BLOCK, GMLP_BLOCK)),
            _layer_spec(layer, (GMLP_BLOCK, GMLP_WIDTH)),
            _layer_spec(layer, (ATT_WIDTH, TQ)),
            _layer_spec(layer, (1, GMLP_WIDTH)),
            _layer_spec(layer, (D, D)),
            _layer_spec(layer, (1, D)),
            _layer_spec(layer, (D, 2 * D_FF)),
            _layer_spec(layer, (D_FF, D)),
        ],
        out_specs=pl.BlockSpec((None, TQ, D), out_tile),
        out_shape=jax.ShapeDtypeStruct(x.shape, x.dtype),
        scratch_shapes=[
            pltpu.VMEM((LEFT + S, ATT_WIDTH), jnp.bfloat16),
            pltpu.VMEM((PAD_TILES + tiles_per_row, ATT_WIDTH, TQ), jnp.bfloat16),
            pltpu.VMEM((LEFT + S, LANES), jnp.bfloat16),
            pltpu.VMEM((ATT_HEADS, TK, TQ), jnp.float32),
            pltpu.VMEM((TQ, D), jnp.float32),
            pltpu.VMEM((TQ, D_FF), jnp.bfloat16),
        ],
        compiler_params=pltpu.CompilerParams(
            dimension_semantics=("arbitrary",),
            vmem_limit_bytes=VMEM_LIMIT_BYTES),
        name="layer",
    )(x, mixg, w_qv, w_kug, qg, kg, rtab, sgug, wsp, bsp, ag, gg, w_out, ffng, w1, w2)


def _reversed_rel_table(rel_bias):
    near = rel_bias[:, :, ::-1][:, :, :NEAR]
    far = jnp.broadcast_to(rel_bias[:, :, 2 * MAX_REL:], near.shape[:2] + (MXU_COLS - NEAR,))
    return jnp.concatenate([near, far], axis=2).astype(jnp.float32)


def _feature_column(g, reps):
    col = jnp.tile(g, (1, reps))[:, :, None]
    return jnp.broadcast_to(col, col.shape[:2] + (TQ,))


def kernel(x, mix_norm_g, w_in, q_norm_g, k_norm_g, rel_bias, sgu_norm_g, w_spatial, b_spatial,
           att_out_norm_g, gmlp_out_norm_g, w_out, ffn_norm_g, w_ffn_in, w_ffn_out):
    depth = w_in.shape[0]
    bf16 = jnp.bfloat16
    row = lambda p: p[:, None, :]
    w_in_b = w_in.astype(bf16)
    w_q, w_k, w_v, w_ug = (w_in_b[:, :, 0:ATT_WIDTH], w_in_b[:, :, ATT_WIDTH:2 * ATT_WIDTH],
                           w_in_b[:, :, 2 * ATT_WIDTH:3 * ATT_WIDTH], w_in_b[:, :, 3 * ATT_WIDTH:])
    w_qv_t = jnp.swapaxes(jnp.concatenate([w_q, w_v], axis=2), 1, 2)
    w_kug = jnp.concatenate([w_k, w_ug], axis=2)
    params = (
        row(mix_norm_g), w_qv_t, w_kug,
        _feature_column(q_norm_g, ATT_HEADS), row(jnp.tile(k_norm_g, (1, ATT_HEADS))),
        _reversed_rel_table(rel_bias), row(sgu_norm_g), w_spatial,
        jnp.repeat(jnp.swapaxes(b_spatial, 1, 2), GMLP_GROUP_DIM, axis=2),
        _feature_column(att_out_norm_g, 1), row(gmlp_out_norm_g), w_out.astype(bf16),
        row(ffn_norm_g), w_ffn_in.astype(bf16), w_ffn_out.astype(bf16))
    for l in range(depth):
        x = _layer(l, x, *params)
    return x
```

```python
import functools

import jax
import jax.numpy as jnp
from jax import lax
from jax.experimental import pallas as pl
from jax.experimental.pallas import tpu as pltpu

D_MODEL = 1024
CHUNK = 64
ATT_HEADS = 8
HEAD_DIM = 64
ATT_WIDTH = ATT_HEADS * HEAD_DIM
LEFT_CHUNKS = 8
LEFT = LEFT_CHUNKS * CHUNK
MAX_REL = 2 * CHUNK
NEAR = MAX_REL + CHUNK
GMLP_WIDTH = 512
GMLP_GROUPS = 8
GMLP_GROUP_DIM = GMLP_WIDTH // GMLP_GROUPS
GMLP_BLOCK = 128
D_FF = 2816
IN_WIDTH = 3 * ATT_WIDTH + 2 * GMLP_WIDTH
EPS = 1e-6
NEG_INF = -1e30

LANES = 128
MXU_COLS = 256
TQ = 256
TK = LEFT + TQ
FF_CHUNK = 256
STAGE_ROWS = {IN_WIDTH: 64, D_MODEL: 128, 2 * D_FF: 32}
VMEM_LIMIT_BYTES = 60 * 1024 * 1024


def _rms(x, g):
    ms = jnp.mean(x * x, axis=-1, keepdims=True)
    return (x * lax.rsqrt(ms + EPS)) * g


def _head_rms(z, gain, post_scale):
    rows = z.shape[0]
    lo = lax.broadcasted_iota(jnp.int32, (rows, LANES), 1) < HEAD_DIM
    outs = []
    for c in range(ATT_WIDTH // LANES):
        zc = z[:, c * LANES:(c + 1) * LANES]
        sq = zc * zc
        ms_lo = jnp.sum(jnp.where(lo, sq, 0.0), axis=-1, keepdims=True) * (1.0 / HEAD_DIM)
        ms_hi = jnp.sum(jnp.where(lo, 0.0, sq), axis=-1, keepdims=True) * (1.0 / HEAD_DIM)
        r = jnp.where(lo, lax.rsqrt(ms_lo + EPS), lax.rsqrt(ms_hi + EPS))
        y = (zc * r) * gain[:, c * LANES:(c + 1) * LANES]
        if post_scale != 1.0:
            y = y * post_scale
        outs.append(y)
    return jnp.concatenate(outs, axis=-1)


def _build_bias(rtab_ref, bias_sc):
    f32 = jnp.float32
    row = lax.broadcasted_iota(jnp.int32, (CHUNK, MXU_COLS), 0)
    lo_half = lax.broadcasted_iota(jnp.int32, (CHUNK, LANES), 1) < CHUNK
    n_kc = TK // CHUNK
    for head in range(ATT_HEADS):
        r = rtab_ref[head:head + 1, :]
        far = rtab_ref[head:head + 1, MXU_COLS - 1:MXU_COLS]
        x0 = jnp.broadcast_to(r, (CHUNK, MXU_COLS))
        for b in range(CHUNK.bit_length() - 1):
            x0 = jnp.where(((row >> b) & 1) == 1, pltpu.roll(x0, 1 << b, axis=1), x0)
        x1 = pltpu.roll(x0, CHUNK, axis=1)
        far_piece = jnp.broadcast_to(far, (CHUNK, LANES))
        neg_piece = jnp.full((CHUNK, LANES), NEG_INF, f32)

        def half(qc, kc):
            rel = kc - qc
            if rel < 0 or rel > LEFT_CHUNKS:
                return neg_piece
            if rel < LEFT_CHUNKS - 2:
                return far_piece
            col = (rel - (LEFT_CHUNKS - 2)) * CHUNK + (qc % 2) * CHUNK
            src = x1 if qc % 2 else x0
            return src[:, (col // LANES) * LANES:(col // LANES + 1) * LANES]

        for qc in range(TQ // CHUNK):
            for vcol in range(n_kc // 2):
                piece = jnp.where(lo_half, half(qc, 2 * vcol), half(qc, 2 * vcol + 1))
                bias_sc[head, qc * CHUNK:(qc + 1) * CHUNK,
                        vcol * LANES:(vcol + 1) * LANES] = piece


def _step(x, y_prev, row0, mixg_ref, w_in_ref, qg_ref, kg_ref, sgug_ref, wsp_ref, bsp_ref,
          ag_ref, gg_ref, w_out_ref, ffng_ref, w1_ref, w2_ref, k_sc, v_sc, bias_sc, act_sc):
    bf16 = jnp.bfloat16
    f32 = jnp.float32
    h = _rms(x, mixg_ref[...]).astype(bf16)
    hf = _rms(y_prev, ffng_ref[...]).astype(bf16)

    def proj(lo, width):
        return jnp.dot(h, w_in_ref[:, lo:lo + width], preferred_element_type=f32)

    def ffn_chunk(j):
        gate = jnp.dot(hf, w1_ref[:, j * FF_CHUNK:(j + 1) * FF_CHUNK], preferred_element_type=f32)
        up = jnp.dot(hf, w1_ref[:, D_FF + j * FF_CHUNK:D_FF + (j + 1) * FF_CHUNK],
                     preferred_element_type=f32)
        act_sc[:, j * FF_CHUNK:(j + 1) * FF_CHUNK] = (jax.nn.silu(gate) * up).astype(bf16)

    q = proj(0, ATT_WIDTH)
    k = proj(ATT_WIDTH, ATT_WIDTH)
    v = proj(2 * ATT_WIDTH, ATT_WIDTH)
    u = proj(3 * ATT_WIDTH, GMLP_WIDTH)
    vg = proj(3 * ATT_WIDTH + GMLP_WIDTH, GMLP_WIDTH)

    qn = _head_rms(q, qg_ref[...], HEAD_DIM ** -0.5).astype(bf16)
    kn = _head_rms(k, kg_ref[...], 1.0).astype(bf16)
    k_sc[pl.ds(row0 + LEFT, TQ), :] = kn
    v_sc[pl.ds(row0 + LEFT, TQ), :] = v.astype(bf16)
    kw = k_sc[pl.ds(row0, TK), :]
    vw = v_sc[pl.ds(row0, TK), :]

    pad_mask = jnp.where(lax.broadcasted_iota(jnp.int32, (1, TK), 1) >= LEFT - row0,
                         0.0, NEG_INF)
    lane_q = lax.broadcasted_iota(jnp.int32, (TQ, LANES), 1)
    lane_o = lax.broadcasted_iota(jnp.int32, (TQ, MXU_COLS), 1)
    heads_per_quad = MXU_COLS // HEAD_DIM

    def scores(head):
        c = head // 2
        in_head = (lane_q < HEAD_DIM) if head % 2 == 0 else (lane_q >= HEAD_DIM)
        qm = jnp.where(in_head, qn[:, c * LANES:(c + 1) * LANES], jnp.zeros((), bf16))
        s = lax.dot_general(qm, kw[:, c * LANES:(c + 1) * LANES],
                            (((1,), (1,)), ((), ())), preferred_element_type=f32)
        return (s + bias_sc[head]) + pad_mask

    def attend(head, s):
        c4 = head // heads_per_quad
        m = jnp.max(s, axis=-1, keepdims=True)
        p = jnp.exp(s - m)
        l = jnp.sum(p, axis=-1, keepdims=True)
        pv = jnp.dot(p.astype(bf16), vw[:, c4 * MXU_COLS:(c4 + 1) * MXU_COLS],
                     preferred_element_type=f32)
        return pv * (1.0 / l)

    quads = []
    acc = None
    s_next = scores(0)
    for head in range(ATT_HEADS):
        s_cur = s_next
        if head + 1 < ATT_HEADS:
            s_next = scores(head + 1)
        ffn_chunk(head)
        pv = attend(head, s_cur)
        hh = head % heads_per_quad
        acc = pv if hh == 0 else jnp.where(lane_o >= hh * HEAD_DIM, pv, acc)
        if hh == heads_per_quad - 1:
            quads.append(acc)
    a = jnp.concatenate(quads, axis=-1)
    a_n = _rms(a, ag_ref[...]).astype(bf16)

    u_act = jax.nn.gelu(u)
    vgn = _rms(jax.nn.gelu(vg), sgug_ref[...]).astype(bf16)
    ti = lax.broadcasted_iota(jnp.int32, (GMLP_GROUPS, GMLP_BLOCK, GMLP_BLOCK), 1)
    si = lax.broadcasted_iota(jnp.int32, (GMLP_GROUPS, GMLP_BLOCK, GMLP_BLOCK), 2)
    causal = (ti // CHUNK) >= (si // CHUNK)
    wsp = jnp.where(causal, wsp_ref[...], 0.0).astype(bf16)
    wsp = wsp.reshape(GMLP_GROUPS * GMLP_BLOCK, GMLP_BLOCK)
    lane_g = lax.broadcasted_iota(jnp.int32, (GMLP_BLOCK, MXU_COLS), 1)
    groups_per_quad = MXU_COLS // GMLP_GROUP_DIM
    next_chunk = ATT_HEADS
    blocks = []
    for blk in range(TQ // GMLP_BLOCK):
        cols = []
        for c4 in range(GMLP_WIDTH // MXU_COLS):
            lhs = wsp[c4 * groups_per_quad * GMLP_BLOCK:(c4 + 1) * groups_per_quad * GMLP_BLOCK, :]
            rhs = vgn[blk * GMLP_BLOCK:(blk + 1) * GMLP_BLOCK, c4 * MXU_COLS:(c4 + 1) * MXU_COLS]
            res = jnp.dot(lhs, rhs, preferred_element_type=f32)
            acc = res[0:GMLP_BLOCK, :]
            for gg in range(1, groups_per_quad):
                acc = jnp.where(lane_g >= gg * GMLP_GROUP_DIM,
                                res[gg * GMLP_BLOCK:(gg + 1) * GMLP_BLOCK, :], acc)
            cols.append(acc)
            if next_chunk < D_FF // FF_CHUNK:
                ffn_chunk(next_chunk)
                next_chunk += 1
        blocks.append(jnp.concatenate(cols, axis=-1) + bsp_ref[...])
    while next_chunk < D_FF // FF_CHUNK:
        ffn_chunk(next_chunk)
        next_chunk += 1
    mixed = jnp.concatenate(blocks, axis=0)
    g_n = _rms(u_act * mixed, gg_ref[...]).astype(bf16)

    ffn_out = y_prev + jnp.dot(act_sc[...], w2_ref[...], preferred_element_type=f32)
    mix = jnp.concatenate([a_n, g_n], axis=-1)
    mix_out = x + jnp.dot(mix, w_out_ref[...], preferred_element_type=f32)
    return mix_out, ffn_out


def _stage_rows(cols):
    return STAGE_ROWS[cols]


def _load_weight(src, dst, stage, sem):
    rows, cols = src.shape
    chunk = _stage_rows(cols)
    assert rows % chunk == 0 and stage.shape == (2, chunk, cols)
    n = rows // chunk

    def copy(i, slot):
        return pltpu.make_async_copy(src.at[pl.ds(i * chunk, chunk)], stage.at[slot], sem.at[slot])

    copy(0, 0).start()

    def body(i, carry):
        slot = lax.rem(i, 2)

        @pl.when(i + 1 < n)
        def _():
            copy(i + 1, 1 - slot).start()

        copy(i, slot).wait()
        dst[pl.ds(pl.multiple_of(i * chunk, chunk), chunk), :] = stage[slot].astype(dst.dtype)
        return carry

    lax.fori_loop(0, n, body, 0)


def _layer_kernel(layer, tiles_per_row, n_tiles,
                  x_ref, mixg_ref, w_in_hbm, qg_ref, kg_ref, rtab_ref, sgug_ref, wsp_ref,
                  bsp_ref, ag_ref, gg_ref, w_out_hbm, ffng_ref, w1_hbm, w2_hbm,
                  o_ref, k_sc, v_sc, bias_sc, y_sc, act_sc,
                  w_in_sc, w_out_sc, w1_sc, w2_sc, st_in, st_d, st_ff, sem):
    g = pl.program_id(0)
    t = lax.rem(jnp.minimum(g, n_tiles - 1), tiles_per_row)
    row0 = pl.multiple_of(t * TQ, TQ)

    @pl.when(g == 0)
    def _():
        _load_weight(w_in_hbm.at[layer], w_in_sc, st_in, sem)
        _load_weight(w_out_hbm.at[layer], w_out_sc, st_d, sem)
        _load_weight(w1_hbm.at[layer], w1_sc, st_ff, sem)
        _load_weight(w2_hbm.at[layer], w2_sc, st_d, sem)
        _build_bias(rtab_ref, bias_sc)
        y_sc[...] = jnp.zeros(y_sc.shape, y_sc.dtype)

    @pl.when(t == 0)
    def _():
        k_sc[0:LEFT, :] = jnp.zeros((LEFT, ATT_WIDTH), k_sc.dtype)
        v_sc[0:LEFT, :] = jnp.zeros((LEFT, ATT_WIDTH), v_sc.dtype)

    mix_out, ffn_out = _step(
        x_ref[...], y_sc[...], row0, mixg_ref, w_in_sc, qg_ref, kg_ref, sgug_ref, wsp_ref,
        bsp_ref, ag_ref, gg_ref, w_out_sc, ffng_ref, w1_sc, w2_sc, k_sc, v_sc, bias_sc, act_sc)
    o_ref[...] = ffn_out
    y_sc[...] = mix_out


def _layer_spec(layer, shape):
    zeros = (0,) * len(shape)
    return pl.BlockSpec((None,) + shape, lambda g: (layer,) + zeros,
                        pipeline_mode=pl.Buffered(1))


def _layer(layer, x, mixg, w_in, qg, kg, rtab, sgug, wsp, bsp, ag, gg, w_out, ffng, w1, w2):
    B, S, D = x.shape
    tiles_per_row = S // TQ
    n_tiles = B * tiles_per_row
    bf16 = jnp.bfloat16
    f32 = jnp.float32

    def in_tile(g):
        gi = jnp.minimum(g, n_tiles - 1)
        return (gi // tiles_per_row, gi % tiles_per_row, 0)

    def out_tile(g):
        go = jnp.maximum(g - 1, 0)
        return (go // tiles_per_row, go % tiles_per_row, 0)

    hbm = pl.BlockSpec(memory_space=pl.ANY)
    return pl.pallas_call(
        functools.partial(_layer_kernel, layer, tiles_per_row, n_tiles),
        grid=(n_tiles + 1,),
        in_specs=[
            pl.BlockSpec((None, TQ, D), in_tile),
            _layer_spec(layer, (1, D)),
            hbm,
            _layer_spec(layer, (1, ATT_WIDTH)),
            _layer_spec(layer, (1, ATT_WIDTH)),
            _layer_spec(layer, (ATT_HEADS, MXU_COLS)),
            _layer_spec(layer, (1, GMLP_WIDTH)),
            _layer_spec(layer, (GMLP_GROUPS, GMLP_BLOCK, GMLP_BLOCK)),
            _layer_spec(layer, (GMLP_BLOCK, GMLP_WIDTH)),
            _layer_spec(layer, (1, ATT_WIDTH)),
            _layer_spec(layer, (1, GMLP_WIDTH)),
            hbm,
            _layer_spec(layer, (1, D)),
            hbm,
            hbm,
        ],
        out_specs=pl.BlockSpec((None, TQ, D), out_tile),
        out_shape=jax.ShapeDtypeStruct(x.shape, x.dtype),
        scratch_shapes=[
            pltpu.VMEM((LEFT + S, ATT_WIDTH), bf16),
            pltpu.VMEM((LEFT + S, ATT_WIDTH), bf16),
            pltpu.VMEM((ATT_HEADS, TQ, TK), f32),
            pltpu.VMEM((TQ, D), f32),
            pltpu.VMEM((TQ, D_FF), bf16),
            pltpu.VMEM((D, IN_WIDTH), bf16),
            pltpu.VMEM((D, D), bf16),
            pltpu.VMEM((D, 2 * D_FF), bf16),
            pltpu.VMEM((D_FF, D), bf16),
            pltpu.VMEM((2, _stage_rows(IN_WIDTH), IN_WIDTH), f32),
            pltpu.VMEM((2, _stage_rows(D), D), f32),
            pltpu.VMEM((2, _stage_rows(2 * D_FF), 2 * D_FF), f32),
            pltpu.SemaphoreType.DMA((2,)),
        ],
        compiler_params=pltpu.CompilerParams(
            dimension_semantics=("arbitrary",),
            vmem_limit_bytes=VMEM_LIMIT_BYTES),
        name="layer",
    )(x, mixg, w_in, qg, kg, rtab, sgug, wsp, bsp, ag, gg, w_out, ffng, w1, w2)


def _reversed_rel_table(rel_bias):
    near = rel_bias[:, :, ::-1][:, :, :NEAR]
    far = jnp.broadcast_to(rel_bias[:, :, 2 * MAX_REL:], near.shape[:2] + (MXU_COLS - NEAR,))
    return jnp.concatenate([near, far], axis=2).astype(jnp.float32)


def kernel(x, mix_norm_g, w_in, q_norm_g, k_norm_g, rel_bias, sgu_norm_g, w_spatial, b_spatial,
           att_out_norm_g, gmlp_out_norm_g, w_out, ffn_norm_g, w_ffn_in, w_ffn_out):
    depth = w_in.shape[0]
    row = lambda p: p[:, None, :]
    params = (
        row(mix_norm_g), w_in,
        row(jnp.tile(q_norm_g, (1, ATT_HEADS))), row(jnp.tile(k_norm_g, (1, ATT_HEADS))),
        _reversed_rel_table(rel_bias), row(sgu_norm_g), w_spatial,
        jnp.repeat(jnp.swapaxes(b_spatial, 1, 2), GMLP_GROUP_DIM, axis=2),
        row(att_out_norm_g), row(gmlp_out_norm_g), w_out,
        row(ffn_norm_g), w_ffn_in, w_ffn_out)
    for l in range(depth):
        x = _layer(l, x, *params)
    return x
```

```python
import functools

import jax
import jax.numpy as jnp
from jax import lax
from jax.experimental import pallas as pl
from jax.experimental.pallas import tpu as pltpu

D_MODEL = 1024
CHUNK = 64
ATT_HEADS = 8
HEAD_DIM = 64
ATT_WIDTH = ATT_HEADS * HEAD_DIM
LEFT_CHUNKS = 8
LEFT = LEFT_CHUNKS * CHUNK
MAX_REL = 2 * CHUNK
NEAR = MAX_REL + CHUNK
GMLP_WIDTH = 512
GMLP_GROUPS = 8
GMLP_GROUP_DIM = GMLP_WIDTH // GMLP_GROUPS
GMLP_BLOCK = 128
D_FF = 2816
IN_WIDTH = 3 * ATT_WIDTH + 2 * GMLP_WIDTH
EPS = 1e-6
NEG_INF = -1e30

LANES = 128
MXU_COLS = 256
TQ = 256
TK = LEFT + TQ
FF_CHUNK = 256
STAGE_ROWS = {IN_WIDTH: 64, D_MODEL: 128, 2 * D_FF: 32}
STAGE_SLOTS = 4
VMEM_LIMIT_BYTES = 60 * 1024 * 1024


def _rms(x, g):
    ms = jnp.mean(x * x, axis=-1, keepdims=True)
    return (x * lax.rsqrt(ms + EPS)) * g


def _head_rms(z, gain, post_scale):
    rows = z.shape[0]
    lo = lax.broadcasted_iota(jnp.int32, (rows, LANES), 1) < HEAD_DIM
    outs = []
    for c in range(ATT_WIDTH // LANES):
        zc = z[:, c * LANES:(c + 1) * LANES]
        sq = zc * zc
        ms_lo = jnp.sum(jnp.where(lo, sq, 0.0), axis=-1, keepdims=True) * (1.0 / HEAD_DIM)
        ms_hi = jnp.sum(jnp.where(lo, 0.0, sq), axis=-1, keepdims=True) * (1.0 / HEAD_DIM)
        r = jnp.where(lo, lax.rsqrt(ms_lo + EPS), lax.rsqrt(ms_hi + EPS))
        y = (zc * r) * gain[:, c * LANES:(c + 1) * LANES]
        if post_scale != 1.0:
            y = y * post_scale
        outs.append(y)
    return jnp.concatenate(outs, axis=-1)


def _build_bias(rtab_ref, bias_sc):
    f32 = jnp.float32
    row = lax.broadcasted_iota(jnp.int32, (CHUNK, MXU_COLS), 0)
    lo_half = lax.broadcasted_iota(jnp.int32, (CHUNK, LANES), 1) < CHUNK
    n_kc = TK // CHUNK
    for head in range(ATT_HEADS):
        r = rtab_ref[head:head + 1, :]
        far = rtab_ref[head:head + 1, MXU_COLS - 1:MXU_COLS]
        x0 = jnp.broadcast_to(r, (CHUNK, MXU_COLS))
        for b in range(CHUNK.bit_length() - 1):
            x0 = jnp.where(((row >> b) & 1) == 1, pltpu.roll(x0, 1 << b, axis=1), x0)
        x1 = pltpu.roll(x0, CHUNK, axis=1)
        far_piece = jnp.broadcast_to(far, (CHUNK, LANES))
        neg_piece = jnp.full((CHUNK, LANES), NEG_INF, f32)

        def half(qc, kc):
            rel = kc - qc
            if rel < 0 or rel > LEFT_CHUNKS:
                return neg_piece
            if rel < LEFT_CHUNKS - 2:
                return far_piece
            col = (rel - (LEFT_CHUNKS - 2)) * CHUNK + (qc % 2) * CHUNK
            src = x1 if qc % 2 else x0
            return src[:, (col // LANES) * LANES:(col // LANES + 1) * LANES]

        for qc in range(TQ // CHUNK):
            for vcol in range(n_kc // 2):
                piece = jnp.where(lo_half, half(qc, 2 * vcol), half(qc, 2 * vcol + 1))
                bias_sc[head, qc * CHUNK:(qc + 1) * CHUNK,
                        vcol * LANES:(vcol + 1) * LANES] = piece


def _step(x, y_prev, row0, mixg_ref, w_in_ref, qg_ref, kg_ref, sgug_ref, wsp_ref, bsp_ref,
          ag_ref, gg_ref, w_out_ref, ffng_ref, w1_ref, w2_ref, k_sc, v_sc, bias_sc, act_sc,
          do_mix=True, do_ffn=True):
    bf16 = jnp.bfloat16
    f32 = jnp.float32
    n_chunks = D_FF // FF_CHUNK
    if do_ffn:
        hf = _rms(y_prev, ffng_ref[...]).astype(bf16)

    def ffn_chunk(j):
        if not do_ffn:
            return
        gate = jnp.dot(hf, w1_ref[:, j * FF_CHUNK:(j + 1) * FF_CHUNK], preferred_element_type=f32)
        up = jnp.dot(hf, w1_ref[:, D_FF + j * FF_CHUNK:D_FF + (j + 1) * FF_CHUNK],
                     preferred_element_type=f32)
        act_sc[:, j * FF_CHUNK:(j + 1) * FF_CHUNK] = (jax.nn.silu(gate) * up).astype(bf16)

    def ffn_finish():
        return y_prev + jnp.dot(act_sc[...], w2_ref[...], preferred_element_type=f32)

    if not do_mix:
        for j in range(n_chunks):
            ffn_chunk(j)
        return None, ffn_finish()

    h = _rms(x, mixg_ref[...]).astype(bf16)

    def proj(lo, width):
        return jnp.dot(h, w_in_ref[:, lo:lo + width], preferred_element_type=f32)

    q = proj(0, ATT_WIDTH)
    k = proj(ATT_WIDTH, ATT_WIDTH)
    v = proj(2 * ATT_WIDTH, ATT_WIDTH)
    u = proj(3 * ATT_WIDTH, GMLP_WIDTH)
    vg = proj(3 * ATT_WIDTH + GMLP_WIDTH, GMLP_WIDTH)

    qn = _head_rms(q, qg_ref[...], HEAD_DIM ** -0.5).astype(bf16)
    kn = _head_rms(k, kg_ref[...], 1.0).astype(bf16)
    k_sc[pl.ds(row0 + LEFT, TQ), :] = kn
    v_sc[pl.ds(row0 + LEFT, TQ), :] = v.astype(bf16)
    kw = k_sc[pl.ds(row0, TK), :]
    vw = v_sc[pl.ds(row0, TK), :]

    pad_mask = jnp.where(lax.broadcasted_iota(jnp.int32, (1, TK), 1) >= LEFT - row0,
                         0.0, NEG_INF)
    lane_q = lax.broadcasted_iota(jnp.int32, (TQ, LANES), 1)
    lane_o = lax.broadcasted_iota(jnp.int32, (TQ, MXU_COLS), 1)
    heads_per_quad = MXU_COLS // HEAD_DIM

    def scores(head):
        c = head // 2
        in_head = (lane_q < HEAD_DIM) if head % 2 == 0 else (lane_q >= HEAD_DIM)
        qm = jnp.where(in_head, qn[:, c * LANES:(c + 1) * LANES], jnp.zeros((), bf16))
        s = lax.dot_general(qm, kw[:, c * LANES:(c + 1) * LANES],
                            (((1,), (1,)), ((), ())), preferred_element_type=f32)
        return (s + bias_sc[head]) + pad_mask

    def attend(head, s):
        c4 = head // heads_per_quad
        m = jnp.max(s, axis=-1, keepdims=True)
        p = jnp.exp(s - m)
        l = jnp.sum(p, axis=-1, keepdims=True)
        pv = jnp.dot(p.astype(bf16), vw[:, c4 * MXU_COLS:(c4 + 1) * MXU_COLS],
                     preferred_element_type=f32)
        return pv * (1.0 / l)

    quads = []
    acc = None
    s_next = scores(0)
    for head in range(ATT_HEADS):
        s_cur = s_next
        if head + 1 < ATT_HEADS:
            s_next = scores(head + 1)
        ffn_chunk(head)
        pv = attend(head, s_cur)
        hh = head % heads_per_quad
        acc = pv if hh == 0 else jnp.where(lane_o >= hh * HEAD_DIM, pv, acc)
        if hh == heads_per_quad - 1:
            quads.append(acc)
    a = jnp.concatenate(quads, axis=-1)
    a_n = _rms(a, ag_ref[...]).astype(bf16)

    u_act = jax.nn.gelu(u)
    vgn = _rms(jax.nn.gelu(vg), sgug_ref[...]).astype(bf16)
    ti = lax.broadcasted_iota(jnp.int32, (GMLP_GROUPS, GMLP_BLOCK, GMLP_BLOCK), 1)
    si = lax.broadcasted_iota(jnp.int32, (GMLP_GROUPS, GMLP_BLOCK, GMLP_BLOCK), 2)
    causal = (ti // CHUNK) >= (si // CHUNK)
    wsp = jnp.where(causal, wsp_ref[...], 0.0).astype(bf16)
    wsp = wsp.reshape(GMLP_GROUPS * GMLP_BLOCK, GMLP_BLOCK)
    lane_g = lax.broadcasted_iota(jnp.int32, (GMLP_BLOCK, MXU_COLS), 1)
    groups_per_quad = MXU_COLS // GMLP_GROUP_DIM
    next_chunk = ATT_HEADS
    blocks = []
    for blk in range(TQ // GMLP_BLOCK):
        cols = []
        for c4 in range(GMLP_WIDTH // MXU_COLS):
            lhs = wsp[c4 * groups_per_quad * GMLP_BLOCK:(c4 + 1) * groups_per_quad * GMLP_BLOCK, :]
            rhs = vgn[blk * GMLP_BLOCK:(blk + 1) * GMLP_BLOCK, c4 * MXU_COLS:(c4 + 1) * MXU_COLS]
            res = jnp.dot(lhs, rhs, preferred_element_type=f32)
            acc = res[0:GMLP_BLOCK, :]
            for gg in range(1, groups_per_quad):
                acc = jnp.where(lane_g >= gg * GMLP_GROUP_DIM,
                                res[gg * GMLP_BLOCK:(gg + 1) * GMLP_BLOCK, :], acc)
            cols.append(acc)
            if next_chunk < D_FF // FF_CHUNK:
                ffn_chunk(next_chunk)
                next_chunk += 1
        blocks.append(jnp.concatenate(cols, axis=-1) + bsp_ref[...])
    while next_chunk < D_FF // FF_CHUNK:
        ffn_chunk(next_chunk)
        next_chunk += 1
    mixed = jnp.concatenate(blocks, axis=0)
    g_n = _rms(u_act * mixed, gg_ref[...]).astype(bf16)

    ffn_out = ffn_finish() if do_ffn else None
    mix = jnp.concatenate([a_n, g_n], axis=-1)
    mix_out = x + jnp.dot(mix, w_out_ref[...], preferred_element_type=f32)
    return mix_out, ffn_out


def _stage_rows(cols):
    return STAGE_ROWS[cols]


def _load_weight(src, dst, stage, sem):
    rows, cols = src.shape
    chunk = _stage_rows(cols)
    assert rows % chunk == 0 and stage.shape == (STAGE_SLOTS, chunk, cols)
    n = rows // chunk
    ahead = STAGE_SLOTS - 1
    assert n >= ahead

    def copy(i, slot):
        return pltpu.make_async_copy(src.at[pl.ds(i * chunk, chunk)], stage.at[slot], sem.at[slot])

    for i in range(ahead):
        copy(i, i).start()

    def body(i, carry):
        slot = lax.rem(i, STAGE_SLOTS)

        @pl.when(i + ahead < n)
        def _():
            copy(i + ahead, lax.rem(i + ahead, STAGE_SLOTS)).start()

        copy(i, slot).wait()
        dst[pl.ds(pl.multiple_of(i * chunk, chunk), chunk), :] = stage[slot].astype(dst.dtype)
        return carry

    lax.fori_loop(0, n, body, 0)


def _layer_kernel(layer, tiles_per_row, n_tiles,
                  x_ref, mixg_ref, w_in_hbm, qg_ref, kg_ref, rtab_ref, sgug_ref, wsp_ref,
                  bsp_ref, ag_ref, gg_ref, w_out_hbm, ffng_ref, w1_hbm, w2_hbm,
                  o_ref, k_sc, v_sc, bias_sc, y_sc, act_sc,
                  w_in_sc, w_out_sc, w1_sc, w2_sc, st_in, st_d, st_ff, sem):
    g = pl.program_id(0)
    t = lax.rem(g, tiles_per_row)
    row0 = pl.multiple_of(t * TQ, TQ)

    @pl.when(g == 0)
    def _():
        _load_weight(w_in_hbm.at[layer], w_in_sc, st_in, sem)
        _load_weight(w_out_hbm.at[layer], w_out_sc, st_d, sem)
        _load_weight(w1_hbm.at[layer], w1_sc, st_ff, sem)
        _load_weight(w2_hbm.at[layer], w2_sc, st_d, sem)
        _build_bias(rtab_ref, bias_sc)

    @pl.when((t == 0) & (g < n_tiles))
    def _():
        k_sc[0:LEFT, :] = jnp.zeros((LEFT, ATT_WIDTH), k_sc.dtype)
        v_sc[0:LEFT, :] = jnp.zeros((LEFT, ATT_WIDTH), v_sc.dtype)

    def step(do_mix, do_ffn):
        mix_out, ffn_out = _step(
            x_ref[...] if do_mix else None, y_sc[...] if do_ffn else None, row0, mixg_ref, w_in_sc, qg_ref, kg_ref, sgug_ref, wsp_ref,
            bsp_ref, ag_ref, gg_ref, w_out_sc, ffng_ref, w1_sc, w2_sc, k_sc, v_sc, bias_sc,
            act_sc, do_mix=do_mix, do_ffn=do_ffn)
        if do_ffn:
            o_ref[...] = ffn_out
        if do_mix:
            y_sc[...] = mix_out

    pl.when(g == 0)(functools.partial(step, True, False))
    pl.when((g > 0) & (g < n_tiles))(functools.partial(step, True, True))
    pl.when(g == n_tiles)(functools.partial(step, False, True))


def _layer_spec(layer, shape):
    zeros = (0,) * len(shape)
    return pl.BlockSpec((None,) + shape, lambda g: (layer,) + zeros,
                        pipeline_mode=pl.Buffered(1))


def _layer(layer, x, mixg, w_in, qg, kg, rtab, sgug, wsp, bsp, ag, gg, w_out, ffng, w1, w2):
    B, S, D = x.shape
    tiles_per_row = S // TQ
    n_tiles = B * tiles_per_row
    bf16 = jnp.bfloat16
    f32 = jnp.float32

    def in_tile(g):
        gi = jnp.minimum(g, n_tiles - 1)
        return (gi // tiles_per_row, gi % tiles_per_row, 0)

    def out_tile(g):
        go = jnp.maximum(g - 1, 0)
        return (go // tiles_per_row, go % tiles_per_row, 0)

    hbm = pl.BlockSpec(memory_space=pl.ANY)
    return pl.pallas_call(
        functools.partial(_layer_kernel, layer, tiles_per_row, n_tiles),
        grid=(n_tiles + 1,),
        in_specs=[
            pl.BlockSpec((None, TQ, D), in_tile),
            _layer_spec(layer, (1, D)),
            hbm,
            _layer_spec(layer, (1, ATT_WIDTH)),
            _layer_spec(layer, (1, ATT_WIDTH)),
            _layer_spec(layer, (ATT_HEADS, MXU_COLS)),
            _layer_spec(layer, (1, GMLP_WIDTH)),
            _layer_spec(layer, (GMLP_GROUPS, GMLP_BLOCK, GMLP_BLOCK)),
            _layer_spec(layer, (GMLP_BLOCK, GMLP_WIDTH)),
            _layer_spec(layer, (1, ATT_WIDTH)),
            _layer_spec(layer, (1, GMLP_WIDTH)),
            hbm,
            _layer_spec(layer, (1, D)),
            hbm,
            hbm,
        ],
        out_specs=pl.BlockSpec((None, TQ, D), out_tile),
        out_shape=jax.ShapeDtypeStruct(x.shape, x.dtype),
        scratch_shapes=[
            pltpu.VMEM((LEFT + S, ATT_WIDTH), bf16),
            pltpu.VMEM((LEFT + S, ATT_WIDTH), bf16),
            pltpu.VMEM((ATT_HEADS, TQ, TK), f32),
            pltpu.VMEM((TQ, D), f32),
            pltpu.VMEM((TQ, D_FF), bf16),
            pltpu.VMEM((D, IN_WIDTH), bf16),
            pltpu.VMEM((D, D), bf16),
            pltpu.VMEM((D, 2 * D_FF), bf16),
            pltpu.VMEM((D_FF, D), bf16),
            pltpu.VMEM((STAGE_SLOTS, _stage_rows(IN_WIDTH), IN_WIDTH), f32),
            pltpu.VMEM((STAGE_SLOTS, _stage_rows(D), D), f32),
            pltpu.VMEM((STAGE_SLOTS, _stage_rows(2 * D_FF), 2 * D_FF), f32),
            pltpu.SemaphoreType.DMA((STAGE_SLOTS,)),
        ],
        compiler_params=pltpu.CompilerParams(
            dimension_semantics=("arbitrary",),
            vmem_limit_bytes=VMEM_LIMIT_BYTES),
        name="layer",
    )(x, mixg, w_in, qg, kg, rtab, sgug, wsp, bsp, ag, gg, w_out, ffng, w1, w2)


def _reversed_rel_table(rel_bias):
    near = rel_bias[:, :, ::-1][:, :, :NEAR]
    far = jnp.broadcast_to(rel_bias[:, :, 2 * MAX_REL:], near.shape[:2] + (MXU_COLS - NEAR,))
    return jnp.concatenate([near, far], axis=2).astype(jnp.float32)


def kernel(x, mix_norm_g, w_in, q_norm_g, k_norm_g, rel_bias, sgu_norm_g, w_spatial, b_spatial,
           att_out_norm_g, gmlp_out_norm_g, w_out, ffn_norm_g, w_ffn_in, w_ffn_out):
    depth = w_in.shape[0]
    row = lambda p: p[:, None, :]
    params = (
        row(mix_norm_g), w_in,
        row(jnp.tile(q_norm_g, (1, ATT_HEADS))), row(jnp.tile(k_norm_g, (1, ATT_HEADS))),
        _reversed_rel_table(rel_bias), row(sgu_norm_g), w_spatial,
        jnp.repeat(jnp.swapaxes(b_spatial, 1, 2), GMLP_GROUP_DIM, axis=2),
        row(att_out_norm_g), row(gmlp_out_norm_g), w_out,
        row(ffn_norm_g), w_ffn_in, w_ffn_out)
    for l in range(depth):
        x = _layer(l, x, *params)
    return x
```

```python
import functools

import jax
import jax.numpy as jnp
from jax import lax
from jax.experimental import pallas as pl
from jax.experimental.pallas import tpu as pltpu

D_MODEL = 1024
CHUNK = 64
ATT_HEADS = 8
HEAD_DIM = 64
ATT_WIDTH = ATT_HEADS * HEAD_DIM
LEFT_CHUNKS = 8
LEFT = LEFT_CHUNKS * CHUNK
MAX_REL = 2 * CHUNK
NEAR = MAX_REL + CHUNK
GMLP_WIDTH = 512
GMLP_GROUPS = 8
GMLP_GROUP_DIM = GMLP_WIDTH // GMLP_GROUPS
GMLP_BLOCK = 128
D_FF = 2816
KUG_WIDTH = ATT_WIDTH + 2 * GMLP_WIDTH
EPS = 1e-6
NEG_INF = -1e30

LANES = 128
BF16_ROWS = 16
MXU_COLS = 256
TQ = 256
TK = LEFT + TQ
PAD_TILES = LEFT // TQ
WIN_TILES = TK // TQ
FF_CHUNK = 256
HEADS_PER_ROUND = 2
STAGE_ROWS = {D_MODEL: 128, 2 * D_FF: 32}
STAGE_SLOTS = 4
VMEM_LIMIT_BYTES = 60 * 1024 * 1024


def _rms(x, g):
    ms = jnp.mean(x * x, axis=-1, keepdims=True)
    return (x * lax.rsqrt(ms + EPS)) * g


def _head_rms(z, gain):
    rows = z.shape[0]
    lo = lax.broadcasted_iota(jnp.int32, (rows, LANES), 1) < HEAD_DIM
    outs = []
    for c in range(ATT_WIDTH // LANES):
        zc = z[:, c * LANES:(c + 1) * LANES]
        sq = zc * zc
        ms_lo = jnp.sum(jnp.where(lo, sq, 0.0), axis=-1, keepdims=True) * (1.0 / HEAD_DIM)
        ms_hi = jnp.sum(jnp.where(lo, 0.0, sq), axis=-1, keepdims=True) * (1.0 / HEAD_DIM)
        r = jnp.where(lo, lax.rsqrt(ms_lo + EPS), lax.rsqrt(ms_hi + EPS))
        outs.append((zc * r) * gain[:, c * LANES:(c + 1) * LANES])
    return jnp.concatenate(outs, axis=-1)


def _build_bias(rtab_ref, bias_sc):
    f32 = jnp.float32
    row = lax.broadcasted_iota(jnp.int32, (CHUNK, MXU_COLS), 0)
    lo_half = lax.broadcasted_iota(jnp.int32, (CHUNK, LANES), 1) < CHUNK
    n_kc = TK // CHUNK
    for head in range(ATT_HEADS):
        r = rtab_ref[head:head + 1, :]
        far = rtab_ref[head:head + 1, MXU_COLS - 1:MXU_COLS]
        x0 = jnp.broadcast_to(r, (CHUNK, MXU_COLS))
        for b in range(CHUNK.bit_length() - 1):
            x0 = jnp.where(((row >> b) & 1) == 1, pltpu.roll(x0, 1 << b, axis=1), x0)
        x1 = pltpu.roll(x0, CHUNK, axis=1)
        far_piece = jnp.broadcast_to(far, (CHUNK, LANES))
        neg_piece = jnp.full((CHUNK, LANES), NEG_INF, f32)

        def half(qc, kc):
            rel = kc - qc
            if rel < 0 or rel > LEFT_CHUNKS:
                return neg_piece
            if rel < LEFT_CHUNKS - 2:
                return far_piece
            col = (rel - (LEFT_CHUNKS - 2)) * CHUNK + (qc % 2) * CHUNK
            src = x1 if qc % 2 else x0
            return src[:, (col // LANES) * LANES:(col // LANES + 1) * LANES]

        row_blocks = []
        for qc in range(TQ // CHUNK):
            pieces = [jnp.where(lo_half, half(qc, 2 * vcol), half(qc, 2 * vcol + 1))
                      for vcol in range(n_kc // 2)]
            row_blocks.append(jnp.concatenate(pieces, axis=1))
        bias_sc[head] = jnp.concatenate(row_blocks, axis=0).T


def _step(x, y_prev, t, row0, mixg_ref, w_qv_ref, w_kug_ref, qg_ref, kg_ref, sgug_ref, wsp_ref,
          bsp_ref, ag_ref, gg_ref, w_out_ref, ffng_ref, w1_ref, w2_ref,
          k_sc, v_sc, km_sc, bias_sc, act_sc):
    bf16 = jnp.bfloat16
    f32 = jnp.float32
    nt = (((1,), (1,)), ((), ()))
    h = _rms(x, mixg_ref[...]).astype(bf16)
    hf = _rms(y_prev, ffng_ref[...]).astype(bf16)

    def ffn_chunk(j):
        gate = jnp.dot(hf, w1_ref[:, j * FF_CHUNK:(j + 1) * FF_CHUNK], preferred_element_type=f32)
        up = jnp.dot(hf, w1_ref[:, D_FF + j * FF_CHUNK:D_FF + (j + 1) * FF_CHUNK],
                     preferred_element_type=f32)
        act_sc[:, j * FF_CHUNK:(j + 1) * FF_CHUNK] = (jax.nn.silu(gate) * up).astype(bf16)

    qv_t = lax.dot_general(w_qv_ref[...], h, nt, preferred_element_type=f32)
    k = jnp.dot(h, w_kug_ref[:, 0:ATT_WIDTH], preferred_element_type=f32)
    u = jnp.dot(h, w_kug_ref[:, ATT_WIDTH:ATT_WIDTH + GMLP_WIDTH], preferred_element_type=f32)
    vg = jnp.dot(h, w_kug_ref[:, ATT_WIDTH + GMLP_WIDTH:KUG_WIDTH], preferred_element_type=f32)

    q3 = qv_t[0:ATT_WIDTH].reshape(ATT_HEADS, HEAD_DIM, TQ)
    q_ms = jnp.mean(q3 * q3, axis=1, keepdims=True)
    q3 = (q3 * lax.rsqrt(q_ms + EPS)) * qg_ref[...].reshape(ATT_HEADS, HEAD_DIM, TQ)
    qn_t = (q3 * HEAD_DIM ** -0.5).reshape(ATT_WIDTH, TQ).astype(bf16)
    kn = _head_rms(k, kg_ref[...]).astype(bf16)
    k_sc[pl.ds(row0 + LEFT, TQ), :] = kn
    v_sc[t + PAD_TILES] = qv_t[ATT_WIDTH:2 * ATT_WIDTH].astype(bf16)
    kw = k_sc[pl.ds(row0, TK), :]
    km = km_sc[pl.ds(row0, TK), :]
    pad_rows = jnp.where(lax.broadcasted_iota(jnp.int32, (LANES, TQ), 0) == 0,
                         1.0, 0.0).astype(bf16)
    zero_rows = jnp.zeros((HEAD_DIM, TQ), bf16)
    ones_rows = jnp.ones((BF16_ROWS, TK), bf16)

    def scores(head):
        c = head // 2
        qh = qn_t[head * HEAD_DIM:(head + 1) * HEAD_DIM]
        pair = [qh, zero_rows] if head % 2 == 0 else [zero_rows, qh]
        rhs = jnp.concatenate(pair + [pad_rows], axis=0)
        lhs = jnp.concatenate([kw[:, c * LANES:(c + 1) * LANES], km], axis=1)
        s = jnp.dot(lhs, rhs, preferred_element_type=f32)
        return s + bias_sc[head]

    def probs(s):
        m = jnp.max(s, axis=0, keepdims=True)
        return jnp.exp(s - m).astype(bf16)

    def weighted_values(head, p):
        v_t = jnp.concatenate(
            [v_sc[t + w, head * HEAD_DIM:(head + 1) * HEAD_DIM, :] for w in range(WIN_TILES)],
            axis=1)
        pv = jnp.dot(jnp.concatenate([v_t, ones_rows], axis=0), p,
                     preferred_element_type=f32)
        return pv[0:HEAD_DIM] * (1.0 / pv[HEAD_DIM:HEAD_DIM + 1])

    rounds = [list(range(r, r + HEADS_PER_ROUND)) for r in range(0, ATT_HEADS, HEADS_PER_ROUND)]
    outs = []
    next_chunk = 0
    s_next = [scores(hd) for hd in rounds[0]]
    for ri, heads in enumerate(rounds):
        s_cur = s_next
        if ri + 1 < len(rounds):
            s_next = [scores(hd) for hd in rounds[ri + 1]]
        for _ in heads:
            ffn_chunk(next_chunk)
            next_chunk += 1
        ps = [probs(s) for s in s_cur]
        outs += [weighted_values(hd, p) for hd, p in zip(heads, ps)]
    a_t = jnp.concatenate(outs, axis=0)
    a_ms = jnp.mean(a_t * a_t, axis=0, keepdims=True)
    a_n = ((a_t * lax.rsqrt(a_ms + EPS)) * ag_ref[...]).T.astype(bf16)

    u_act = jax.nn.gelu(u)
    vgn = _rms(jax.nn.gelu(vg), sgug_ref[...]).astype(bf16)
    ti = lax.broadcasted_iota(jnp.int32, (GMLP_GROUPS, GMLP_BLOCK, GMLP_BLOCK), 1)
    si = lax.broadcasted_iota(jnp.int32, (GMLP_GROUPS, GMLP_BLOCK, GMLP_BLOCK), 2)
    causal = (ti // CHUNK) >= (si // CHUNK)
    wsp = jnp.where(causal, wsp_ref[...], 0.0).astype(bf16)
    wsp = wsp.reshape(GMLP_GROUPS * GMLP_BLOCK, GMLP_BLOCK)
    lane_g = lax.broadcasted_iota(jnp.int32, (GMLP_BLOCK, MXU_COLS), 1)
    groups_per_quad = MXU_COLS // GMLP_GROUP_DIM
    blocks = []
    for blk in range(TQ // GMLP_BLOCK):
        cols = []
        for c4 in range(GMLP_WIDTH // MXU_COLS):
            lhs = wsp[c4 * groups_per_quad * GMLP_BLOCK:(c4 + 1) * groups_per_quad * GMLP_BLOCK, :]
            rhs = vgn[blk * GMLP_BLOCK:(blk + 1) * GMLP_BLOCK, c4 * MXU_COLS:(c4 + 1) * MXU_COLS]
            res = jnp.dot(lhs, rhs, preferred_element_type=f32)
            acc = res[0:GMLP_BLOCK, :]
            for gg in range(1, groups_per_quad):
                acc = jnp.where(lane_g >= gg * GMLP_GROUP_DIM,
                                res[gg * GMLP_BLOCK:(gg + 1) * GMLP_BLOCK, :], acc)
            cols.append(acc)
            if next_chunk < D_FF // FF_CHUNK:
                ffn_chunk(next_chunk)
                next_chunk += 1
        blocks.append(jnp.concatenate(cols, axis=-1) + bsp_ref[...])
    while next_chunk < D_FF // FF_CHUNK:
        ffn_chunk(next_chunk)
        next_chunk += 1
    mixed = jnp.concatenate(blocks, axis=0)
    g_n = _rms(u_act * mixed, gg_ref[...]).astype(bf16)

    ffn_out = y_prev + jnp.dot(act_sc[...], w2_ref[...], preferred_element_type=f32)
    mix = jnp.concatenate([a_n, g_n], axis=-1)
    mix_out = x + jnp.dot(mix, w_out_ref[...], preferred_element_type=f32)
    return mix_out, ffn_out


def _load_weight(src, dst, stage, sem):
    rows, cols = src.shape
    chunk = STAGE_ROWS[cols]
    assert rows % chunk == 0 and stage.shape == (STAGE_SLOTS, chunk, cols)
    n = rows // chunk
    ahead = STAGE_SLOTS - 1
    assert n >= ahead

    def copy(i, slot):
        return pltpu.make_async_copy(src.at[pl.ds(i * chunk, chunk)], stage.at[slot], sem.at[slot])

    for i in range(ahead):
        copy(i, i).start()

    def body(i, carry):
        slot = lax.rem(i, STAGE_SLOTS)

        @pl.when(i + ahead < n)
        def _():
            copy(i + ahead, lax.rem(i + ahead, STAGE_SLOTS)).start()

        copy(i, slot).wait()
        dst[pl.ds(pl.multiple_of(i * chunk, chunk), chunk), :] = stage[slot].astype(dst.dtype)
        return carry

    lax.fori_loop(0, n, body, 0)


def _layer_kernel(layer, tiles_per_row, n_tiles,
                  x_ref, mixg_ref, w_qv_ref, w_kug_ref, qg_ref, kg_ref, rtab_ref, sgug_ref,
                  wsp_ref, bsp_ref, ag_ref, gg_ref, w_out_hbm, ffng_ref, w1_hbm, w2_hbm,
                  o_ref, k_sc, v_sc, km_sc, bias_sc, y_sc, act_sc,
                  w_out_sc, w1_sc, w2_sc, st_d, st_ff, sem):
    g = pl.program_id(0)
    t = lax.rem(jnp.minimum(g, n_tiles - 1), tiles_per_row)
    row0 = pl.multiple_of(t * TQ, TQ)

    @pl.when(g == 0)
    def _():
        _load_weight(w_out_hbm.at[layer], w_out_sc, st_d, sem)
        _load_weight(w1_hbm.at[layer], w1_sc, st_ff, sem)
        _load_weight(w2_hbm.at[layer], w2_sc, st_d, sem)
        _build_bias(rtab_ref, bias_sc)
        y_sc[...] = jnp.zeros(y_sc.shape, y_sc.dtype)
        is_pad = ((lax.broadcasted_iota(jnp.int32, km_sc.shape, 0) < LEFT)
                  & (lax.broadcasted_iota(jnp.int32, km_sc.shape, 1) == 0))
        km_sc[...] = jnp.where(is_pad, NEG_INF, 0.0).astype(km_sc.dtype)

    @pl.when(t == 0)
    def _():
        k_sc[0:LEFT, :] = jnp.zeros((LEFT, ATT_WIDTH), k_sc.dtype)
        v_sc[0:PAD_TILES] = jnp.zeros((PAD_TILES, ATT_WIDTH, TQ), v_sc.dtype)

    mix_out, ffn_out = _step(
        x_ref[...], y_sc[...], t, row0, mixg_ref, w_qv_ref, w_kug_ref, qg_ref, kg_ref, sgug_ref,
        wsp_ref, bsp_ref, ag_ref, gg_ref, w_out_sc, ffng_ref, w1_sc, w2_sc,
        k_sc, v_sc, km_sc, bias_sc, act_sc)
    o_ref[...] = ffn_out
    y_sc[...] = mix_out


def _layer_spec(layer, shape):
    zeros = (0,) * len(shape)
    return pl.BlockSpec((None,) + shape, lambda g: (layer,) + zeros,
                        pipeline_mode=pl.Buffered(1))


def _layer(layer, x, mixg, w_qv, w_kug, qg, kg, rtab, sgug, wsp, bsp, ag, gg, w_out, ffng, w1, w2):
    B, S, D = x.shape
    tiles_per_row = S // TQ
    n_tiles = B * tiles_per_row
    bf16 = jnp.bfloat16
    f32 = jnp.float32

    def in_tile(g):
        gi = jnp.minimum(g, n_tiles - 1)
        return (gi // tiles_per_row, gi % tiles_per_row, 0)

    def out_tile(g):
        go = jnp.maximum(g - 1, 0)
        return (go // tiles_per_row, go % tiles_per_row, 0)

    hbm = pl.BlockSpec(memory_space=pl.ANY)
    return pl.pallas_call(
        functools.partial(_layer_kernel, layer, tiles_per_row, n_tiles),
        grid=(n_tiles + 1,),
        in_specs=[
            pl.BlockSpec((None, TQ, D), in_tile),
            _layer_spec(layer, (1, D)),
            _layer_spec(layer, (2 * ATT_WIDTH, D)),
            _layer_spec(layer, (D, KUG_WIDTH)),
            _layer_spec(layer, (ATT_WIDTH, TQ)),
            _layer_spec(layer, (1, ATT_WIDTH)),
            _layer_spec(layer, (ATT_HEADS, MXU_COLS)),
            _layer_spec(layer, (1, GMLP_WIDTH)),
            _layer_spec(layer, (GMLP_GROUPS, GMLP_BLOCK, GMLP_BLOCK)),
            _layer_spec(layer, (GMLP_BLOCK, GMLP_WIDTH)),
            _layer_spec(layer, (ATT_WIDTH, TQ)),
            _layer_spec(layer, (1, GMLP_WIDTH)),
            hbm,
            _layer_spec(layer, (1, D)),
            hbm,
            hbm,
        ],
        out_specs=pl.BlockSpec((None, TQ, D), out_tile),
        out_shape=jax.ShapeDtypeStruct(x.shape, x.dtype),
        scratch_shapes=[
            pltpu.VMEM((LEFT + S, ATT_WIDTH), bf16),
            pltpu.VMEM((PAD_TILES + tiles_per_row, ATT_WIDTH, TQ), bf16),
            pltpu.VMEM((LEFT + S, LANES), bf16),
            pltpu.VMEM((ATT_HEADS, TK, TQ), f32),
            pltpu.VMEM((TQ, D), f32),
            pltpu.VMEM((TQ, D_FF), bf16),
            pltpu.VMEM((D, D), bf16),
            pltpu.VMEM((D, 2 * D_FF), bf16),
            pltpu.VMEM((D_FF, D), bf16),
            pltpu.VMEM((STAGE_SLOTS, STAGE_ROWS[D], D), f32),
            pltpu.VMEM((STAGE_SLOTS, STAGE_ROWS[2 * D_FF], 2 * D_FF), f32),
            pltpu.SemaphoreType.DMA((STAGE_SLOTS,)),
        ],
        compiler_params=pltpu.CompilerParams(
            dimension_semantics=("arbitrary",),
            vmem_limit_bytes=VMEM_LIMIT_BYTES),
        name="layer",
    )(x, mixg, w_qv, w_kug, qg, kg, rtab, sgug, wsp, bsp, ag, gg, w_out, ffng, w1, w2)


def _reversed_rel_table(rel_bias):
    near = rel_bias[:, :, ::-1][:, :, :NEAR]
    far = jnp.broadcast_to(rel_bias[:, :, 2 * MAX_REL:], near.shape[:2] + (MXU_COLS - NEAR,))
    return jnp.concatenate([near, far], axis=2).astype(jnp.float32)


def _feature_column(g, reps):
    col = jnp.tile(g, (1, reps))[:, :, None]
    return jnp.broadcast_to(col, col.shape[:2] + (TQ,))


def kernel(x, mix_norm_g, w_in, q_norm_g, k_norm_g, rel_bias, sgu_norm_g, w_spatial, b_spatial,
           att_out_norm_g, gmlp_out_norm_g, w_out, ffn_norm_g, w_ffn_in, w_ffn_out):
    depth = w_in.shape[0]
    bf16 = jnp.bfloat16
    row = lambda p: p[:, None, :]
    w_in_b = w_in.astype(bf16)
    w_q, w_k, w_v, w_ug = (w_in_b[:, :, 0:ATT_WIDTH], w_in_b[:, :, ATT_WIDTH:2 * ATT_WIDTH],
                           w_in_b[:, :, 2 * ATT_WIDTH:3 * ATT_WIDTH], w_in_b[:, :, 3 * ATT_WIDTH:])
    w_qv_t = jnp.swapaxes(jnp.concatenate([w_q, w_v], axis=2), 1, 2)
    w_kug = jnp.concatenate([w_k, w_ug], axis=2)
    params = (
        row(mix_norm_g), w_qv_t, w_kug,
        _feature_column(q_norm_g, ATT_HEADS), row(jnp.tile(k_norm_g, (1, ATT_HEADS))),
        _reversed_rel_table(rel_bias), row(sgu_norm_g), w_spatial,
        jnp.repeat(jnp.swapaxes(b_spatial, 1, 2), GMLP_GROUP_DIM, axis=2),
        _feature_column(att_out_norm_g, 1), row(gmlp_out_norm_g), w_out,
        row(ffn_norm_g), w_ffn_in, w_ffn_out)
    for l in range(depth):
        x = _layer(l, x, *params)
    return x
```

```python
import functools

import jax
import jax.numpy as jnp
from jax import lax
from jax.experimental import pallas as pl
from jax.experimental.pallas import tpu as pltpu

D_MODEL = 1024
CHUNK = 64
ATT_HEADS = 8
HEAD_DIM = 64
ATT_WIDTH = ATT_HEADS * HEAD_DIM
LEFT_CHUNKS = 8
LEFT = LEFT_CHUNKS * CHUNK
MAX_REL = 2 * CHUNK
NEAR = MAX_REL + CHUNK
GMLP_WIDTH = 512
GMLP_GROUPS = 8
GMLP_GROUP_DIM = GMLP_WIDTH // GMLP_GROUPS
GMLP_BLOCK = 128
D_FF = 2816
KUG_WIDTH = ATT_WIDTH + 2 * GMLP_WIDTH
EPS = 1e-6
NEG_INF = -1e30

LANES = 128
BF16_ROWS = 16
MXU_COLS = 256
TQ = 256
TK = LEFT + TQ
PAD_TILES = LEFT // TQ
WIN_TILES = TK // TQ
FF_CHUNK = 256
HEADS_PER_ROUND = 2
STAGE_ROWS = {D_MODEL: 128, 2 * D_FF: 32}
STAGE_SLOTS = 4
VMEM_LIMIT_BYTES = 60 * 1024 * 1024


def _rms(x, g):
    ms = jnp.mean(x * x, axis=-1, keepdims=True)
    return (x * lax.rsqrt(ms + EPS)) * g


def _head_rms(z, gain):
    rows = z.shape[0]
    lo = lax.broadcasted_iota(jnp.int32, (rows, LANES), 1) < HEAD_DIM
    outs = []
    for c in range(ATT_WIDTH // LANES):
        zc = z[:, c * LANES:(c + 1) * LANES]
        sq = zc * zc
        ms_lo = jnp.sum(jnp.where(lo, sq, 0.0), axis=-1, keepdims=True) * (1.0 / HEAD_DIM)
        ms_hi = jnp.sum(jnp.where(lo, 0.0, sq), axis=-1, keepdims=True) * (1.0 / HEAD_DIM)
        r = jnp.where(lo, lax.rsqrt(ms_lo + EPS), lax.rsqrt(ms_hi + EPS))
        outs.append((zc * r) * gain[:, c * LANES:(c + 1) * LANES])
    return jnp.concatenate(outs, axis=-1)


def _build_bias(rtab_ref, bias_sc):
    f32 = jnp.float32
    row = lax.broadcasted_iota(jnp.int32, (CHUNK, MXU_COLS), 0)
    lo_half = lax.broadcasted_iota(jnp.int32, (CHUNK, LANES), 1) < CHUNK
    n_kc = TK // CHUNK
    for head in range(ATT_HEADS):
        r = rtab_ref[head:head + 1, :]
        far = rtab_ref[head:head + 1, MXU_COLS - 1:MXU_COLS]
        x0 = jnp.broadcast_to(r, (CHUNK, MXU_COLS))
        for b in range(CHUNK.bit_length() - 1):
            x0 = jnp.where(((row >> b) & 1) == 1, pltpu.roll(x0, 1 << b, axis=1), x0)
        x1 = pltpu.roll(x0, CHUNK, axis=1)
        far_piece = jnp.broadcast_to(far, (CHUNK, LANES))
        neg_piece = jnp.full((CHUNK, LANES), NEG_INF, f32)

        def half(qc, kc):
            rel = kc - qc
            if rel < 0 or rel > LEFT_CHUNKS:
                return neg_piece
            if rel < LEFT_CHUNKS - 2:
                return far_piece
            col = (rel - (LEFT_CHUNKS - 2)) * CHUNK + (qc % 2) * CHUNK
            src = x1 if qc % 2 else x0
            return src[:, (col // LANES) * LANES:(col // LANES + 1) * LANES]

        row_blocks = []
        for qc in range(TQ // CHUNK):
            pieces = [jnp.where(lo_half, half(qc, 2 * vcol), half(qc, 2 * vcol + 1))
                      for vcol in range(n_kc // 2)]
            row_blocks.append(jnp.concatenate(pieces, axis=1))
        bias_sc[head] = jnp.concatenate(row_blocks, axis=0).T


def _step(x, y_prev, t, row0, mixg_ref, w_qv_ref, w_kug_ref, qg_ref, kg_ref, sgug_ref, wsp_ref,
          bsp_ref, ag_ref, gg_ref, w_out_ref, ffng_ref, w1_ref, w2_ref,
          k_sc, v_sc, km_sc, bias_sc, act_sc):
    bf16 = jnp.bfloat16
    f32 = jnp.float32
    nt = (((1,), (1,)), ((), ()))
    h = _rms(x, mixg_ref[...]).astype(bf16)
    hf = _rms(y_prev, ffng_ref[...]).astype(bf16)

    def ffn_chunk(j):
        gate = jnp.dot(hf, w1_ref[:, j * FF_CHUNK:(j + 1) * FF_CHUNK], preferred_element_type=f32)
        up = jnp.dot(hf, w1_ref[:, D_FF + j * FF_CHUNK:D_FF + (j + 1) * FF_CHUNK],
                     preferred_element_type=f32)
        act_sc[:, j * FF_CHUNK:(j + 1) * FF_CHUNK] = (jax.nn.silu(gate) * up).astype(bf16)

    qv_t = lax.dot_general(w_qv_ref[...], h, nt, preferred_element_type=f32)
    k = jnp.dot(h, w_kug_ref[:, 0:ATT_WIDTH], preferred_element_type=f32)
    u = jnp.dot(h, w_kug_ref[:, ATT_WIDTH:ATT_WIDTH + GMLP_WIDTH], preferred_element_type=f32)
    vg = jnp.dot(h, w_kug_ref[:, ATT_WIDTH + GMLP_WIDTH:KUG_WIDTH], preferred_element_type=f32)

    q3 = qv_t[0:ATT_WIDTH].reshape(ATT_HEADS, HEAD_DIM, TQ)
    q_ms = jnp.mean(q3 * q3, axis=1, keepdims=True)
    q3 = (q3 * lax.rsqrt(q_ms + EPS)) * qg_ref[...].reshape(ATT_HEADS, HEAD_DIM, TQ)
    qn_t = (q3 * HEAD_DIM ** -0.5).reshape(ATT_WIDTH, TQ).astype(bf16)
    kn = _head_rms(k, kg_ref[...]).astype(bf16)
    k_sc[pl.ds(row0 + LEFT, TQ), :] = kn
    v_sc[t + PAD_TILES] = qv_t[ATT_WIDTH:2 * ATT_WIDTH].astype(bf16)
    kw = k_sc[pl.ds(row0, TK), :]
    km = km_sc[pl.ds(row0, TK), :]
    pad_rows = jnp.where(lax.broadcasted_iota(jnp.int32, (LANES, TQ), 0) == 0,
                         1.0, 0.0).astype(bf16)
    zero_rows = jnp.zeros((HEAD_DIM, TQ), bf16)
    ones_rows = jnp.ones((BF16_ROWS, TK), bf16)

    def scores(head):
        c = head // 2
        qh = qn_t[head * HEAD_DIM:(head + 1) * HEAD_DIM]
        pair = [qh, zero_rows] if head % 2 == 0 else [zero_rows, qh]
        rhs = jnp.concatenate(pair + [pad_rows], axis=0)
        lhs = jnp.concatenate([kw[:, c * LANES:(c + 1) * LANES], km], axis=1)
        s = jnp.dot(lhs, rhs, preferred_element_type=f32)
        return s + bias_sc[head]

    def probs(s):
        m = jnp.max(s, axis=0, keepdims=True)
        return jnp.exp(s - m).astype(bf16)

    def weighted_values(head, p):
        v_t = jnp.concatenate(
            [v_sc[t + w, head * HEAD_DIM:(head + 1) * HEAD_DIM, :] for w in range(WIN_TILES)],
            axis=1)
        pv = jnp.dot(jnp.concatenate([v_t, ones_rows], axis=0), p,
                     preferred_element_type=f32)
        return pv[0:HEAD_DIM] * (1.0 / pv[HEAD_DIM:HEAD_DIM + 1])

    rounds = [list(range(r, r + HEADS_PER_ROUND)) for r in range(0, ATT_HEADS, HEADS_PER_ROUND)]
    outs = []
    next_chunk = 0
    s_next = [scores(hd) for hd in rounds[0]]
    for ri, heads in enumerate(rounds):
        s_cur = s_next
        if ri + 1 < len(rounds):
            s_next = [scores(hd) for hd in rounds[ri + 1]]
        for _ in heads:
            ffn_chunk(next_chunk)
            next_chunk += 1
        ps = [probs(s) for s in s_cur]
        outs += [weighted_values(hd, p) for hd, p in zip(heads, ps)]
    a_t = jnp.concatenate(outs, axis=0)
    a_ms = jnp.mean(a_t * a_t, axis=0, keepdims=True)
    a_n = ((a_t * lax.rsqrt(a_ms + EPS)) * ag_ref[...]).T.astype(bf16)

    u_act = jax.nn.gelu(u)
    vgn = _rms(jax.nn.gelu(vg), sgug_ref[...]).astype(bf16)
    ti = lax.broadcasted_iota(jnp.int32, (GMLP_GROUPS, GMLP_BLOCK, GMLP_BLOCK), 1)
    si = lax.broadcasted_iota(jnp.int32, (GMLP_GROUPS, GMLP_BLOCK, GMLP_BLOCK), 2)
    causal = (ti // CHUNK) >= (si // CHUNK)
    wsp = jnp.where(causal, wsp_ref[...], 0.0).astype(bf16)
    wsp = wsp.reshape(GMLP_GROUPS * GMLP_BLOCK, GMLP_BLOCK)
    first_of_pair = lax.broadcasted_iota(jnp.int32, (GMLP_BLOCK, LANES), 1) < GMLP_GROUP_DIM
    n_blk = TQ // GMLP_BLOCK
    pair_cols = []
    for pr in range(GMLP_WIDTH // LANES):
        lhs = wsp[2 * pr * GMLP_BLOCK:2 * (pr + 1) * GMLP_BLOCK, :]
        rhs = jnp.concatenate(
            [vgn[blk * GMLP_BLOCK:(blk + 1) * GMLP_BLOCK, pr * LANES:(pr + 1) * LANES]
             for blk in range(n_blk)], axis=1)
        res = jnp.dot(lhs, rhs, preferred_element_type=f32)
        pair_cols.append(jnp.concatenate(
            [jnp.where(first_of_pair,
                       res[0:GMLP_BLOCK, blk * LANES:(blk + 1) * LANES],
                       res[GMLP_BLOCK:2 * GMLP_BLOCK, blk * LANES:(blk + 1) * LANES])
             for blk in range(n_blk)], axis=0))
        if next_chunk < D_FF // FF_CHUNK:
            ffn_chunk(next_chunk)
            next_chunk += 1
    while next_chunk < D_FF // FF_CHUNK:
        ffn_chunk(next_chunk)
        next_chunk += 1
    mixed = (jnp.concatenate(pair_cols, axis=1)
             + jnp.concatenate([bsp_ref[...]] * n_blk, axis=0))
    g_n = _rms(u_act * mixed, gg_ref[...]).astype(bf16)

    ffn_out = y_prev + jnp.dot(act_sc[...], w2_ref[...], preferred_element_type=f32)
    mix = jnp.concatenate([a_n, g_n], axis=-1)
    mix_out = x + jnp.dot(mix, w_out_ref[...], preferred_element_type=f32)
    return mix_out, ffn_out


def _load_weight(src, dst, stage, sem):
    rows, cols = src.shape
    chunk = STAGE_ROWS[cols]
    assert rows % chunk == 0 and stage.shape == (STAGE_SLOTS, chunk, cols)
    n = rows // chunk
    ahead = STAGE_SLOTS - 1
    assert n >= ahead

    def copy(i, slot):
        return pltpu.make_async_copy(src.at[pl.ds(i * chunk, chunk)], stage.at[slot], sem.at[slot])

    for i in range(ahead):
        copy(i, i).start()

    def body(i, carry):
        slot = lax.rem(i, STAGE_SLOTS)

        @pl.when(i + ahead < n)
        def _():
            copy(i + ahead, lax.rem(i + ahead, STAGE_SLOTS)).start()

        copy(i, slot).wait()
        dst[pl.ds(pl.multiple_of(i * chunk, chunk), chunk), :] = stage[slot].astype(dst.dtype)
        return carry

    lax.fori_loop(0, n, body, 0)


def _layer_kernel(layer, tiles_per_row, n_tiles,
                  x_ref, mixg_ref, w_qv_ref, w_kug_ref, qg_ref, kg_ref, rtab_ref, sgug_ref,
                  wsp_ref, bsp_ref, ag_ref, gg_ref, w_out_hbm, ffng_ref, w1_hbm, w2_hbm,
                  o_ref, k_sc, v_sc, km_sc, bias_sc, y_sc, act_sc,
                  w_out_sc, w1_sc, w2_sc, st_d, st_ff, sem):
    g = pl.program_id(0)
    t = lax.rem(jnp.minimum(g, n_tiles - 1), tiles_per_row)
    row0 = pl.multiple_of(t * TQ, TQ)

    @pl.when(g == 0)
    def _():
        _load_weight(w_out_hbm.at[layer], w_out_sc, st_d, sem)
        _load_weight(w1_hbm.at[layer], w1_sc, st_ff, sem)
        _load_weight(w2_hbm.at[layer], w2_sc, st_d, sem)
        _build_bias(rtab_ref, bias_sc)
        y_sc[...] = jnp.zeros(y_sc.shape, y_sc.dtype)
        is_pad = ((lax.broadcasted_iota(jnp.int32, km_sc.shape, 0) < LEFT)
                  & (lax.broadcasted_iota(jnp.int32, km_sc.shape, 1) == 0))
        km_sc[...] = jnp.where(is_pad, NEG_INF, 0.0).astype(km_sc.dtype)

    @pl.when(t == 0)
    def _():
        k_sc[0:LEFT, :] = jnp.zeros((LEFT, ATT_WIDTH), k_sc.dtype)
        v_sc[0:PAD_TILES] = jnp.zeros((PAD_TILES, ATT_WIDTH, TQ), v_sc.dtype)

    mix_out, ffn_out = _step(
        x_ref[...], y_sc[...], t, row0, mixg_ref, w_qv_ref, w_kug_ref, qg_ref, kg_ref, sgug_ref,
        wsp_ref, bsp_ref, ag_ref, gg_ref, w_out_sc, ffng_ref, w1_sc, w2_sc,
        k_sc, v_sc, km_sc, bias_sc, act_sc)
    o_ref[...] = ffn_out
    y_sc[...] = mix_out


def _layer_spec(layer, shape):
    zeros = (0,) * len(shape)
    return pl.BlockSpec((None,) + shape, lambda g: (layer,) + zeros,
                        pipeline_mode=pl.Buffered(1))


def _layer(layer, x, mixg, w_qv, w_kug, qg, kg, rtab, sgug, wsp, bsp, ag, gg, w_out, ffng, w1, w2):
    B, S, D = x.shape
    tiles_per_row = S // TQ
    n_tiles = B * tiles_per_row
    bf16 = jnp.bfloat16
    f32 = jnp.float32

    def in_tile(g):
        gi = jnp.minimum(g, n_tiles - 1)
        return (gi // tiles_per_row, gi % tiles_per_row, 0)

    def out_tile(g):
        go = jnp.maximum(g - 1, 0)
        return (go // tiles_per_row, go % tiles_per_row, 0)

    hbm = pl.BlockSpec(memory_space=pl.ANY)
    return pl.pallas_call(
        functools.partial(_layer_kernel, layer, tiles_per_row, n_tiles),
        grid=(n_tiles + 1,),
        in_specs=[
            pl.BlockSpec((None, TQ, D), in_tile),
            _layer_spec(layer, (1, D)),
            _layer_spec(layer, (2 * ATT_WIDTH, D)),
            _layer_spec(layer, (D, KUG_WIDTH)),
            _layer_spec(layer, (ATT_WIDTH, TQ)),
            _layer_spec(layer, (1, ATT_WIDTH)),
            _layer_spec(layer, (ATT_HEADS, MXU_COLS)),
            _layer_spec(layer, (1, GMLP_WIDTH)),
            _layer_spec(layer, (GMLP_GROUPS, GMLP_BLOCK, GMLP_BLOCK)),
            _layer_spec(layer, (GMLP_BLOCK, GMLP_WIDTH)),
            _layer_spec(layer, (ATT_WIDTH, TQ)),
            _layer_spec(layer, (1, GMLP_WIDTH)),
            hbm,
            _layer_spec(layer, (1, D)),
            hbm,
            hbm,
        ],
        out_specs=pl.BlockSpec((None, TQ, D), out_tile),
        out_shape=jax.ShapeDtypeStruct(x.shape, x.dtype),
        scratch_shapes=[
            pltpu.VMEM((LEFT + S, ATT_WIDTH), bf16),
            pltpu.VMEM((PAD_TILES + tiles_per_row, ATT_WIDTH, TQ), bf16),
            pltpu.VMEM((LEFT + S, LANES), bf16),
            pltpu.VMEM((ATT_HEADS, TK, TQ), f32),
            pltpu.VMEM((TQ, D), f32),
            pltpu.VMEM((TQ, D_FF), bf16),
            pltpu.VMEM((D, D), bf16),
            pltpu.VMEM((D, 2 * D_FF), bf16),
            pltpu.VMEM((D_FF, D), bf16),
            pltpu.VMEM((STAGE_SLOTS, STAGE_ROWS[D], D), f32),
            pltpu.VMEM((STAGE_SLOTS, STAGE_ROWS[2 * D_FF], 2 * D_FF), f32),
            pltpu.SemaphoreType.DMA((STAGE_SLOTS,)),
        ],
        compiler_params=pltpu.CompilerParams(
            dimension_semantics=("arbitrary",),
            vmem_limit_bytes=VMEM_LIMIT_BYTES),
        name="layer",
    )(x, mixg, w_qv, w_kug, qg, kg, rtab, sgug, wsp, bsp, ag, gg, w_out, ffng, w1, w2)


def _reversed_rel_table(rel_bias):
    near = rel_bias[:, :, ::-1][:, :, :NEAR]
    far = jnp.broadcast_to(rel_bias[:, :, 2 * MAX_REL:], near.shape[:2] + (MXU_COLS - NEAR,))
    return jnp.concatenate([near, far], axis=2).astype(jnp.float32)


def _feature_column(g, reps):
    col = jnp.tile(g, (1, reps))[:, :, None]
    return jnp.broadcast_to(col, col.shape[:2] + (TQ,))


def kernel(x, mix_norm_g, w_in, q_norm_g, k_norm_g, rel_bias, sgu_norm_g, w_spatial, b_spatial,
           att_out_norm_g, gmlp_out_norm_g, w_out, ffn_norm_g, w_ffn_in, w_ffn_out):
    depth = w_in.shape[0]
    bf16 = jnp.bfloat16
    row = lambda p: p[:, None, :]
    w_in_b = w_in.astype(bf16)
    w_q, w_k, w_v, w_ug = (w_in_b[:, :, 0:ATT_WIDTH], w_in_b[:, :, ATT_WIDTH:2 * ATT_WIDTH],
                           w_in_b[:, :, 2 * ATT_WIDTH:3 * ATT_WIDTH], w_in_b[:, :, 3 * ATT_WIDTH:])
    w_qv_t = jnp.swapaxes(jnp.concatenate([w_q, w_v], axis=2), 1, 2)
    w_kug = jnp.concatenate([w_k, w_ug], axis=2)
    params = (
        row(mix_norm_g), w_qv_t, w_kug,
        _feature_column(q_norm_g, ATT_HEADS), row(jnp.tile(k_norm_g, (1, ATT_HEADS))),
        _reversed_rel_table(rel_bias), row(sgu_norm_g), w_spatial,
        jnp.repeat(jnp.swapaxes(b_spatial, 1, 2), GMLP_GROUP_DIM, axis=2),
        _feature_column(att_out_norm_g, 1), row(gmlp_out_norm_g), w_out,
        row(ffn_norm_g), w_ffn_in, w_ffn_out)
    for l in range(depth):
        x = _layer(l, x, *params)
    return x
```

```python
import functools

import jax
import jax.numpy as jnp
from jax import lax
from jax.experimental import pallas as pl
from jax.experimental.pallas import tpu as pltpu

D_MODEL = 1024
CHUNK = 64
ATT_HEADS = 8
HEAD_DIM = 64
ATT_WIDTH = ATT_HEADS * HEAD_DIM
LEFT_CHUNKS = 8
LEFT = LEFT_CHUNKS * CHUNK
MAX_REL = 2 * CHUNK
NEAR = MAX_REL + CHUNK
GMLP_WIDTH = 512
GMLP_GROUPS = 8
GMLP_GROUP_DIM = GMLP_WIDTH // GMLP_GROUPS
GMLP_BLOCK = 128
D_FF = 2816
KUG_WIDTH = ATT_WIDTH + 2 * GMLP_WIDTH
EPS = 1e-6
NEG_INF = -1e30

LANES = 128
BF16_ROWS = 16
MXU_COLS = 256
TQ = 256
TK = LEFT + TQ
PAD_TILES = LEFT // TQ
WIN_TILES = TK // TQ
FF_CHUNK = 256
HEADS_PER_ROUND = 2
STAGE_ROWS = {ATT_WIDTH: 128, D_MODEL: 128, 2 * D_FF: 32}
GAIN_ROWS = 8
STAGE_SLOTS = 4
VMEM_LIMIT_BYTES = 60 * 1024 * 1024


def _rms(x, g):
    ms = jnp.mean(x * x, axis=-1, keepdims=True)
    return (x * lax.rsqrt(ms + EPS)) * g


def _head_rms(z, gain):
    rows = z.shape[0]
    lo = lax.broadcasted_iota(jnp.int32, (rows, LANES), 1) < HEAD_DIM
    outs = []
    for c in range(ATT_WIDTH // LANES):
        zc = z[:, c * LANES:(c + 1) * LANES]
        sq = zc * zc
        ms_lo = jnp.sum(jnp.where(lo, sq, 0.0), axis=-1, keepdims=True) * (1.0 / HEAD_DIM)
        ms_hi = jnp.sum(jnp.where(lo, 0.0, sq), axis=-1, keepdims=True) * (1.0 / HEAD_DIM)
        r = jnp.where(lo, lax.rsqrt(ms_lo + EPS), lax.rsqrt(ms_hi + EPS))
        outs.append((zc * r) * gain[:, c * LANES:(c + 1) * LANES])
    return jnp.concatenate(outs, axis=-1)


def _build_bias(rtab_ref, bias_sc):
    f32 = jnp.float32
    row = lax.broadcasted_iota(jnp.int32, (CHUNK, MXU_COLS), 0)
    lo_half = lax.broadcasted_iota(jnp.int32, (CHUNK, LANES), 1) < CHUNK
    n_kc = TK // CHUNK
    for head in range(ATT_HEADS):
        r = rtab_ref[head:head + 1, :]
        far = rtab_ref[head:head + 1, MXU_COLS - 1:MXU_COLS]
        x0 = jnp.broadcast_to(r, (CHUNK, MXU_COLS))
        for b in range(CHUNK.bit_length() - 1):
            x0 = jnp.where(((row >> b) & 1) == 1, pltpu.roll(x0, 1 << b, axis=1), x0)
        x1 = pltpu.roll(x0, CHUNK, axis=1)
        far_piece = jnp.broadcast_to(far, (CHUNK, LANES))
        neg_piece = jnp.full((CHUNK, LANES), NEG_INF, f32)

        def half(qc, kc):
            rel = kc - qc
            if rel < 0 or rel > LEFT_CHUNKS:
                return neg_piece
            if rel < LEFT_CHUNKS - 2:
                return far_piece
            col = (rel - (LEFT_CHUNKS - 2)) * CHUNK + (qc % 2) * CHUNK
            src = x1 if qc % 2 else x0
            return src[:, (col // LANES) * LANES:(col // LANES + 1) * LANES]

        row_blocks = []
        for qc in range(TQ // CHUNK):
            pieces = [jnp.where(lo_half, half(qc, 2 * vcol), half(qc, 2 * vcol + 1))
                      for vcol in range(n_kc // 2)]
            row_blocks.append(jnp.concatenate(pieces, axis=1))
        bias_sc[head] = jnp.concatenate(row_blocks, axis=0).T


def _step(x, y_prev, t, row0, mixg_ref, w_qv_ref, w_kug_ref, qg_ref, kg_ref, sgug_ref, wsp_ref,
          bsp_ref, ag_ref, gg_ref, w_out_ref, ffng_ref, w1_ref, w2_ref,
          k_sc, v_sc, km_sc, bias_sc, act_sc):
    bf16 = jnp.bfloat16
    f32 = jnp.float32
    nt = (((1,), (1,)), ((), ()))
    h = _rms(x, mixg_ref[...]).astype(bf16)
    hf = _rms(y_prev, ffng_ref[...]).astype(bf16)

    def ffn_chunk(j):
        gate = jnp.dot(hf, w1_ref[:, j * FF_CHUNK:(j + 1) * FF_CHUNK], preferred_element_type=f32)
        up = jnp.dot(hf, w1_ref[:, D_FF + j * FF_CHUNK:D_FF + (j + 1) * FF_CHUNK],
                     preferred_element_type=f32)
        act_sc[:, j * FF_CHUNK:(j + 1) * FF_CHUNK] = (jax.nn.silu(gate) * up).astype(bf16)

    qv_t = lax.dot_general(w_qv_ref[...], h, nt, preferred_element_type=f32)
    k = jnp.dot(h, w_kug_ref[:, 0:ATT_WIDTH], preferred_element_type=f32)
    u = jnp.dot(h, w_kug_ref[:, ATT_WIDTH:ATT_WIDTH + GMLP_WIDTH], preferred_element_type=f32)
    vg = jnp.dot(h, w_kug_ref[:, ATT_WIDTH + GMLP_WIDTH:KUG_WIDTH], preferred_element_type=f32)

    q3 = qv_t[0:ATT_WIDTH].reshape(ATT_HEADS, HEAD_DIM, TQ)
    q_ms = jnp.mean(q3 * q3, axis=1, keepdims=True)
    q3 = (q3 * lax.rsqrt(q_ms + EPS)) * qg_ref[...].reshape(ATT_HEADS, HEAD_DIM, TQ)
    qn_t = (q3 * HEAD_DIM ** -0.5).reshape(ATT_WIDTH, TQ).astype(bf16)
    kn = _head_rms(k, kg_ref[...]).astype(bf16)
    k_sc[pl.ds(row0 + LEFT, TQ), :] = kn
    v_sc[t + PAD_TILES] = qv_t[ATT_WIDTH:2 * ATT_WIDTH].astype(bf16)
    kw = k_sc[pl.ds(row0, TK), :]
    km = km_sc[pl.ds(row0, TK), :]
    pad_rows = jnp.where(lax.broadcasted_iota(jnp.int32, (LANES, TQ), 0) == 0,
                         1.0, 0.0).astype(bf16)
    zero_rows = jnp.zeros((HEAD_DIM, TQ), bf16)
    ones_rows = jnp.ones((BF16_ROWS, TK), bf16)

    def scores(head):
        c = head // 2
        qh = qn_t[head * HEAD_DIM:(head + 1) * HEAD_DIM]
        pair = [qh, zero_rows] if head % 2 == 0 else [zero_rows, qh]
        rhs = jnp.concatenate(pair + [pad_rows], axis=0)
        lhs = jnp.concatenate([kw[:, c * LANES:(c + 1) * LANES], km], axis=1)
        s = jnp.dot(lhs, rhs, preferred_element_type=f32)
        return s + bias_sc[head]

    def probs(s):
        m = jnp.max(s, axis=0, keepdims=True)
        return jnp.exp(s - m).astype(bf16)

    def weighted_values(head, p):
        v_t = jnp.concatenate(
            [v_sc[t + w, head * HEAD_DIM:(head + 1) * HEAD_DIM, :] for w in range(WIN_TILES)],
            axis=1)
        pv = jnp.dot(jnp.concatenate([v_t, ones_rows], axis=0), p,
                     preferred_element_type=f32)
        return pv[0:HEAD_DIM] * (1.0 / pv[HEAD_DIM:HEAD_DIM + 1])

    rounds = [list(range(r, r + HEADS_PER_ROUND)) for r in range(0, ATT_HEADS, HEADS_PER_ROUND)]
    outs = []
    next_chunk = 0
    s_next = [scores(hd) for hd in rounds[0]]
    for ri, heads in enumerate(rounds):
        s_cur = s_next
        if ri + 1 < len(rounds):
            s_next = [scores(hd) for hd in rounds[ri + 1]]
        for _ in heads:
            ffn_chunk(next_chunk)
            next_chunk += 1
        ps = [probs(s) for s in s_cur]
        outs += [weighted_values(hd, p) for hd, p in zip(heads, ps)]
    a_t = jnp.concatenate(outs, axis=0)
    a_ms = jnp.mean(a_t * a_t, axis=0, keepdims=True)
    a_n = ((a_t * lax.rsqrt(a_ms + EPS)) * ag_ref[...]).T.astype(bf16)

    u_act = jax.nn.gelu(u)
    vgn = _rms(jax.nn.gelu(vg), sgug_ref[...]).astype(bf16)
    ti = lax.broadcasted_iota(jnp.int32, (GMLP_GROUPS, GMLP_BLOCK, GMLP_BLOCK), 1)
    si = lax.broadcasted_iota(jnp.int32, (GMLP_GROUPS, GMLP_BLOCK, GMLP_BLOCK), 2)
    causal = (ti // CHUNK) >= (si // CHUNK)
    wsp = jnp.where(causal, wsp_ref[...], 0.0).astype(bf16)
    wsp = wsp.reshape(GMLP_GROUPS * GMLP_BLOCK, GMLP_BLOCK)
    first_of_pair = lax.broadcasted_iota(jnp.int32, (GMLP_BLOCK, LANES), 1) < GMLP_GROUP_DIM
    n_blk = TQ // GMLP_BLOCK
    pair_cols = []
    for pr in range(GMLP_WIDTH // LANES):
        lhs = wsp[2 * pr * GMLP_BLOCK:2 * (pr + 1) * GMLP_BLOCK, :]
        rhs = jnp.concatenate(
            [vgn[blk * GMLP_BLOCK:(blk + 1) * GMLP_BLOCK, pr * LANES:(pr + 1) * LANES]
             for blk in range(n_blk)], axis=1)
        res = jnp.dot(lhs, rhs, preferred_element_type=f32)
        pair_cols.append(jnp.concatenate(
            [jnp.where(first_of_pair,
                       res[0:GMLP_BLOCK, blk * LANES:(blk + 1) * LANES],
                       res[GMLP_BLOCK:2 * GMLP_BLOCK, blk * LANES:(blk + 1) * LANES])
             for blk in range(n_blk)], axis=0))
        if next_chunk < D_FF // FF_CHUNK:
            ffn_chunk(next_chunk)
            next_chunk += 1
    while next_chunk < D_FF // FF_CHUNK:
        ffn_chunk(next_chunk)
        next_chunk += 1
    mixed = (jnp.concatenate(pair_cols, axis=1)
             + jnp.concatenate([bsp_ref[...]] * n_blk, axis=0))
    g_n = _rms(u_act * mixed, gg_ref[...]).astype(bf16)

    ffn_out = y_prev + jnp.dot(act_sc[...], w2_ref[...], preferred_element_type=f32)
    mix = jnp.concatenate([a_n, g_n], axis=-1)
    mix_out = x + jnp.dot(mix, w_out_ref[...], preferred_element_type=f32)
    return mix_out, ffn_out


def _load_weight(src, dst, stage, sem, src_col=0, dst_col=0):
    rows = src.shape[0]
    _, chunk, cols = stage.shape
    assert rows % chunk == 0 and chunk == STAGE_ROWS[cols] and stage.shape[0] == STAGE_SLOTS
    n = rows // chunk
    ahead = STAGE_SLOTS - 1
    assert n >= ahead

    def copy(i, slot):
        return pltpu.make_async_copy(src.at[pl.ds(i * chunk, chunk), pl.ds(src_col, cols)],
                                     stage.at[slot], sem.at[slot])

    for i in range(ahead):
        copy(i, i).start()

    def body(i, carry):
        slot = lax.rem(i, STAGE_SLOTS)

        @pl.when(i + ahead < n)
        def _():
            copy(i + ahead, lax.rem(i + ahead, STAGE_SLOTS)).start()

        copy(i, slot).wait()
        dst[pl.ds(pl.multiple_of(i * chunk, chunk), chunk), dst_col:dst_col + cols] = (
            stage[slot].astype(dst.dtype))
        return carry

    lax.fori_loop(0, n, body, 0)


def _layer_kernel(layer, tiles_per_row, n_tiles,
                  x_ref, vecs_ref, w_qv_ref, w_in_hbm, rtab_ref, wsp_ref, bsp_ref,
                  w_out_hbm, w1_hbm, w2_hbm,
                  o_ref, k_sc, v_sc, km_sc, bias_sc, y_sc, act_sc, qg_sc, ag_sc,
                  w_kug_sc, w_out_sc, w1_sc, w2_sc, st_k, st_d, st_ff, sem):
    g = pl.program_id(0)
    mixg_ref = vecs_ref.at[0:1, :]
    ffng_ref = vecs_ref.at[1:2, :]
    kg_ref = vecs_ref.at[2:3, 0:ATT_WIDTH]
    sgug_ref = vecs_ref.at[2:3, ATT_WIDTH:ATT_WIDTH + GMLP_WIDTH]
    gg_ref = vecs_ref.at[3:4, 0:GMLP_WIDTH]
    t = lax.rem(jnp.minimum(g, n_tiles - 1), tiles_per_row)
    row0 = pl.multiple_of(t * TQ, TQ)

    @pl.when(g == 0)
    def _():
        _load_weight(w_in_hbm.at[layer], w_kug_sc, st_k, sem, src_col=ATT_WIDTH, dst_col=0)
        _load_weight(w_in_hbm.at[layer], w_kug_sc, st_d, sem,
                     src_col=3 * ATT_WIDTH, dst_col=ATT_WIDTH)
        _load_weight(w_out_hbm.at[layer], w_out_sc, st_d, sem)
        _load_weight(w1_hbm.at[layer], w1_sc, st_ff, sem)
        _load_weight(w2_hbm.at[layer], w2_sc, st_d, sem)
        _build_bias(rtab_ref, bias_sc)
        qg_sc[...] = jnp.broadcast_to(vecs_ref[4:5, 0:ATT_WIDTH], (TQ, ATT_WIDTH)).T
        ag_sc[...] = jnp.broadcast_to(vecs_ref[3:4, GMLP_WIDTH:GMLP_WIDTH + ATT_WIDTH],
                                      (TQ, ATT_WIDTH)).T
        y_sc[...] = jnp.zeros(y_sc.shape, y_sc.dtype)
        is_pad = ((lax.broadcasted_iota(jnp.int32, km_sc.shape, 0) < LEFT)
                  & (lax.broadcasted_iota(jnp.int32, km_sc.shape, 1) == 0))
        km_sc[...] = jnp.where(is_pad, NEG_INF, 0.0).astype(km_sc.dtype)

    @pl.when(t == 0)
    def _():
        k_sc[0:LEFT, :] = jnp.zeros((LEFT, ATT_WIDTH), k_sc.dtype)
        v_sc[0:PAD_TILES] = jnp.zeros((PAD_TILES, ATT_WIDTH, TQ), v_sc.dtype)

    mix_out, ffn_out = _step(
        x_ref[...], y_sc[...], t, row0, mixg_ref, w_qv_ref, w_kug_sc, qg_sc, kg_ref, sgug_ref,
        wsp_ref, bsp_ref, ag_sc, gg_ref, w_out_sc, ffng_ref, w1_sc, w2_sc,
        k_sc, v_sc, km_sc, bias_sc, act_sc)
    o_ref[...] = ffn_out
    y_sc[...] = mix_out


def _layer_spec(layer, shape):
    zeros = (0,) * len(shape)
    return pl.BlockSpec((None,) + shape, lambda g: (layer,) + zeros,
                        pipeline_mode=pl.Buffered(1))


def _layer(layer, x, vecs, w_qv, w_in, rtab, wsp, bsp, w_out, w1, w2):
    B, S, D = x.shape
    tiles_per_row = S // TQ
    n_tiles = B * tiles_per_row
    bf16 = jnp.bfloat16
    f32 = jnp.float32

    def in_tile(g):
        gi = jnp.minimum(g, n_tiles - 1)
        return (gi // tiles_per_row, gi % tiles_per_row, 0)

    def out_tile(g):
        go = jnp.maximum(g - 1, 0)
        return (go // tiles_per_row, go % tiles_per_row, 0)

    hbm = pl.BlockSpec(memory_space=pl.ANY)
    return pl.pallas_call(
        functools.partial(_layer_kernel, layer, tiles_per_row, n_tiles),
        grid=(n_tiles + 1,),
        in_specs=[
            pl.BlockSpec((None, TQ, D), in_tile),
            _layer_spec(layer, (GAIN_ROWS, D)),
            _layer_spec(layer, (2 * ATT_WIDTH, D)),
            hbm,
            _layer_spec(layer, (ATT_HEADS, MXU_COLS)),
            _layer_spec(layer, (GMLP_GROUPS, GMLP_BLOCK, GMLP_BLOCK)),
            _layer_spec(layer, (GMLP_BLOCK, GMLP_WIDTH)),
            hbm,
            hbm,
            hbm,
        ],
        out_specs=pl.BlockSpec((None, TQ, D), out_tile),
        out_shape=jax.ShapeDtypeStruct(x.shape, x.dtype),
        scratch_shapes=[
            pltpu.VMEM((LEFT + S, ATT_WIDTH), bf16),
            pltpu.VMEM((PAD_TILES + tiles_per_row, ATT_WIDTH, TQ), bf16),
            pltpu.VMEM((LEFT + S, LANES), bf16),
            pltpu.VMEM((ATT_HEADS, TK, TQ), f32),
            pltpu.VMEM((TQ, D), f32),
            pltpu.VMEM((TQ, D_FF), bf16),
            pltpu.VMEM((ATT_WIDTH, TQ), f32),
            pltpu.VMEM((ATT_WIDTH, TQ), f32),
            pltpu.VMEM((D, KUG_WIDTH), bf16),
            pltpu.VMEM((D, D), bf16),
            pltpu.VMEM((D, 2 * D_FF), bf16),
            pltpu.VMEM((D_FF, D), bf16),
            pltpu.VMEM((STAGE_SLOTS, STAGE_ROWS[ATT_WIDTH], ATT_WIDTH), f32),
            pltpu.VMEM((STAGE_SLOTS, STAGE_ROWS[D], D), f32),
            pltpu.VMEM((STAGE_SLOTS, STAGE_ROWS[2 * D_FF], 2 * D_FF), f32),
            pltpu.SemaphoreType.DMA((STAGE_SLOTS,)),
        ],
        compiler_params=pltpu.CompilerParams(
            dimension_semantics=("arbitrary",),
            vmem_limit_bytes=VMEM_LIMIT_BYTES),
        name="layer",
    )(x, vecs, w_qv, w_in, rtab, wsp, bsp, w_out, w1, w2)


def _reversed_rel_table(rel_bias):
    near = rel_bias[:, :, ::-1][:, :, :NEAR]
    far = jnp.broadcast_to(rel_bias[:, :, 2 * MAX_REL:], near.shape[:2] + (MXU_COLS - NEAR,))
    return jnp.concatenate([near, far], axis=2).astype(jnp.float32)


def _pack_gains(mix_norm_g, ffn_norm_g, k_norm_g, sgu_norm_g, gmlp_out_norm_g, att_out_norm_g,
                q_norm_g):
    depth = mix_norm_g.shape[0]
    tiled = lambda g: jnp.tile(g, (1, ATT_HEADS))
    rows = [mix_norm_g, ffn_norm_g,
            jnp.concatenate([tiled(k_norm_g), sgu_norm_g], axis=1),
            jnp.concatenate([gmlp_out_norm_g, att_out_norm_g], axis=1),
            jnp.concatenate([tiled(q_norm_g), jnp.zeros((depth, D_MODEL - ATT_WIDTH))], axis=1)]
    rows += [jnp.zeros((depth, D_MODEL))] * (GAIN_ROWS - len(rows))
    return jnp.stack(rows, axis=1).astype(jnp.float32)


def kernel(x, mix_norm_g, w_in, q_norm_g, k_norm_g, rel_bias, sgu_norm_g, w_spatial, b_spatial,
           att_out_norm_g, gmlp_out_norm_g, w_out, ffn_norm_g, w_ffn_in, w_ffn_out):
    depth = w_in.shape[0]
    w_qv = jnp.concatenate([w_in[:, :, 0:ATT_WIDTH], w_in[:, :, 2 * ATT_WIDTH:3 * ATT_WIDTH]],
                           axis=2)
    w_qv_t = jnp.swapaxes(w_qv, 1, 2).astype(jnp.bfloat16)
    params = (
        _pack_gains(mix_norm_g, ffn_norm_g, k_norm_g, sgu_norm_g, gmlp_out_norm_g,
                    att_out_norm_g, q_norm_g),
        w_qv_t, w_in, _reversed_rel_table(rel_bias), w_spatial,
        jnp.repeat(jnp.swapaxes(b_spatial, 1, 2), GMLP_GROUP_DIM, axis=2),
        w_out, w_ffn_in, w_ffn_out)
    for l in range(depth):
        x = _layer(l, x, *params)
    return x
```

```python
import functools

import jax
import jax.numpy as jnp
from jax import lax
from jax.experimental import pallas as pl
from jax.experimental.pallas import tpu as pltpu

D_MODEL = 1024
CHUNK = 64
ATT_HEADS = 8
HEAD_DIM = 64
ATT_WIDTH = ATT_HEADS * HEAD_DIM
LEFT_CHUNKS = 8
LEFT = LEFT_CHUNKS * CHUNK
MAX_REL = 2 * CHUNK
NEAR = MAX_REL + CHUNK
GMLP_WIDTH = 512
GMLP_GROUPS = 8
GMLP_GROUP_DIM = GMLP_WIDTH // GMLP_GROUPS
GMLP_BLOCK = 128
D_FF = 2816
KUG_WIDTH = ATT_WIDTH + 2 * GMLP_WIDTH
EPS = 1e-6
NEG_INF = -1e30

LANES = 128
BF16_ROWS = 16
MXU_COLS = 256
TQ = 256
TK = LEFT + TQ
PAD_TILES = LEFT // TQ
WIN_TILES = TK // TQ
FF_CHUNK = 256
HEADS_PER_ROUND = 2
STAGE_ROWS = {ATT_WIDTH: 128, D_MODEL: 128, 2 * D_FF: 32}
GAIN_ROWS = 8
STAGE_SLOTS = 4
VMEM_LIMIT_BYTES = 60 * 1024 * 1024


def _rms(x, g):
    ms = jnp.mean(x * x, axis=-1, keepdims=True)
    return (x * lax.rsqrt(ms + EPS)) * g


def _head_rms(z, gain):
    rows = z.shape[0]
    lo = lax.broadcasted_iota(jnp.int32, (rows, LANES), 1) < HEAD_DIM
    outs = []
    for c in range(ATT_WIDTH // LANES):
        zc = z[:, c * LANES:(c + 1) * LANES]
        sq = zc * zc
        ms_lo = jnp.sum(jnp.where(lo, sq, 0.0), axis=-1, keepdims=True) * (1.0 / HEAD_DIM)
        ms_hi = jnp.sum(jnp.where(lo, 0.0, sq), axis=-1, keepdims=True) * (1.0 / HEAD_DIM)
        r = jnp.where(lo, lax.rsqrt(ms_lo + EPS), lax.rsqrt(ms_hi + EPS))
        outs.append((zc * r) * gain[:, c * LANES:(c + 1) * LANES])
    return jnp.concatenate(outs, axis=-1)


def _build_bias(rtab_ref, bias_sc):
    f32 = jnp.float32
    row = lax.broadcasted_iota(jnp.int32, (CHUNK, MXU_COLS), 0)
    lo_half = lax.broadcasted_iota(jnp.int32, (CHUNK, LANES), 1) < CHUNK
    n_kc = TK // CHUNK
    for head in range(ATT_HEADS):
        r = rtab_ref[head:head + 1, :]
        far = rtab_ref[head:head + 1, MXU_COLS - 1:MXU_COLS]
        x0 = jnp.broadcast_to(r, (CHUNK, MXU_COLS))
        for b in range(CHUNK.bit_length() - 1):
            x0 = jnp.where(((row >> b) & 1) == 1, pltpu.roll(x0, 1 << b, axis=1), x0)
        x1 = pltpu.roll(x0, CHUNK, axis=1)
        far_piece = jnp.broadcast_to(far, (CHUNK, LANES))
        neg_piece = jnp.full((CHUNK, LANES), NEG_INF, f32)

        def half(qc, kc):
            rel = kc - qc
            if rel < 0 or rel > LEFT_CHUNKS:
                return neg_piece
            if rel < LEFT_CHUNKS - 2:
                return far_piece
            col = (rel - (LEFT_CHUNKS - 2)) * CHUNK + (qc % 2) * CHUNK
            src = x1 if qc % 2 else x0
            return src[:, (col // LANES) * LANES:(col // LANES + 1) * LANES]

        row_blocks = []
        for qc in range(TQ // CHUNK):
            pieces = [jnp.where(lo_half, half(qc, 2 * vcol), half(qc, 2 * vcol + 1))
                      for vcol in range(n_kc // 2)]
            row_blocks.append(jnp.concatenate(pieces, axis=1))
        bias_sc[head] = jnp.concatenate(row_blocks, axis=0).T


def _step(x, y_prev, t, row0, mixg_ref, w_qv_ref, w_kug_ref, qg_ref, kg_ref, sgug_ref, wsp_ref,
          bsp_ref, ag_ref, gg_ref, w_out_ref, ffng_ref, w1_ref, w2_ref,
          k_sc, v_sc, km_sc, bias_sc, act_sc):
    bf16 = jnp.bfloat16
    f32 = jnp.float32
    nt = (((1,), (1,)), ((), ()))
    h = _rms(x, mixg_ref[...]).astype(bf16)
    hf = _rms(y_prev, ffng_ref[...]).astype(bf16)

    def ffn_chunk(j):
        gate = jnp.dot(hf, w1_ref[:, j * FF_CHUNK:(j + 1) * FF_CHUNK], preferred_element_type=f32)
        up = jnp.dot(hf, w1_ref[:, D_FF + j * FF_CHUNK:D_FF + (j + 1) * FF_CHUNK],
                     preferred_element_type=f32)
        act_sc[:, j * FF_CHUNK:(j + 1) * FF_CHUNK] = (jax.nn.silu(gate) * up).astype(bf16)

    qv_t = lax.dot_general(w_qv_ref[...], h, nt, preferred_element_type=f32)
    k = jnp.dot(h, w_kug_ref[:, 0:ATT_WIDTH], preferred_element_type=f32)
    u = jnp.dot(h, w_kug_ref[:, ATT_WIDTH:ATT_WIDTH + GMLP_WIDTH], preferred_element_type=f32)
    vg = jnp.dot(h, w_kug_ref[:, ATT_WIDTH + GMLP_WIDTH:KUG_WIDTH], preferred_element_type=f32)

    q3 = qv_t[0:ATT_WIDTH].reshape(ATT_HEADS, HEAD_DIM, TQ)
    q_ms = jnp.mean(q3 * q3, axis=1, keepdims=True)
    q3 = (q3 * lax.rsqrt(q_ms + EPS)) * qg_ref[...].reshape(ATT_HEADS, HEAD_DIM, TQ)
    qn_t = (q3 * HEAD_DIM ** -0.5).reshape(ATT_WIDTH, TQ).astype(bf16)
    kn = _head_rms(k, kg_ref[...]).astype(bf16)
    k_sc[pl.ds(row0 + LEFT, TQ), :] = kn
    v_sc[t + PAD_TILES] = qv_t[ATT_WIDTH:2 * ATT_WIDTH].astype(bf16)
    kw = k_sc[pl.ds(row0, TK), :]
    km = km_sc[pl.ds(row0, TK), :]
    pad_rows = jnp.where(lax.broadcasted_iota(jnp.int32, (LANES, TQ), 0) == 0,
                         1.0, 0.0).astype(bf16)
    zero_rows = jnp.zeros((HEAD_DIM, TQ), bf16)
    ones_rows = jnp.ones((BF16_ROWS, TK), bf16)

    def scores(head):
        c = head // 2
        qh = qn_t[head * HEAD_DIM:(head + 1) * HEAD_DIM]
        pair = [qh, zero_rows] if head % 2 == 0 else [zero_rows, qh]
        rhs = jnp.concatenate(pair + [pad_rows], axis=0)
        lhs = jnp.concatenate([kw[:, c * LANES:(c + 1) * LANES], km], axis=1)
        s = jnp.dot(lhs, rhs, preferred_element_type=f32)
        return s + bias_sc[head]

    def probs(s):
        m = jnp.max(s, axis=0, keepdims=True)
        return jnp.exp(s - m).astype(bf16)

    def weighted_values(head, p):
        v_t = jnp.concatenate(
            [v_sc[t + w, head * HEAD_DIM:(head + 1) * HEAD_DIM, :] for w in range(WIN_TILES)],
            axis=1)
        pv = jnp.dot(jnp.concatenate([v_t, ones_rows], axis=0), p,
                     preferred_element_type=f32)
        return pv[0:HEAD_DIM] * (1.0 / pv[HEAD_DIM:HEAD_DIM + 1])

    rounds = [list(range(r, r + HEADS_PER_ROUND)) for r in range(0, ATT_HEADS, HEADS_PER_ROUND)]
    outs = []
    next_chunk = 0
    s_next = [scores(hd) for hd in rounds[0]]
    for ri, heads in enumerate(rounds):
        s_cur = s_next
        if ri + 1 < len(rounds):
            s_next = [scores(hd) for hd in rounds[ri + 1]]
        for _ in heads:
            ffn_chunk(next_chunk)
            next_chunk += 1
        ps = [probs(s) for s in s_cur]
        outs += [weighted_values(hd, p) for hd, p in zip(heads, ps)]
    a_t = jnp.concatenate(outs, axis=0)
    a_ms = jnp.mean(a_t * a_t, axis=0, keepdims=True)
    a_n = ((a_t * lax.rsqrt(a_ms + EPS)) * ag_ref[...]).T.astype(bf16)

    u_act = jax.nn.gelu(u)
    vgn = _rms(jax.nn.gelu(vg), sgug_ref[...]).astype(bf16)
    ti = lax.broadcasted_iota(jnp.int32, (GMLP_GROUPS, GMLP_BLOCK, GMLP_BLOCK), 1)
    si = lax.broadcasted_iota(jnp.int32, (GMLP_GROUPS, GMLP_BLOCK, GMLP_BLOCK), 2)
    causal = (ti // CHUNK) >= (si // CHUNK)
    wsp = jnp.where(causal, wsp_ref[...], 0.0).astype(bf16)
    wsp = wsp.reshape(GMLP_GROUPS * GMLP_BLOCK, GMLP_BLOCK)
    first_of_pair = lax.broadcasted_iota(jnp.int32, (GMLP_BLOCK, LANES), 1) < GMLP_GROUP_DIM
    n_blk = TQ // GMLP_BLOCK
    pair_cols = []
    for pr in range(GMLP_WIDTH // LANES):
        lhs = wsp[2 * pr * GMLP_BLOCK:2 * (pr + 1) * GMLP_BLOCK, :]
        rhs = jnp.concatenate(
            [vgn[blk * GMLP_BLOCK:(blk + 1) * GMLP_BLOCK, pr * LANES:(pr + 1) * LANES]
             for blk in range(n_blk)], axis=1)
        res = jnp.dot(lhs, rhs, preferred_element_type=f32)
        pair_cols.append(jnp.concatenate(
            [jnp.where(first_of_pair,
                       res[0:GMLP_BLOCK, blk * LANES:(blk + 1) * LANES],
                       res[GMLP_BLOCK:2 * GMLP_BLOCK, blk * LANES:(blk + 1) * LANES])
             for blk in range(n_blk)], axis=0))
        if next_chunk < D_FF // FF_CHUNK:
            ffn_chunk(next_chunk)
            next_chunk += 1
    while next_chunk < D_FF // FF_CHUNK:
        ffn_chunk(next_chunk)
        next_chunk += 1
    mixed = (jnp.concatenate(pair_cols, axis=1)
             + jnp.concatenate([bsp_ref[...]] * n_blk, axis=0))
    g_n = _rms(u_act * mixed, gg_ref[...]).astype(bf16)

    ffn_out = y_prev + jnp.dot(act_sc[...], w2_ref[...], preferred_element_type=f32)
    mix = jnp.concatenate([a_n, g_n], axis=-1)
    mix_out = x + jnp.dot(mix, w_out_ref[...], preferred_element_type=f32)
    return mix_out, ffn_out


def _load_weight(src, dst, stage, sem, src_col=0, dst_col=0):
    rows = src.shape[0]
    _, chunk, cols = stage.shape
    assert rows % chunk == 0 and chunk == STAGE_ROWS[cols] and stage.shape[0] == STAGE_SLOTS
    n = rows // chunk
    ahead = STAGE_SLOTS - 1
    assert n >= ahead

    def copy(i, slot):
        return pltpu.make_async_copy(src.at[pl.ds(i * chunk, chunk), pl.ds(src_col, cols)],
                                     stage.at[slot], sem.at[slot])

    for i in range(ahead):
        copy(i, i).start()

    def body(i, carry):
        slot = lax.rem(i, STAGE_SLOTS)

        @pl.when(i + ahead < n)
        def _():
            copy(i + ahead, lax.rem(i + ahead, STAGE_SLOTS)).start()

        copy(i, slot).wait()
        dst[pl.ds(pl.multiple_of(i * chunk, chunk), chunk), dst_col:dst_col + cols] = (
            stage[slot].astype(dst.dtype))
        return carry

    lax.fori_loop(0, n, body, 0)


def _load_weight_transposed(src, dst, stage, sem, src_col, dst_row):
    rows = src.shape[0]
    _, chunk, cols = stage.shape
    assert rows % chunk == 0 and chunk % LANES == 0 and stage.shape[0] == STAGE_SLOTS
    n = rows // chunk
    ahead = STAGE_SLOTS - 1
    assert n >= ahead

    def copy(i):
        slot = i % STAGE_SLOTS
        return pltpu.make_async_copy(src.at[pl.ds(i * chunk, chunk), pl.ds(src_col, cols)],
                                     stage.at[slot], sem.at[slot])

    for i in range(ahead):
        copy(i).start()
    for i in range(n):
        if i + ahead < n:
            copy(i + ahead).start()
        copy(i).wait()
        dst[dst_row:dst_row + cols, i * chunk:(i + 1) * chunk] = (
            stage[i % STAGE_SLOTS].T.astype(dst.dtype))


def _layer_kernel(layer, tiles_per_row, n_tiles,
                  x_ref, vecs_ref, w_in_hbm, rtab_ref, wsp_ref, bsp_ref,
                  w_out_hbm, w1_hbm, w2_hbm,
                  o_ref, k_sc, v_sc, km_sc, bias_sc, y_sc, act_sc, qg_sc, ag_sc,
                  w_qv_sc, w_kug_sc, w_out_sc, w1_sc, w2_sc, st_k, st_d, st_ff, sem):
    g = pl.program_id(0)
    mixg_ref = vecs_ref.at[0:1, :]
    ffng_ref = vecs_ref.at[1:2, :]
    kg_ref = vecs_ref.at[2:3, 0:ATT_WIDTH]
    sgug_ref = vecs_ref.at[2:3, ATT_WIDTH:ATT_WIDTH + GMLP_WIDTH]
    gg_ref = vecs_ref.at[3:4, 0:GMLP_WIDTH]
    t = lax.rem(jnp.minimum(g, n_tiles - 1), tiles_per_row)
    row0 = pl.multiple_of(t * TQ, TQ)

    @pl.when(g == 0)
    def _():
        _load_weight_transposed(w_in_hbm.at[layer], w_qv_sc, st_k, sem, src_col=0, dst_row=0)
        _load_weight_transposed(w_in_hbm.at[layer], w_qv_sc, st_k, sem,
                                src_col=2 * ATT_WIDTH, dst_row=ATT_WIDTH)
        _load_weight(w_in_hbm.at[layer], w_kug_sc, st_k, sem, src_col=ATT_WIDTH, dst_col=0)
        _load_weight(w_in_hbm.at[layer], w_kug_sc, st_d, sem,
                     src_col=3 * ATT_WIDTH, dst_col=ATT_WIDTH)
        _load_weight(w_out_hbm.at[layer], w_out_sc, st_d, sem)
        _load_weight(w1_hbm.at[layer], w1_sc, st_ff, sem)
        _load_weight(w2_hbm.at[layer], w2_sc, st_d, sem)
        _build_bias(rtab_ref, bias_sc)
        qg_sc[...] = jnp.broadcast_to(vecs_ref[4:5, 0:ATT_WIDTH], (TQ, ATT_WIDTH)).T
        ag_sc[...] = jnp.broadcast_to(vecs_ref[3:4, GMLP_WIDTH:GMLP_WIDTH + ATT_WIDTH],
                                      (TQ, ATT_WIDTH)).T
        y_sc[...] = jnp.zeros(y_sc.shape, y_sc.dtype)
        is_pad = ((lax.broadcasted_iota(jnp.int32, km_sc.shape, 0) < LEFT)
                  & (lax.broadcasted_iota(jnp.int32, km_sc.shape, 1) == 0))
        km_sc[...] = jnp.where(is_pad, NEG_INF, 0.0).astype(km_sc.dtype)

    @pl.when(t == 0)
    def _():
        k_sc[0:LEFT, :] = jnp.zeros((LEFT, ATT_WIDTH), k_sc.dtype)
        v_sc[0:PAD_TILES] = jnp.zeros((PAD_TILES, ATT_WIDTH, TQ), v_sc.dtype)

    mix_out, ffn_out = _step(
        x_ref[...], y_sc[...], t, row0, mixg_ref, w_qv_sc, w_kug_sc, qg_sc, kg_ref, sgug_ref,
        wsp_ref, bsp_ref, ag_sc, gg_ref, w_out_sc, ffng_ref, w1_sc, w2_sc,
        k_sc, v_sc, km_sc, bias_sc, act_sc)
    o_ref[...] = ffn_out
    y_sc[...] = mix_out


def _layer_spec(layer, shape):
    zeros = (0,) * len(shape)
    return pl.BlockSpec((None,) + shape, lambda g: (layer,) + zeros,
                        pipeline_mode=pl.Buffered(1))


def _layer(layer, x, vecs, w_in, rtab, wsp, bsp, w_out, w1, w2):
    B, S, D = x.shape
    tiles_per_row = S // TQ
    n_tiles = B * tiles_per_row
    bf16 = jnp.bfloat16
    f32 = jnp.float32

    def in_tile(g):
        gi = jnp.minimum(g, n_tiles - 1)
        return (gi // tiles_per_row, gi % tiles_per_row, 0)

    def out_tile(g):
        go = jnp.maximum(g - 1, 0)
        return (go // tiles_per_row, go % tiles_per_row, 0)

    hbm = pl.BlockSpec(memory_space=pl.ANY)
    return pl.pallas_call(
        functools.partial(_layer_kernel, layer, tiles_per_row, n_tiles),
        grid=(n_tiles + 1,),
        in_specs=[
            pl.BlockSpec((None, TQ, D), in_tile),
            _layer_spec(layer, (GAIN_ROWS, D)),
            hbm,
            _layer_spec(layer, (ATT_HEADS, MXU_COLS)),
            _layer_spec(layer, (GMLP_GROUPS, GMLP_BLOCK, GMLP_BLOCK)),
            _layer_spec(layer, (GMLP_BLOCK, GMLP_WIDTH)),
            hbm,
            hbm,
            hbm,
        ],
        out_specs=pl.BlockSpec((None, TQ, D), out_tile),
        out_shape=jax.ShapeDtypeStruct(x.shape, x.dtype),
        scratch_shapes=[
            pltpu.VMEM((LEFT + S, ATT_WIDTH), bf16),
            pltpu.VMEM((PAD_TILES + tiles_per_row, ATT_WIDTH, TQ), bf16),
            pltpu.VMEM((LEFT + S, LANES), bf16),
            pltpu.VMEM((ATT_HEADS, TK, TQ), f32),
            pltpu.VMEM((TQ, D), f32),
            pltpu.VMEM((TQ, D_FF), bf16),
            pltpu.VMEM((ATT_WIDTH, TQ), f32),
            pltpu.VMEM((ATT_WIDTH, TQ), f32),
            pltpu.VMEM((2 * ATT_WIDTH, D), bf16),
            pltpu.VMEM((D, KUG_WIDTH), bf16),
            pltpu.VMEM((D, D), bf16),
            pltpu.VMEM((D, 2 * D_FF), bf16),
            pltpu.VMEM((D_FF, D), bf16),
            pltpu.VMEM((STAGE_SLOTS, STAGE_ROWS[ATT_WIDTH], ATT_WIDTH), f32),
            pltpu.VMEM((STAGE_SLOTS, STAGE_ROWS[D], D), f32),
            pltpu.VMEM((STAGE_SLOTS, STAGE_ROWS[2 * D_FF], 2 * D_FF), f32),
            pltpu.SemaphoreType.DMA((STAGE_SLOTS,)),
        ],
        compiler_params=pltpu.CompilerParams(
            dimension_semantics=("arbitrary",),
            vmem_limit_bytes=VMEM_LIMIT_BYTES),
        name="layer",
    )(x, vecs, w_in, rtab, wsp, bsp, w_out, w1, w2)


def _reversed_rel_table(rel_bias):
    near = rel_bias[:, :, ::-1][:, :, :NEAR]
    far = jnp.broadcast_to(rel_bias[:, :, 2 * MAX_REL:], near.shape[:2] + (MXU_COLS - NEAR,))
    return jnp.concatenate([near, far], axis=2).astype(jnp.float32)


def _pack_gains(mix_norm_g, ffn_norm_g, k_norm_g, sgu_norm_g, gmlp_out_norm_g, att_out_norm_g,
                q_norm_g):
    depth = mix_norm_g.shape[0]
    tiled = lambda g: jnp.tile(g, (1, ATT_HEADS))
    rows = [mix_norm_g, ffn_norm_g,
            jnp.concatenate([tiled(k_norm_g), sgu_norm_g], axis=1),
            jnp.concatenate([gmlp_out_norm_g, att_out_norm_g], axis=1),
            jnp.concatenate([tiled(q_norm_g), jnp.zeros((depth, D_MODEL - ATT_WIDTH))], axis=1)]
    rows += [jnp.zeros((depth, D_MODEL))] * (GAIN_ROWS - len(rows))
    return jnp.stack(rows, axis=1).astype(jnp.float32)


def kernel(x, mix_norm_g, w_in, q_norm_g, k_norm_g, rel_bias, sgu_norm_g, w_spatial, b_spatial,
           att_out_norm_g, gmlp_out_norm_g, w_out, ffn_norm_g, w_ffn_in, w_ffn_out):
    depth = w_in.shape[0]
    params = (
        _pack_gains(mix_norm_g, ffn_norm_g, k_norm_g, sgu_norm_g, gmlp_out_norm_g,
                    att_out_norm_g, q_norm_g),
        w_in, _reversed_rel_table(rel_bias), w_spatial,
        jnp.repeat(jnp.swapaxes(b_spatial, 1, 2), GMLP_GROUP_DIM, axis=2),
        w_out, w_ffn_in, w_ffn_out)
    for l in range(depth):
        x = _layer(l, x, *params)
    return x
```

```python
import functools

import jax
import jax.numpy as jnp
from jax import lax
from jax.experimental import pallas as pl
from jax.experimental.pallas import tpu as pltpu

D_MODEL = 1024
CHUNK = 64
ATT_HEADS = 8
HEAD_DIM = 64
ATT_WIDTH = ATT_HEADS * HEAD_DIM
LEFT_CHUNKS = 8
LEFT = LEFT_CHUNKS * CHUNK
MAX_REL = 2 * CHUNK
NEAR = MAX_REL + CHUNK
GMLP_WIDTH = 512
GMLP_GROUPS = 8
GMLP_GROUP_DIM = GMLP_WIDTH // GMLP_GROUPS
GMLP_BLOCK = 128
D_FF = 2816
KUG_WIDTH = ATT_WIDTH + 2 * GMLP_WIDTH
EPS = 1e-6
NEG_INF = -1e30

LANES = 128
BF16_ROWS = 16
MXU_COLS = 256
TQ = 256
TK = LEFT + TQ
PAD_TILES = LEFT // TQ
WIN_TILES = TK // TQ
FF_CHUNK = 256
HEADS_PER_ROUND = 2
STAGE_ROWS = {ATT_WIDTH: 128, D_MODEL: 128, 2 * D_FF: 32}
GAIN_ROWS = 8
STAGE_SLOTS = 4
VMEM_LIMIT_BYTES = 60 * 1024 * 1024


def _rms(x, g):
    ms = jnp.mean(x * x, axis=-1, keepdims=True)
    return (x * lax.rsqrt(ms + EPS)) * g


def _head_rms(z, gain):
    rows = z.shape[0]
    lo = lax.broadcasted_iota(jnp.int32, (rows, LANES), 1) < HEAD_DIM
    outs = []
    for c in range(ATT_WIDTH // LANES):
        zc = z[:, c * LANES:(c + 1) * LANES]
        sq = zc * zc
        ms_lo = jnp.sum(jnp.where(lo, sq, 0.0), axis=-1, keepdims=True) * (1.0 / HEAD_DIM)
        ms_hi = jnp.sum(jnp.where(lo, 0.0, sq), axis=-1, keepdims=True) * (1.0 / HEAD_DIM)
        r = jnp.where(lo, lax.rsqrt(ms_lo + EPS), lax.rsqrt(ms_hi + EPS))
        outs.append((zc * r) * gain[:, c * LANES:(c + 1) * LANES])
    return jnp.concatenate(outs, axis=-1)


def _build_bias(rtab_ref, bias_sc):
    f32 = jnp.float32
    row = lax.broadcasted_iota(jnp.int32, (CHUNK, MXU_COLS), 0)
    lo_half = lax.broadcasted_iota(jnp.int32, (CHUNK, LANES), 1) < CHUNK
    n_kc = TK // CHUNK
    for head in range(ATT_HEADS):
        r = rtab_ref[head:head + 1, :]
        far = rtab_ref[head:head + 1, MXU_COLS - 1:MXU_COLS]
        x0 = jnp.broadcast_to(r, (CHUNK, MXU_COLS))
        for b in range(CHUNK.bit_length() - 1):
            x0 = jnp.where(((row >> b) & 1) == 1, pltpu.roll(x0, 1 << b, axis=1), x0)
        x1 = pltpu.roll(x0, CHUNK, axis=1)
        far_piece = jnp.broadcast_to(far, (CHUNK, LANES))
        neg_piece = jnp.full((CHUNK, LANES), NEG_INF, f32)

        def half(qc, kc):
            rel = kc - qc
            if rel < 0 or rel > LEFT_CHUNKS:
                return neg_piece
            if rel < LEFT_CHUNKS - 2:
                return far_piece
            col = (rel - (LEFT_CHUNKS - 2)) * CHUNK + (qc % 2) * CHUNK
            src = x1 if qc % 2 else x0
            return src[:, (col // LANES) * LANES:(col // LANES + 1) * LANES]

        row_blocks = []
        for qc in range(TQ // CHUNK):
            pieces = [jnp.where(lo_half, half(qc, 2 * vcol), half(qc, 2 * vcol + 1))
                      for vcol in range(n_kc // 2)]
            row_blocks.append(jnp.concatenate(pieces, axis=1))
        bias_sc[head] = jnp.concatenate(row_blocks, axis=0).T


def _step(x, y_prev, t, row0, mixg_ref, w_qv_ref, w_kug_ref, qg_ref, kg_ref, sgug_ref, wsp_ref,
          bsp_ref, ag_ref, gg_ref, w_out_ref, ffng_ref, w1_ref, w2_ref,
          k_sc, v_sc, km_sc, bias_sc, act_sc):
    bf16 = jnp.bfloat16
    f32 = jnp.float32
    nt = (((1,), (1,)), ((), ()))
    do_ffn = y_prev is not None
    h = _rms(x, mixg_ref[...]).astype(bf16)
    if do_ffn:
        hf = _rms(y_prev, ffng_ref[...]).astype(bf16)

    def ffn_chunk(j):
        if not do_ffn:
            return
        gate = jnp.dot(hf, w1_ref[:, j * FF_CHUNK:(j + 1) * FF_CHUNK], preferred_element_type=f32)
        up = jnp.dot(hf, w1_ref[:, D_FF + j * FF_CHUNK:D_FF + (j + 1) * FF_CHUNK],
                     preferred_element_type=f32)
        act_sc[:, j * FF_CHUNK:(j + 1) * FF_CHUNK] = (jax.nn.silu(gate) * up).astype(bf16)

    qv_t = lax.dot_general(w_qv_ref[...], h, nt, preferred_element_type=f32)
    k = jnp.dot(h, w_kug_ref[:, 0:ATT_WIDTH], preferred_element_type=f32)
    u = jnp.dot(h, w_kug_ref[:, ATT_WIDTH:ATT_WIDTH + GMLP_WIDTH], preferred_element_type=f32)
    vg = jnp.dot(h, w_kug_ref[:, ATT_WIDTH + GMLP_WIDTH:KUG_WIDTH], preferred_element_type=f32)

    q3 = qv_t[0:ATT_WIDTH].reshape(ATT_HEADS, HEAD_DIM, TQ)
    q_ms = jnp.mean(q3 * q3, axis=1, keepdims=True)
    q3 = (q3 * lax.rsqrt(q_ms + EPS)) * qg_ref[...].reshape(ATT_HEADS, HEAD_DIM, TQ)
    qn_t = (q3 * HEAD_DIM ** -0.5).reshape(ATT_WIDTH, TQ).astype(bf16)
    kn = _head_rms(k, kg_ref[...]).astype(bf16)
    k_sc[pl.ds(row0 + LEFT, TQ), :] = kn
    v_sc[t + PAD_TILES] = qv_t[ATT_WIDTH:2 * ATT_WIDTH].astype(bf16)
    kw = k_sc[pl.ds(row0, TK), :]
    km = km_sc[pl.ds(row0, TK), :]
    pad_rows = jnp.where(lax.broadcasted_iota(jnp.int32, (LANES, TQ), 0) == 0,
                         1.0, 0.0).astype(bf16)
    zero_rows = jnp.zeros((HEAD_DIM, TQ), bf16)
    ones_rows = jnp.ones((BF16_ROWS, TK), bf16)

    def scores(head):
        c = head // 2
        qh = qn_t[head * HEAD_DIM:(head + 1) * HEAD_DIM]
        pair = [qh, zero_rows] if head % 2 == 0 else [zero_rows, qh]
        rhs = jnp.concatenate(pair + [pad_rows], axis=0)
        lhs = jnp.concatenate([kw[:, c * LANES:(c + 1) * LANES], km], axis=1)
        s = jnp.dot(lhs, rhs, preferred_element_type=f32)
        return s + bias_sc[head]

    def probs(s):
        m = jnp.max(s, axis=0, keepdims=True)
        return jnp.exp(s - m).astype(bf16)

    def weighted_values(head, p):
        v_t = jnp.concatenate(
            [v_sc[t + w, head * HEAD_DIM:(head + 1) * HEAD_DIM, :] for w in range(WIN_TILES)],
            axis=1)
        pv = jnp.dot(jnp.concatenate([v_t, ones_rows], axis=0), p,
                     preferred_element_type=f32)
        return pv[0:HEAD_DIM] * (1.0 / pv[HEAD_DIM:HEAD_DIM + 1])

    rounds = [list(range(r, r + HEADS_PER_ROUND)) for r in range(0, ATT_HEADS, HEADS_PER_ROUND)]
    outs = []
    next_chunk = 0
    s_next = [scores(hd) for hd in rounds[0]]
    for ri, heads in enumerate(rounds):
        s_cur = s_next
        if ri + 1 < len(rounds):
            s_next = [scores(hd) for hd in rounds[ri + 1]]
        for _ in heads:
            ffn_chunk(next_chunk)
            next_chunk += 1
        ps = [probs(s) for s in s_cur]
        outs += [weighted_values(hd, p) for hd, p in zip(heads, ps)]
    a_t = jnp.concatenate(outs, axis=0)
    a_ms = jnp.mean(a_t * a_t, axis=0, keepdims=True)
    a_n = ((a_t * lax.rsqrt(a_ms + EPS)) * ag_ref[...]).T.astype(bf16)

    u_act = jax.nn.gelu(u)
    vgn = _rms(jax.nn.gelu(vg), sgug_ref[...]).astype(bf16)
    ti = lax.broadcasted_iota(jnp.int32, (GMLP_GROUPS, GMLP_BLOCK, GMLP_BLOCK), 1)
    si = lax.broadcasted_iota(jnp.int32, (GMLP_GROUPS, GMLP_BLOCK, GMLP_BLOCK), 2)
    causal = (ti // CHUNK) >= (si // CHUNK)
    wsp = jnp.where(causal, wsp_ref[...], 0.0).astype(bf16)
    wsp = wsp.reshape(GMLP_GROUPS * GMLP_BLOCK, GMLP_BLOCK)
    first_of_pair = lax.broadcasted_iota(jnp.int32, (GMLP_BLOCK, LANES), 1) < GMLP_GROUP_DIM
    n_blk = TQ // GMLP_BLOCK
    pair_cols = []
    for pr in range(GMLP_WIDTH // LANES):
        lhs = wsp[2 * pr * GMLP_BLOCK:2 * (pr + 1) * GMLP_BLOCK, :]
        rhs = jnp.concatenate(
            [vgn[blk * GMLP_BLOCK:(blk + 1) * GMLP_BLOCK, pr * LANES:(pr + 1) * LANES]
             for blk in range(n_blk)], axis=1)
        res = jnp.dot(lhs, rhs, preferred_element_type=f32)
        pair_cols.append(jnp.concatenate(
            [jnp.where(first_of_pair,
                       res[0:GMLP_BLOCK, blk * LANES:(blk + 1) * LANES],
                       res[GMLP_BLOCK:2 * GMLP_BLOCK, blk * LANES:(blk + 1) * LANES])
             for blk in range(n_blk)], axis=0))
        if next_chunk < D_FF // FF_CHUNK:
            ffn_chunk(next_chunk)
            next_chunk += 1
    while next_chunk < D_FF // FF_CHUNK:
        ffn_chunk(next_chunk)
        next_chunk += 1
    mixed = (jnp.concatenate(pair_cols, axis=1)
             + jnp.concatenate([bsp_ref[...]] * n_blk, axis=0))
    g_n = _rms(u_act * mixed, gg_ref[...]).astype(bf16)

    ffn_out = None
    if do_ffn:
        ffn_out = y_prev + jnp.dot(act_sc[...], w2_ref[...], preferred_element_type=f32)
    mix = jnp.concatenate([a_n, g_n], axis=-1)
    mix_out = x + jnp.dot(mix, w_out_ref[...], preferred_element_type=f32)
    return mix_out, ffn_out


def _load_weight(src, dst, stage, sem, src_col=0, dst_col=0):
    rows = src.shape[0]
    _, chunk, cols = stage.shape
    assert rows % chunk == 0 and chunk == STAGE_ROWS[cols] and stage.shape[0] == STAGE_SLOTS
    n = rows // chunk
    ahead = STAGE_SLOTS - 1
    assert n >= ahead

    def copy(i, slot):
        return pltpu.make_async_copy(src.at[pl.ds(i * chunk, chunk), pl.ds(src_col, cols)],
                                     stage.at[slot], sem.at[slot])

    for i in range(ahead):
        copy(i, i).start()

    def body(i, carry):
        slot = lax.rem(i, STAGE_SLOTS)

        @pl.when(i + ahead < n)
        def _():
            copy(i + ahead, lax.rem(i + ahead, STAGE_SLOTS)).start()

        copy(i, slot).wait()
        dst[pl.ds(pl.multiple_of(i * chunk, chunk), chunk), dst_col:dst_col + cols] = (
            stage[slot].astype(dst.dtype))
        return carry

    lax.fori_loop(0, n, body, 0)


def _load_weight_transposed(src, dst, stage, sem, src_col, dst_row):
    rows = src.shape[0]
    _, chunk, cols = stage.shape
    assert rows % chunk == 0 and chunk % LANES == 0 and stage.shape[0] == STAGE_SLOTS
    n = rows // chunk
    ahead = STAGE_SLOTS - 1
    assert n >= ahead

    def copy(i):
        slot = i % STAGE_SLOTS
        return pltpu.make_async_copy(src.at[pl.ds(i * chunk, chunk), pl.ds(src_col, cols)],
                                     stage.at[slot], sem.at[slot])

    for i in range(ahead):
        copy(i).start()
    for i in range(n):
        if i + ahead < n:
            copy(i + ahead).start()
        copy(i).wait()
        dst[dst_row:dst_row + cols, i * chunk:(i + 1) * chunk] = (
            stage[i % STAGE_SLOTS].T.astype(dst.dtype))


def _layer_kernel(layer, tiles_per_row, n_tiles,
                  x0_ref, x_ref, vecs_ref, w_in_hbm, rtab_ref, wsp_ref, bsp_ref,
                  w_out_hbm, w1_hbm, w2_hbm,
                  o_ref, k_sc, v_sc, km_sc, bias_sc, y_sc, act_sc, qg_sc, ag_sc,
                  w_qv_sc, w_kug_sc, w_out_sc, w1_sc, w2_sc, st_k, st_d, st_ff, sem):
    g = pl.program_id(0)
    mixg_ref = vecs_ref.at[0:1, :]
    ffng_ref = vecs_ref.at[1:2, :]
    kg_ref = vecs_ref.at[2:3, 0:ATT_WIDTH]
    sgug_ref = vecs_ref.at[2:3, ATT_WIDTH:ATT_WIDTH + GMLP_WIDTH]
    gg_ref = vecs_ref.at[3:4, 0:GMLP_WIDTH]
    t = lax.rem(jnp.minimum(g + 1, n_tiles - 1), tiles_per_row)
    row0 = pl.multiple_of(t * TQ, TQ)

    def mix_and_ffn(x, y_prev, t_x, row0_x):
        return _step(x, y_prev, t_x, row0_x, mixg_ref, w_qv_sc, w_kug_sc, qg_sc, kg_ref, sgug_ref,
                     wsp_ref, bsp_ref, ag_sc, gg_ref, w_out_sc, ffng_ref, w1_sc, w2_sc,
                     k_sc, v_sc, km_sc, bias_sc, act_sc)

    def zero_left_context():
        k_sc[0:LEFT, :] = jnp.zeros((LEFT, ATT_WIDTH), k_sc.dtype)
        v_sc[0:PAD_TILES] = jnp.zeros((PAD_TILES, ATT_WIDTH, TQ), v_sc.dtype)

    @pl.when(g == 0)
    def _():
        _load_weight_transposed(w_in_hbm.at[layer], w_qv_sc, st_k, sem, src_col=0, dst_row=0)
        _load_weight_transposed(w_in_hbm.at[layer], w_qv_sc, st_k, sem,
                                src_col=2 * ATT_WIDTH, dst_row=ATT_WIDTH)
        _load_weight(w_in_hbm.at[layer], w_kug_sc, st_k, sem, src_col=ATT_WIDTH, dst_col=0)
        _load_weight(w_in_hbm.at[layer], w_kug_sc, st_d, sem,
                     src_col=3 * ATT_WIDTH, dst_col=ATT_WIDTH)
        _load_weight(w_out_hbm.at[layer], w_out_sc, st_d, sem)
        _load_weight(w1_hbm.at[layer], w1_sc, st_ff, sem)
        _load_weight(w2_hbm.at[layer], w2_sc, st_d, sem)
        _build_bias(rtab_ref, bias_sc)
        qg_sc[...] = jnp.broadcast_to(vecs_ref[4:5, 0:ATT_WIDTH], (TQ, ATT_WIDTH)).T
        ag_sc[...] = jnp.broadcast_to(vecs_ref[3:4, GMLP_WIDTH:GMLP_WIDTH + ATT_WIDTH],
                                      (TQ, ATT_WIDTH)).T
        is_pad = ((lax.broadcasted_iota(jnp.int32, km_sc.shape, 0) < LEFT)
                  & (lax.broadcasted_iota(jnp.int32, km_sc.shape, 1) == 0))
        km_sc[...] = jnp.where(is_pad, NEG_INF, 0.0).astype(km_sc.dtype)
        zero_left_context()
        y_sc[...] = mix_and_ffn(x0_ref[...], None, 0, 0)[0]

    pl.when(t == 0)(zero_left_context)

    mix_out, ffn_out = mix_and_ffn(x_ref[...], y_sc[...], t, row0)
    o_ref[...] = ffn_out
    y_sc[...] = mix_out


def _layer_spec(layer, shape):
    zeros = (0,) * len(shape)
    return pl.BlockSpec((None,) + shape, lambda g: (layer,) + zeros,
                        pipeline_mode=pl.Buffered(1))


def _layer(layer, x, vecs, w_in, rtab, wsp, bsp, w_out, w1, w2):
    B, S, D = x.shape
    tiles_per_row = S // TQ
    n_tiles = B * tiles_per_row
    bf16 = jnp.bfloat16
    f32 = jnp.float32

    def next_tile(g):
        gi = jnp.minimum(g + 1, n_tiles - 1)
        return (gi // tiles_per_row, gi % tiles_per_row, 0)

    def out_tile(g):
        return (g // tiles_per_row, g % tiles_per_row, 0)

    hbm = pl.BlockSpec(memory_space=pl.ANY)
    return pl.pallas_call(
        functools.partial(_layer_kernel, layer, tiles_per_row, n_tiles),
        grid=(n_tiles,),
        in_specs=[
            pl.BlockSpec((None, TQ, D), lambda g: (0, 0, 0), pipeline_mode=pl.Buffered(1)),
            pl.BlockSpec((None, TQ, D), next_tile),
            _layer_spec(layer, (GAIN_ROWS, D)),
            hbm,
            _layer_spec(layer, (ATT_HEADS, MXU_COLS)),
            _layer_spec(layer, (GMLP_GROUPS, GMLP_BLOCK, GMLP_BLOCK)),
            _layer_spec(layer, (GMLP_BLOCK, GMLP_WIDTH)),
            hbm,
            hbm,
            hbm,
        ],
        out_specs=pl.BlockSpec((None, TQ, D), out_tile),
        out_shape=jax.ShapeDtypeStruct(x.shape, x.dtype),
        scratch_shapes=[
            pltpu.VMEM((LEFT + S, ATT_WIDTH), bf16),
            pltpu.VMEM((PAD_TILES + tiles_per_row, ATT_WIDTH, TQ), bf16),
            pltpu.VMEM((LEFT + S, LANES), bf16),
            pltpu.VMEM((ATT_HEADS, TK, TQ), f32),
            pltpu.VMEM((TQ, D), f32),
            pltpu.VMEM((TQ, D_FF), bf16),
            pltpu.VMEM((ATT_WIDTH, TQ), f32),
            pltpu.VMEM((ATT_WIDTH, TQ), f32),
            pltpu.VMEM((2 * ATT_WIDTH, D), bf16),
            pltpu.VMEM((D, KUG_WIDTH), bf16),
            pltpu.VMEM((D, D), bf16),
            pltpu.VMEM((D, 2 * D_FF), bf16),
            pltpu.VMEM((D_FF, D), bf16),
            pltpu.VMEM((STAGE_SLOTS, STAGE_ROWS[ATT_WIDTH], ATT_WIDTH), f32),
            pltpu.VMEM((STAGE_SLOTS, STAGE_ROWS[D], D), f32),
            pltpu.VMEM((STAGE_SLOTS, STAGE_ROWS[2 * D_FF], 2 * D_FF), f32),
            pltpu.SemaphoreType.DMA((STAGE_SLOTS,)),
        ],
        compiler_params=pltpu.CompilerParams(
            dimension_semantics=("arbitrary",),
            vmem_limit_bytes=VMEM_LIMIT_BYTES),
        name="layer",
    )(x, x, vecs, w_in, rtab, wsp, bsp, w_out, w1, w2)


def _reversed_rel_table(rel_bias):
    near = rel_bias[:, :, ::-1][:, :, :NEAR]
    far = jnp.broadcast_to(rel_bias[:, :, 2 * MAX_REL:], near.shape[:2] + (MXU_COLS - NEAR,))
    return jnp.concatenate([near, far], axis=2).astype(jnp.float32)


def _pack_gains(mix_norm_g, ffn_norm_g, k_norm_g, sgu_norm_g, gmlp_out_norm_g, att_out_norm_g,
                q_norm_g):
    depth = mix_norm_g.shape[0]
    tiled = lambda g: jnp.tile(g, (1, ATT_HEADS))
    rows = [mix_norm_g, ffn_norm_g,
            jnp.concatenate([tiled(k_norm_g), sgu_norm_g], axis=1),
            jnp.concatenate([gmlp_out_norm_g, att_out_norm_g], axis=1),
            jnp.concatenate([tiled(q_norm_g), jnp.zeros((depth, D_MODEL - ATT_WIDTH))], axis=1)]
    rows += [jnp.zeros((depth, D_MODEL))] * (GAIN_ROWS - len(rows))
    return jnp.stack(rows, axis=1).astype(jnp.float32)


def kernel(x, mix_norm_g, w_in, q_norm_g, k_norm_g, rel_bias, sgu_norm_g, w_spatial, b_spatial,
           att_out_norm_g, gmlp_out_norm_g, w_out, ffn_norm_g, w_ffn_in, w_ffn_out):
    depth = w_in.shape[0]
    params = (
        _pack_gains(mix_norm_g, ffn_norm_g, k_norm_g, sgu_norm_g, gmlp_out_norm_g,
                    att_out_norm_g, q_norm_g),
        w_in, _reversed_rel_table(rel_bias), w_spatial,
        jnp.repeat(jnp.swapaxes(b_spatial, 1, 2), GMLP_GROUP_DIM, axis=2),
        w_out, w_ffn_in, w_ffn_out)
    for l in range(depth):
        x = _layer(l, x, *params)
    return x
```

```python
import functools

import jax
import jax.numpy as jnp
from jax import lax
from jax.experimental import pallas as pl
from jax.experimental.pallas import tpu as pltpu

D_MODEL = 1024
CHUNK = 64
ATT_HEADS = 8
HEAD_DIM = 64
ATT_WIDTH = ATT_HEADS * HEAD_DIM
LEFT_CHUNKS = 8
LEFT = LEFT_CHUNKS * CHUNK
MAX_REL = 2 * CHUNK
NEAR = MAX_REL + CHUNK
GMLP_WIDTH = 512
GMLP_GROUPS = 8
GMLP_GROUP_DIM = GMLP_WIDTH // GMLP_GROUPS
GMLP_BLOCK = 128
D_FF = 2816
KUG_WIDTH = ATT_WIDTH + 2 * GMLP_WIDTH
EPS = 1e-6
NEG_INF = -1e30

LANES = 128
BF16_ROWS = 16
MXU_COLS = 256
TQ = 256
TK = LEFT + TQ
PAD_TILES = LEFT // TQ
WIN_TILES = TK // TQ
FF_CHUNK = 256
FFN_CHUNKS_FIRST = 1
HEADS_PER_ROUND = 2
STAGE_ROWS = {ATT_WIDTH: 128, D_MODEL: 128, 2 * D_FF: 32}
GAIN_ROWS = 8
STAGE_SLOTS = 4
VMEM_LIMIT_BYTES = 61 * 1024 * 1024


def _rms(x, g):
    ms = jnp.mean(x * x, axis=-1, keepdims=True)
    return (x * lax.rsqrt(ms + EPS)) * g


def _head_rms(z, gain):
    rows = z.shape[0]
    lo = lax.broadcasted_iota(jnp.int32, (rows, LANES), 1) < HEAD_DIM
    outs = []
    for c in range(ATT_WIDTH // LANES):
        zc = z[:, c * LANES:(c + 1) * LANES]
        sq = zc * zc
        ms_lo = jnp.sum(jnp.where(lo, sq, 0.0), axis=-1, keepdims=True) * (1.0 / HEAD_DIM)
        ms_hi = jnp.sum(jnp.where(lo, 0.0, sq), axis=-1, keepdims=True) * (1.0 / HEAD_DIM)
        r = jnp.where(lo, lax.rsqrt(ms_lo + EPS), lax.rsqrt(ms_hi + EPS))
        outs.append((zc * r) * gain[:, c * LANES:(c + 1) * LANES])
    return jnp.concatenate(outs, axis=-1)


def _build_bias(rtab_ref, bias_sc):
    f32 = jnp.float32
    row = lax.broadcasted_iota(jnp.int32, (CHUNK, MXU_COLS), 0)
    lo_half = lax.broadcasted_iota(jnp.int32, (CHUNK, LANES), 1) < CHUNK
    n_kc = TK // CHUNK
    for head in range(ATT_HEADS):
        r = rtab_ref[head:head + 1, :]
        far = rtab_ref[head:head + 1, MXU_COLS - 1:MXU_COLS]
        x0 = jnp.broadcast_to(r, (CHUNK, MXU_COLS))
        for b in range(CHUNK.bit_length() - 1):
            x0 = jnp.where(((row >> b) & 1) == 1, pltpu.roll(x0, 1 << b, axis=1), x0)
        x1 = pltpu.roll(x0, CHUNK, axis=1)
        far_piece = jnp.broadcast_to(far, (CHUNK, LANES))
        neg_piece = jnp.full((CHUNK, LANES), NEG_INF, f32)

        def half(qc, kc):
            rel = kc - qc
            if rel < 0 or rel > LEFT_CHUNKS:
                return neg_piece
            if rel < LEFT_CHUNKS - 2:
                return far_piece
            col = (rel - (LEFT_CHUNKS - 2)) * CHUNK + (qc % 2) * CHUNK
            src = x1 if qc % 2 else x0
            return src[:, (col // LANES) * LANES:(col // LANES + 1) * LANES]

        row_blocks = []
        for qc in range(TQ // CHUNK):
            pieces = [jnp.where(lo_half, half(qc, 2 * vcol), half(qc, 2 * vcol + 1))
                      for vcol in range(n_kc // 2)]
            row_blocks.append(jnp.concatenate(pieces, axis=1))
        bias_sc[head] = jnp.concatenate(row_blocks, axis=0).T


def _step(x, y_prev, hf, t, row0, mixg_ref, w_qv_ref, w_kug_ref, qg_ref, kg_ref, sgug_ref, wsp_ref,
          bsp_ref, ag_ref, gg_ref, w_out_ref, ffng_ref, w1_ref, w2_ref,
          k_sc, v_sc, km_sc, bias_sc, act_sc):
    bf16 = jnp.bfloat16
    f32 = jnp.float32
    nt = (((1,), (1,)), ((), ()))
    do_ffn = y_prev is not None
    h = _rms(x, mixg_ref[...]).astype(bf16)

    def ffn_chunk(j):
        if not do_ffn:
            return
        gate = jnp.dot(hf, w1_ref[:, j * FF_CHUNK:(j + 1) * FF_CHUNK], preferred_element_type=f32)
        up = jnp.dot(hf, w1_ref[:, D_FF + j * FF_CHUNK:D_FF + (j + 1) * FF_CHUNK],
                     preferred_element_type=f32)
        act_sc[:, j * FF_CHUNK:(j + 1) * FF_CHUNK] = (jax.nn.silu(gate) * up).astype(bf16)

    for j in range(FFN_CHUNKS_FIRST):
        ffn_chunk(j)
    qv_t = lax.dot_general(w_qv_ref[...], h, nt, preferred_element_type=f32)
    k = jnp.dot(h, w_kug_ref[:, 0:ATT_WIDTH], preferred_element_type=f32)
    u = jnp.dot(h, w_kug_ref[:, ATT_WIDTH:ATT_WIDTH + GMLP_WIDTH], preferred_element_type=f32)
    vg = jnp.dot(h, w_kug_ref[:, ATT_WIDTH + GMLP_WIDTH:KUG_WIDTH], preferred_element_type=f32)

    q3 = qv_t[0:ATT_WIDTH].reshape(ATT_HEADS, HEAD_DIM, TQ)
    q_ms = jnp.mean(q3 * q3, axis=1, keepdims=True)
    q3 = (q3 * lax.rsqrt(q_ms + EPS)) * qg_ref[...].reshape(ATT_HEADS, HEAD_DIM, TQ)
    qn_t = (q3 * HEAD_DIM ** -0.5).reshape(ATT_WIDTH, TQ).astype(bf16)
    kn = _head_rms(k, kg_ref[...]).astype(bf16)
    k_sc[pl.ds(row0 + LEFT, TQ), :] = kn
    v_sc[t + PAD_TILES] = qv_t[ATT_WIDTH:2 * ATT_WIDTH].astype(bf16)
    kw = k_sc[pl.ds(row0, TK), :]
    km = km_sc[pl.ds(row0, TK), :]
    pad_rows = jnp.where(lax.broadcasted_iota(jnp.int32, (LANES, TQ), 0) == 0,
                         1.0, 0.0).astype(bf16)
    zero_rows = jnp.zeros((HEAD_DIM, TQ), bf16)
    ones_rows = jnp.ones((BF16_ROWS, TK), bf16)

    def scores(head):
        c = head // 2
        qh = qn_t[head * HEAD_DIM:(head + 1) * HEAD_DIM]
        pair = [qh, zero_rows] if head % 2 == 0 else [zero_rows, qh]
        rhs = jnp.concatenate(pair + [pad_rows], axis=0)
        lhs = jnp.concatenate([kw[:, c * LANES:(c + 1) * LANES], km], axis=1)
        s = jnp.dot(lhs, rhs, preferred_element_type=f32)
        return s + bias_sc[head]

    def probs(s):
        m = jnp.max(s, axis=0, keepdims=True)
        return jnp.exp(s - m).astype(bf16)

    def weighted_values(head, p):
        v_t = jnp.concatenate(
            [v_sc[t + w, head * HEAD_DIM:(head + 1) * HEAD_DIM, :] for w in range(WIN_TILES)],
            axis=1)
        pv = jnp.dot(jnp.concatenate([v_t, ones_rows], axis=0), p,
                     preferred_element_type=f32)
        return pv[0:HEAD_DIM] * (1.0 / pv[HEAD_DIM:HEAD_DIM + 1])

    rounds = [list(range(r, r + HEADS_PER_ROUND)) for r in range(0, ATT_HEADS, HEADS_PER_ROUND)]
    outs = []
    next_chunk = FFN_CHUNKS_FIRST
    s_next = [scores(hd) for hd in rounds[0]]
    for ri, heads in enumerate(rounds):
        s_cur = s_next
        if ri + 1 < len(rounds):
            s_next = [scores(hd) for hd in rounds[ri + 1]]
        for _ in heads:
            ffn_chunk(next_chunk)
            next_chunk += 1
        ps = [probs(s) for s in s_cur]
        outs += [weighted_values(hd, p) for hd, p in zip(heads, ps)]
    a_t = jnp.concatenate(outs, axis=0)
    a_ms = jnp.mean(a_t * a_t, axis=0, keepdims=True)
    a_n = ((a_t * lax.rsqrt(a_ms + EPS)) * ag_ref[...]).T.astype(bf16)

    u_act = jax.nn.gelu(u)
    vgn = _rms(jax.nn.gelu(vg), sgug_ref[...]).astype(bf16)
    ti = lax.broadcasted_iota(jnp.int32, (GMLP_GROUPS, GMLP_BLOCK, GMLP_BLOCK), 1)
    si = lax.broadcasted_iota(jnp.int32, (GMLP_GROUPS, GMLP_BLOCK, GMLP_BLOCK), 2)
    causal = (ti // CHUNK) >= (si // CHUNK)
    wsp = jnp.where(causal, wsp_ref[...], 0.0).astype(bf16)
    wsp = wsp.reshape(GMLP_GROUPS * GMLP_BLOCK, GMLP_BLOCK)
    first_of_pair = lax.broadcasted_iota(jnp.int32, (GMLP_BLOCK, LANES), 1) < GMLP_GROUP_DIM
    n_blk = TQ // GMLP_BLOCK
    pair_cols = []
    for pr in range(GMLP_WIDTH // LANES):
        lhs = wsp[2 * pr * GMLP_BLOCK:2 * (pr + 1) * GMLP_BLOCK, :]
        rhs = jnp.concatenate(
            [vgn[blk * GMLP_BLOCK:(blk + 1) * GMLP_BLOCK, pr * LANES:(pr + 1) * LANES]
             for blk in range(n_blk)], axis=1)
        res = jnp.dot(lhs, rhs, preferred_element_type=f32)
        pair_cols.append(jnp.concatenate(
            [jnp.where(first_of_pair,
                       res[0:GMLP_BLOCK, blk * LANES:(blk + 1) * LANES],
                       res[GMLP_BLOCK:2 * GMLP_BLOCK, blk * LANES:(blk + 1) * LANES])
             for blk in range(n_blk)], axis=0))
        if next_chunk < D_FF // FF_CHUNK:
            ffn_chunk(next_chunk)
            next_chunk += 1
    while next_chunk < D_FF // FF_CHUNK:
        ffn_chunk(next_chunk)
        next_chunk += 1
    mixed = (jnp.concatenate(pair_cols, axis=1)
             + jnp.concatenate([bsp_ref[...]] * n_blk, axis=0))
    g_n = _rms(u_act * mixed, gg_ref[...]).astype(bf16)

    mix = jnp.concatenate([a_n, g_n], axis=-1)
    mix_out = x + jnp.dot(mix, w_out_ref[...], preferred_element_type=f32)
    ffn_out = None
    if do_ffn:
        ffn_out = y_prev + jnp.dot(act_sc[...], w2_ref[...], preferred_element_type=f32)
    hf_next = _rms(mix_out, ffng_ref[...]).astype(bf16)
    return mix_out, hf_next, ffn_out


def _load_weight(src, dst, stage, sem, src_col=0, dst_col=0):
    rows = src.shape[0]
    _, chunk, cols = stage.shape
    assert rows % chunk == 0 and chunk == STAGE_ROWS[cols] and stage.shape[0] == STAGE_SLOTS
    n = rows // chunk
    ahead = STAGE_SLOTS - 1
    assert n >= ahead

    def copy(i, slot):
        return pltpu.make_async_copy(src.at[pl.ds(i * chunk, chunk), pl.ds(src_col, cols)],
                                     stage.at[slot], sem.at[slot])

    for i in range(ahead):
        copy(i, i).start()

    def body(i, carry):
        slot = lax.rem(i, STAGE_SLOTS)

        @pl.when(i + ahead < n)
        def _():
            copy(i + ahead, lax.rem(i + ahead, STAGE_SLOTS)).start()

        copy(i, slot).wait()
        dst[pl.ds(pl.multiple_of(i * chunk, chunk), chunk), dst_col:dst_col + cols] = (
            stage[slot].astype(dst.dtype))
        return carry

    lax.fori_loop(0, n, body, 0)


def _load_weight_transposed(src, dst, stage, sem, src_col, dst_row):
    rows = src.shape[0]
    _, chunk, cols = stage.shape
    assert rows % chunk == 0 and chunk % LANES == 0 and stage.shape[0] == STAGE_SLOTS
    n = rows // chunk
    ahead = STAGE_SLOTS - 1
    assert n >= ahead

    def copy(i):
        slot = i % STAGE_SLOTS
        return pltpu.make_async_copy(src.at[pl.ds(i * chunk, chunk), pl.ds(src_col, cols)],
                                     stage.at[slot], sem.at[slot])

    for i in range(ahead):
        copy(i).start()
    for i in range(n):
        if i + ahead < n:
            copy(i + ahead).start()
        copy(i).wait()
        dst[dst_row:dst_row + cols, i * chunk:(i + 1) * chunk] = (
            stage[i % STAGE_SLOTS].T.astype(dst.dtype))


def _layer_kernel(layer, tiles_per_row, n_tiles,
                  x0_ref, x_ref, vecs_ref, w_in_hbm, rtab_ref, wsp_ref, bsp_ref,
                  w_out_hbm, w1_hbm, w2_hbm,
                  o_ref, k_sc, v_sc, km_sc, bias_sc, y_sc, hf_sc, act_sc, qg_sc, ag_sc,
                  w_qv_sc, w_kug_sc, w_out_sc, w1_sc, w2_sc, st_k, st_d, st_ff, sem):
    g = pl.program_id(0)
    mixg_ref = vecs_ref.at[0:1, :]
    ffng_ref = vecs_ref.at[1:2, :]
    kg_ref = vecs_ref.at[2:3, 0:ATT_WIDTH]
    sgug_ref = vecs_ref.at[2:3, ATT_WIDTH:ATT_WIDTH + GMLP_WIDTH]
    gg_ref = vecs_ref.at[3:4, 0:GMLP_WIDTH]
    t = lax.rem(jnp.minimum(g + 1, n_tiles - 1), tiles_per_row)
    row0 = pl.multiple_of(t * TQ, TQ)

    def mix_and_ffn(x, y_prev, hf, t_x, row0_x):
        return _step(x, y_prev, hf, t_x, row0_x, mixg_ref, w_qv_sc, w_kug_sc, qg_sc, kg_ref, sgug_ref,
                     wsp_ref, bsp_ref, ag_sc, gg_ref, w_out_sc, ffng_ref, w1_sc, w2_sc,
                     k_sc, v_sc, km_sc, bias_sc, act_sc)

    def zero_left_context():
        k_sc[0:LEFT, :] = jnp.zeros((LEFT, ATT_WIDTH), k_sc.dtype)
        v_sc[0:PAD_TILES] = jnp.zeros((PAD_TILES, ATT_WIDTH, TQ), v_sc.dtype)

    @pl.when(g == 0)
    def _():
        _load_weight_transposed(w_in_hbm.at[layer], w_qv_sc, st_k, sem, src_col=0, dst_row=0)
        _load_weight_transposed(w_in_hbm.at[layer], w_qv_sc, st_k, sem,
                                src_col=2 * ATT_WIDTH, dst_row=ATT_WIDTH)
        _load_weight(w_in_hbm.at[layer], w_kug_sc, st_k, sem, src_col=ATT_WIDTH, dst_col=0)
        _load_weight(w_in_hbm.at[layer], w_kug_sc, st_d, sem,
                     src_col=3 * ATT_WIDTH, dst_col=ATT_WIDTH)
        _load_weight(w_out_hbm.at[layer], w_out_sc, st_d, sem)
        _load_weight(w1_hbm.at[layer], w1_sc, st_ff, sem)
        _load_weight(w2_hbm.at[layer], w2_sc, st_d, sem)
        _build_bias(rtab_ref, bias_sc)
        qg_sc[...] = jnp.broadcast_to(vecs_ref[4:5, 0:ATT_WIDTH], (TQ, ATT_WIDTH)).T
        ag_sc[...] = jnp.broadcast_to(vecs_ref[3:4, GMLP_WIDTH:GMLP_WIDTH + ATT_WIDTH],
                                      (TQ, ATT_WIDTH)).T
        is_pad = ((lax.broadcasted_iota(jnp.int32, km_sc.shape, 0) < LEFT)
                  & (lax.broadcasted_iota(jnp.int32, km_sc.shape, 1) == 0))
        km_sc[...] = jnp.where(is_pad, NEG_INF, 0.0).astype(km_sc.dtype)
        zero_left_context()
        y_sc[...], hf_sc[...], _ = mix_and_ffn(x0_ref[...], None, None, 0, 0)

    pl.when(t == 0)(zero_left_context)

    mix_out, hf_next, ffn_out = mix_and_ffn(x_ref[...], y_sc[...], hf_sc[...], t, row0)
    o_ref[...] = ffn_out
    y_sc[...] = mix_out
    hf_sc[...] = hf_next


def _layer_spec(layer, shape):
    zeros = (0,) * len(shape)
    return pl.BlockSpec((None,) + shape, lambda g: (layer,) + zeros,
                        pipeline_mode=pl.Buffered(1))


def _layer(layer, x, vecs, w_in, rtab, wsp, bsp, w_out, w1, w2):
    B, S, D = x.shape
    tiles_per_row = S // TQ
    n_tiles = B * tiles_per_row
    bf16 = jnp.bfloat16
    f32 = jnp.float32

    def next_tile(g):
        gi = jnp.minimum(g + 1, n_tiles - 1)
        return (gi // tiles_per_row, gi % tiles_per_row, 0)

    def out_tile(g):
        return (g // tiles_per_row, g % tiles_per_row, 0)

    hbm = pl.BlockSpec(memory_space=pl.ANY)
    return pl.pallas_call(
        functools.partial(_layer_kernel, layer, tiles_per_row, n_tiles),
        grid=(n_tiles,),
        in_specs=[
            pl.BlockSpec((None, TQ, D), lambda g: (0, 0, 0), pipeline_mode=pl.Buffered(1)),
            pl.BlockSpec((None, TQ, D), next_tile),
            _layer_spec(layer, (GAIN_ROWS, D)),
            hbm,
            _layer_spec(layer, (ATT_HEADS, MXU_COLS)),
            _layer_spec(layer, (GMLP_GROUPS, GMLP_BLOCK, GMLP_BLOCK)),
            _layer_spec(layer, (GMLP_BLOCK, GMLP_WIDTH)),
            hbm,
            hbm,
            hbm,
        ],
        out_specs=pl.BlockSpec((None, TQ, D), out_tile),
        out_shape=jax.ShapeDtypeStruct(x.shape, x.dtype),
        scratch_shapes=[
            pltpu.VMEM((LEFT + S, ATT_WIDTH), bf16),
            pltpu.VMEM((PAD_TILES + tiles_per_row, ATT_WIDTH, TQ), bf16),
            pltpu.VMEM((LEFT + S, LANES), bf16),
            pltpu.VMEM((ATT_HEADS, TK, TQ), f32),
            pltpu.VMEM((TQ, D), f32),
            pltpu.VMEM((TQ, D), bf16),
            pltpu.VMEM((TQ, D_FF), bf16),
            pltpu.VMEM((ATT_WIDTH, TQ), f32),
            pltpu.VMEM((ATT_WIDTH, TQ), f32),
            pltpu.VMEM((2 * ATT_WIDTH, D), bf16),
            pltpu.VMEM((D, KUG_WIDTH), bf16),
            pltpu.VMEM((D, D), bf16),
            pltpu.VMEM((D, 2 * D_FF), bf16),
            pltpu.VMEM((D_FF, D), bf16),
            pltpu.VMEM((STAGE_SLOTS, STAGE_ROWS[ATT_WIDTH], ATT_WIDTH), f32),
            pltpu.VMEM((STAGE_SLOTS, STAGE_ROWS[D], D), f32),
            pltpu.VMEM((STAGE_SLOTS, STAGE_ROWS[2 * D_FF], 2 * D_FF), f32),
            pltpu.SemaphoreType.DMA((STAGE_SLOTS,)),
        ],
        compiler_params=pltpu.CompilerParams(
            dimension_semantics=("arbitrary",),
            vmem_limit_bytes=VMEM_LIMIT_BYTES),
        name="layer",
    )(x, x, vecs, w_in, rtab, wsp, bsp, w_out, w1, w2)


def _reversed_rel_table(rel_bias):
    near = rel_bias[:, :, ::-1][:, :, :NEAR]
    far = jnp.broadcast_to(rel_bias[:, :, 2 * MAX_REL:], near.shape[:2] + (MXU_COLS - NEAR,))
    return jnp.concatenate([near, far], axis=2).astype(jnp.float32)


def _pack_gains(mix_norm_g, ffn_norm_g, k_norm_g, sgu_norm_g, gmlp_out_norm_g, att_out_norm_g,
                q_norm_g):
    depth = mix_norm_g.shape[0]
    tiled = lambda g: jnp.tile(g, (1, ATT_HEADS))
    rows = [mix_norm_g, ffn_norm_g,
            jnp.concatenate([tiled(k_norm_g), sgu_norm_g], axis=1),
            jnp.concatenate([gmlp_out_norm_g, att_out_norm_g], axis=1),
            jnp.concatenate([tiled(q_norm_g), jnp.zeros((depth, D_MODEL - ATT_WIDTH))], axis=1)]
    rows += [jnp.zeros((depth, D_MODEL))] * (GAIN_ROWS - len(rows))
    return jnp.stack(rows, axis=1).astype(jnp.float32)


def kernel(x, mix_norm_g, w_in, q_norm_g, k_norm_g, rel_bias, sgu_norm_g, w_spatial, b_spatial,
           att_out_norm_g, gmlp_out_norm_g, w_out, ffn_norm_g, w_ffn_in, w_ffn_out):
    depth = w_in.shape[0]
    params = (
        _pack_gains(mix_norm_g, ffn_norm_g, k_norm_g, sgu_norm_g, gmlp_out_norm_g,
                    att_out_norm_g, q_norm_g),
        w_in, _reversed_rel_table(rel_bias), w_spatial,
        jnp.repeat(jnp.swapaxes(b_spatial, 1, 2), GMLP_GROUP_DIM, axis=2),
        w_out, w_ffn_in, w_ffn_out)
    for l in range(depth):
        x = _layer(l, x, *params)
    return x
```

```python
import functools

import jax
import jax.numpy as jnp
from jax import lax
from jax.experimental import pallas as pl
from jax.experimental.pallas import tpu as pltpu

D_MODEL = 1024
CHUNK = 64
ATT_HEADS = 8
HEAD_DIM = 64
ATT_WIDTH = ATT_HEADS * HEAD_DIM
LEFT_CHUNKS = 8
LEFT = LEFT_CHUNKS * CHUNK
MAX_REL = 2 * CHUNK
NEAR = MAX_REL + CHUNK
GMLP_WIDTH = 512
GMLP_GROUPS = 8
GMLP_GROUP_DIM = GMLP_WIDTH // GMLP_GROUPS
GMLP_BLOCK = 128
D_FF = 2816
KUG_WIDTH = ATT_WIDTH + 2 * GMLP_WIDTH
EPS = 1e-6
NEG_INF = -1e30

LANES = 128
BF16_ROWS = 16
MXU_COLS = 256
TQ = 256
TK = LEFT + TQ
PAD_TILES = LEFT // TQ
WIN_TILES = TK // TQ
FF_CHUNK = 256
HEADS_PER_ROUND = 2
STAGE_ROWS = {ATT_WIDTH: 128, D_MODEL: 128, 2 * D_FF: 32}
GAIN_ROWS = 8
STAGE_SLOTS = 4
VMEM_LIMIT_BYTES = 60 * 1024 * 1024


def _rms(x, g):
    ms = jnp.mean(x * x, axis=-1, keepdims=True)
    return (x * lax.rsqrt(ms + EPS)) * g


def _head_rms(z, gain):
    rows = z.shape[0]
    lo = lax.broadcasted_iota(jnp.int32, (rows, LANES), 1) < HEAD_DIM
    outs = []
    for c in range(ATT_WIDTH // LANES):
        zc = z[:, c * LANES:(c + 1) * LANES]
        sq = zc * zc
        ms_lo = jnp.sum(jnp.where(lo, sq, 0.0), axis=-1, keepdims=True) * (1.0 / HEAD_DIM)
        ms_hi = jnp.sum(jnp.where(lo, 0.0, sq), axis=-1, keepdims=True) * (1.0 / HEAD_DIM)
        r = jnp.where(lo, lax.rsqrt(ms_lo + EPS), lax.rsqrt(ms_hi + EPS))
        outs.append((zc * r) * gain[:, c * LANES:(c + 1) * LANES])
    return jnp.concatenate(outs, axis=-1)


def _build_bias(rtab_ref, bias_sc):
    f32 = jnp.float32
    row = lax.broadcasted_iota(jnp.int32, (CHUNK, MXU_COLS), 0)
    lo_half = lax.broadcasted_iota(jnp.int32, (CHUNK, LANES), 1) < CHUNK
    n_kc = TK // CHUNK
    for head in range(ATT_HEADS):
        r = rtab_ref[head:head + 1, :]
        far = rtab_ref[head:head + 1, MXU_COLS - 1:MXU_COLS]
        x0 = jnp.broadcast_to(r, (CHUNK, MXU_COLS))
        for b in range(CHUNK.bit_length() - 1):
            x0 = jnp.where(((row >> b) & 1) == 1, pltpu.roll(x0, 1 << b, axis=1), x0)
        x1 = pltpu.roll(x0, CHUNK, axis=1)
        far_piece = jnp.broadcast_to(far, (CHUNK, LANES))
        neg_piece = jnp.full((CHUNK, LANES), NEG_INF, f32)

        def half(qc, kc):
            rel = kc - qc
            if rel < 0 or rel > LEFT_CHUNKS:
                return neg_piece
            if rel < LEFT_CHUNKS - 2:
                return far_piece
            col = (rel - (LEFT_CHUNKS - 2)) * CHUNK + (qc % 2) * CHUNK
            src = x1 if qc % 2 else x0
            return src[:, (col // LANES) * LANES:(col // LANES + 1) * LANES]

        row_blocks = []
        for qc in range(TQ // CHUNK):
            pieces = [jnp.where(lo_half, half(qc, 2 * vcol), half(qc, 2 * vcol + 1))
                      for vcol in range(n_kc // 2)]
            row_blocks.append(jnp.concatenate(pieces, axis=1))
        bias_sc[head] = jnp.concatenate(row_blocks, axis=0).T


def _step(x, y_prev, t, row0, mixg_ref, w_qv_ref, w_kug_ref, qg_ref, kg_ref, sgug_ref, wsp_ref,
          bsp_ref, ag_ref, gg_ref, w_out_ref, ffng_ref, w1_ref, w2_ref,
          k_sc, v_sc, km_sc, bias_sc, act_sc):
    bf16 = jnp.bfloat16
    f32 = jnp.float32
    nt = (((1,), (1,)), ((), ()))
    do_ffn = y_prev is not None
    h = _rms(x, mixg_ref[...]).astype(bf16)
    if do_ffn:
        hf = _rms(y_prev, ffng_ref[...]).astype(bf16)

    def ffn_chunk(j):
        if not do_ffn:
            return
        gate = jnp.dot(hf, w1_ref[:, j * FF_CHUNK:(j + 1) * FF_CHUNK], preferred_element_type=f32)
        up = jnp.dot(hf, w1_ref[:, D_FF + j * FF_CHUNK:D_FF + (j + 1) * FF_CHUNK],
                     preferred_element_type=f32)
        act_sc[:, j * FF_CHUNK:(j + 1) * FF_CHUNK] = (jax.nn.silu(gate) * up).astype(bf16)

    qv_t = lax.dot_general(w_qv_ref[...], h, nt, preferred_element_type=f32)
    k = jnp.dot(h, w_kug_ref[:, 0:ATT_WIDTH], preferred_element_type=f32)
    u = jnp.dot(h, w_kug_ref[:, ATT_WIDTH:ATT_WIDTH + GMLP_WIDTH], preferred_element_type=f32)
    vg = jnp.dot(h, w_kug_ref[:, ATT_WIDTH + GMLP_WIDTH:KUG_WIDTH], preferred_element_type=f32)

    q3 = qv_t[0:ATT_WIDTH].reshape(ATT_HEADS, HEAD_DIM, TQ)
    q_ms = jnp.mean(q3 * q3, axis=1, keepdims=True)
    q3 = (q3 * lax.rsqrt(q_ms + EPS)) * qg_ref[...].reshape(ATT_HEADS, HEAD_DIM, TQ)
    qn_t = (q3 * HEAD_DIM ** -0.5).reshape(ATT_WIDTH, TQ).astype(bf16)
    kn = _head_rms(k, kg_ref[...]).astype(bf16)
    k_sc[pl.ds(row0 + LEFT, TQ), :] = kn
    v_sc[t + PAD_TILES] = qv_t[ATT_WIDTH:2 * ATT_WIDTH].astype(bf16)
    kw = k_sc[pl.ds(row0, TK), :]
    km = km_sc[pl.ds(row0, TK), :]
    pad_rows = jnp.where(lax.broadcasted_iota(jnp.int32, (LANES, TQ), 0) == 0,
                         1.0, 0.0).astype(bf16)
    zero_rows = jnp.zeros((HEAD_DIM, TQ), bf16)
    ones_rows = jnp.ones((BF16_ROWS, TK), bf16)

    def scores(head):
        c = head // 2
        qh = qn_t[head * HEAD_DIM:(head + 1) * HEAD_DIM]
        pair = [qh, zero_rows] if head % 2 == 0 else [zero_rows, qh]
        rhs = jnp.concatenate(pair + [pad_rows], axis=0)
        lhs = jnp.concatenate([kw[:, c * LANES:(c + 1) * LANES], km], axis=1)
        s = jnp.dot(lhs, rhs, preferred_element_type=f32)
        return s + bias_sc[head]

    def probs(s):
        m = jnp.max(s, axis=0, keepdims=True)
        return jnp.exp(s - m).astype(bf16)

    def weighted_values(head, p):
        v_t = jnp.concatenate(
            [v_sc[t + w, head * HEAD_DIM:(head + 1) * HEAD_DIM, :] for w in range(WIN_TILES)],
            axis=1)
        pv = jnp.dot(jnp.concatenate([v_t, ones_rows], axis=0), p,
                     preferred_element_type=f32)
        return pv[0:HEAD_DIM] * (1.0 / pv[HEAD_DIM:HEAD_DIM + 1])

    rounds = [list(range(r, r + HEADS_PER_ROUND)) for r in range(0, ATT_HEADS, HEADS_PER_ROUND)]
    outs = []
    next_chunk = 0
    s_next = [scores(hd) for hd in rounds[0]]
    for ri, heads in enumerate(rounds):
        s_cur = s_next
        if ri + 1 < len(rounds):
            s_next = [scores(hd) for hd in rounds[ri + 1]]
        for _ in heads:
            ffn_chunk(next_chunk)
            next_chunk += 1
        ps = [probs(s) for s in s_cur]
        outs += [weighted_values(hd, p) for hd, p in zip(heads, ps)]
    a_t = jnp.concatenate(outs, axis=0)
    a_ms = jnp.mean(a_t * a_t, axis=0, keepdims=True)
    a_n = ((a_t * lax.rsqrt(a_ms + EPS)) * ag_ref[...]).T.astype(bf16)

    u_act = jax.nn.gelu(u)
    vgn = _rms(jax.nn.gelu(vg), sgug_ref[...]).astype(bf16)
    ti = lax.broadcasted_iota(jnp.int32, (GMLP_GROUPS, GMLP_BLOCK, GMLP_BLOCK), 1)
    si = lax.broadcasted_iota(jnp.int32, (GMLP_GROUPS, GMLP_BLOCK, GMLP_BLOCK), 2)
    causal = (ti // CHUNK) >= (si // CHUNK)
    wsp = jnp.where(causal, wsp_ref[...], 0.0).astype(bf16)
    wsp = wsp.reshape(GMLP_GROUPS * GMLP_BLOCK, GMLP_BLOCK)
    first_of_pair = lax.broadcasted_iota(jnp.int32, (GMLP_BLOCK, LANES), 1) < GMLP_GROUP_DIM
    n_blk = TQ // GMLP_BLOCK
    pair_cols = []
    for pr in range(GMLP_WIDTH // LANES):
        lhs = wsp[2 * pr * GMLP_BLOCK:2 * (pr + 1) * GMLP_BLOCK, :]
        rhs = jnp.concatenate(
            [vgn[blk * GMLP_BLOCK:(blk + 1) * GMLP_BLOCK, pr * LANES:(pr + 1) * LANES]
             for blk in range(n_blk)], axis=1)
        res = jnp.dot(lhs, rhs, preferred_element_type=f32)
        pair_cols.append(jnp.concatenate(
            [jnp.where(first_of_pair,
                       res[0:GMLP_BLOCK, blk * LANES:(blk + 1) * LANES],
                       res[GMLP_BLOCK:2 * GMLP_BLOCK, blk * LANES:(blk + 1) * LANES])
             for blk in range(n_blk)], axis=0))
        if next_chunk < D_FF // FF_CHUNK:
            ffn_chunk(next_chunk)
            next_chunk += 1
    while next_chunk < D_FF // FF_CHUNK:
        ffn_chunk(next_chunk)
        next_chunk += 1
    mixed = (jnp.concatenate(pair_cols, axis=1)
             + jnp.concatenate([bsp_ref[...]] * n_blk, axis=0))
    g_n = _rms(u_act * mixed, gg_ref[...]).astype(bf16)

    ffn_out = None
    if do_ffn:
        ffn_out = y_prev + jnp.dot(act_sc[...], w2_ref[...], preferred_element_type=f32)
    mix = jnp.concatenate([a_n, g_n], axis=-1)
    mix_out = x + jnp.dot(mix, w_out_ref[...], preferred_element_type=f32)
    return mix_out, ffn_out


def _weight_loader(src, dst, ring, src_col=0, dst_col=0, dst_row=None):
    stage, sem = ring
    rows = src.shape[0]
    _, chunk, cols = stage.shape
    assert rows % chunk == 0 and chunk == STAGE_ROWS[cols] and stage.shape[0] == STAGE_SLOTS
    n = rows // chunk
    ahead = STAGE_SLOTS - 1
    assert n >= ahead

    def copy(i, slot):
        return pltpu.make_async_copy(src.at[pl.ds(i * chunk, chunk), pl.ds(src_col, cols)],
                                     stage.at[slot], sem.at[slot])

    def prefetch():
        for i in range(ahead):
            copy(i, i).start()

    def drain_natural():
        def body(i, carry):
            slot = lax.rem(i, STAGE_SLOTS)

            @pl.when(i + ahead < n)
            def _():
                copy(i + ahead, lax.rem(i + ahead, STAGE_SLOTS)).start()

            copy(i, slot).wait()
            dst[pl.ds(pl.multiple_of(i * chunk, chunk), chunk), dst_col:dst_col + cols] = (
                stage[slot].astype(dst.dtype))
            return carry

        lax.fori_loop(0, n, body, 0)

    def drain_transposed():
        assert chunk % LANES == 0
        for i in range(n):
            if i + ahead < n:
                copy(i + ahead, (i + ahead) % STAGE_SLOTS).start()
            copy(i, i % STAGE_SLOTS).wait()
            dst[dst_row:dst_row + cols, i * chunk:(i + 1) * chunk] = (
                stage[i % STAGE_SLOTS].T.astype(dst.dtype))

    return prefetch, (drain_natural if dst_row is None else drain_transposed)


def _layer_kernel(layer, tiles_per_row, n_tiles,
                  x0_ref, x_ref, vecs_ref, w_in_hbm, rtab_ref, wsp_ref, bsp_ref,
                  w_out_hbm, w1_hbm, w2_hbm,
                  o_ref, k_sc, v_sc, km_sc, bias_sc, y_sc, act_sc, qg_sc, ag_sc,
                  w_qv_sc, w_kug_sc, w_out_sc, w1_sc, w2_sc, st_k, st_d, st_ff,
                  sem_k, sem_d, sem_ff):
    g = pl.program_id(0)
    mixg_ref = vecs_ref.at[0:1, :]
    ffng_ref = vecs_ref.at[1:2, :]
    kg_ref = vecs_ref.at[2:3, 0:ATT_WIDTH]
    sgug_ref = vecs_ref.at[2:3, ATT_WIDTH:ATT_WIDTH + GMLP_WIDTH]
    gg_ref = vecs_ref.at[3:4, 0:GMLP_WIDTH]
    t = lax.rem(jnp.minimum(g + 1, n_tiles - 1), tiles_per_row)
    row0 = pl.multiple_of(t * TQ, TQ)

    def mix_and_ffn(x, y_prev, t_x, row0_x):
        return _step(x, y_prev, t_x, row0_x, mixg_ref, w_qv_sc, w_kug_sc, qg_sc, kg_ref, sgug_ref,
                     wsp_ref, bsp_ref, ag_sc, gg_ref, w_out_sc, ffng_ref, w1_sc, w2_sc,
                     k_sc, v_sc, km_sc, bias_sc, act_sc)

    def zero_left_context():
        k_sc[0:LEFT, :] = jnp.zeros((LEFT, ATT_WIDTH), k_sc.dtype)
        v_sc[0:PAD_TILES] = jnp.zeros((PAD_TILES, ATT_WIDTH, TQ), v_sc.dtype)

    @pl.when(g == 0)
    def _():
        w_in_l = w_in_hbm.at[layer]
        ring_k, ring_d, ring_ff = (st_k, sem_k), (st_d, sem_d), (st_ff, sem_ff)
        loads = [
            _weight_loader(w_in_l, w_qv_sc, ring_k, src_col=0, dst_row=0),
            _weight_loader(w_in_l, w_kug_sc, ring_d, src_col=3 * ATT_WIDTH, dst_col=ATT_WIDTH),
            _weight_loader(w_in_l, w_qv_sc, ring_k, src_col=2 * ATT_WIDTH, dst_row=ATT_WIDTH),
            _weight_loader(w_out_hbm.at[layer], w_out_sc, ring_d),
            _weight_loader(w_in_l, w_kug_sc, ring_k, src_col=ATT_WIDTH, dst_col=0),
            _weight_loader(w1_hbm.at[layer], w1_sc, ring_ff),
            _weight_loader(w2_hbm.at[layer], w2_sc, ring_d),
        ]
        loads[0][0]()
        for i, (_, drain) in enumerate(loads):
            if i + 1 < len(loads):
                loads[i + 1][0]()
            drain()
        _build_bias(rtab_ref, bias_sc)
        qg_sc[...] = jnp.broadcast_to(vecs_ref[4:5, 0:ATT_WIDTH], (TQ, ATT_WIDTH)).T
        ag_sc[...] = jnp.broadcast_to(vecs_ref[3:4, GMLP_WIDTH:GMLP_WIDTH + ATT_WIDTH],
                                      (TQ, ATT_WIDTH)).T
        is_pad = ((lax.broadcasted_iota(jnp.int32, km_sc.shape, 0) < LEFT)
                  & (lax.broadcasted_iota(jnp.int32, km_sc.shape, 1) == 0))
        km_sc[...] = jnp.where(is_pad, NEG_INF, 0.0).astype(km_sc.dtype)
        zero_left_context()
        y_sc[...] = mix_and_ffn(x0_ref[...], None, 0, 0)[0]

    pl.when(t == 0)(zero_left_context)

    mix_out, ffn_out = mix_and_ffn(x_ref[...], y_sc[...], t, row0)
    o_ref[...] = ffn_out
    y_sc[...] = mix_out


def _layer_spec(layer, shape):
    zeros = (0,) * len(shape)
    return pl.BlockSpec((None,) + shape, lambda g: (layer,) + zeros,
                        pipeline_mode=pl.Buffered(1))


def _layer(layer, x, vecs, w_in, rtab, wsp, bsp, w_out, w1, w2):
    B, S, D = x.shape
    tiles_per_row = S // TQ
    n_tiles = B * tiles_per_row
    bf16 = jnp.bfloat16
    f32 = jnp.float32

    def next_tile(g):
        gi = jnp.minimum(g + 1, n_tiles - 1)
        return (gi // tiles_per_row, gi % tiles_per_row, 0)

    def out_tile(g):
        return (g // tiles_per_row, g % tiles_per_row, 0)

    hbm = pl.BlockSpec(memory_space=pl.ANY)
    return pl.pallas_call(
        functools.partial(_layer_kernel, layer, tiles_per_row, n_tiles),
        grid=(n_tiles,),
        in_specs=[
            pl.BlockSpec((None, TQ, D), lambda g: (0, 0, 0), pipeline_mode=pl.Buffered(1)),
            pl.BlockSpec((None, TQ, D), next_tile),
            _layer_spec(layer, (GAIN_ROWS, D)),
            hbm,
            _layer_spec(layer, (ATT_HEADS, MXU_COLS)),
            _layer_spec(layer, (GMLP_GROUPS, GMLP_BLOCK, GMLP_BLOCK)),
            _layer_spec(layer, (GMLP_BLOCK, GMLP_WIDTH)),
            hbm,
            hbm,
            hbm,
        ],
        out_specs=pl.BlockSpec((None, TQ, D), out_tile),
        out_shape=jax.ShapeDtypeStruct(x.shape, x.dtype),
        scratch_shapes=[
            pltpu.VMEM((LEFT + S, ATT_WIDTH), bf16),
            pltpu.VMEM((PAD_TILES + tiles_per_row, ATT_WIDTH, TQ), bf16),
            pltpu.VMEM((LEFT + S, LANES), bf16),
            pltpu.VMEM((ATT_HEADS, TK, TQ), f32),
            pltpu.VMEM((TQ, D), f32),
            pltpu.VMEM((TQ, D_FF), bf16),
            pltpu.VMEM((ATT_WIDTH, TQ), f32),
            pltpu.VMEM((ATT_WIDTH, TQ), f32),
            pltpu.VMEM((2 * ATT_WIDTH, D), bf16),
            pltpu.VMEM((D, KUG_WIDTH), bf16),
            pltpu.VMEM((D, D), bf16),
            pltpu.VMEM((D, 2 * D_FF), bf16),
            pltpu.VMEM((D_FF, D), bf16),
            pltpu.VMEM((STAGE_SLOTS, STAGE_ROWS[ATT_WIDTH], ATT_WIDTH), f32),
            pltpu.VMEM((STAGE_SLOTS, STAGE_ROWS[D], D), f32),
            pltpu.VMEM((STAGE_SLOTS, STAGE_ROWS[2 * D_FF], 2 * D_FF), f32),
            pltpu.SemaphoreType.DMA((STAGE_SLOTS,)),
            pltpu.SemaphoreType.DMA((STAGE_SLOTS,)),
            pltpu.SemaphoreType.DMA((STAGE_SLOTS,)),
        ],
        compiler_params=pltpu.CompilerParams(
            dimension_semantics=("arbitrary",),
            vmem_limit_bytes=VMEM_LIMIT_BYTES),
        name="layer",
    )(x, x, vecs, w_in, rtab, wsp, bsp, w_out, w1, w2)


def _reversed_rel_table(rel_bias):
    near = rel_bias[:, :, ::-1][:, :, :NEAR]
    far = jnp.broadcast_to(rel_bias[:, :, 2 * MAX_REL:], near.shape[:2] + (MXU_COLS - NEAR,))
    return jnp.concatenate([near, far], axis=2).astype(jnp.float32)


def _pack_gains(mix_norm_g, ffn_norm_g, k_norm_g, sgu_norm_g, gmlp_out_norm_g, att_out_norm_g,
                q_norm_g):
    depth = mix_norm_g.shape[0]
    tiled = lambda g: jnp.tile(g, (1, ATT_HEADS))
    rows = [mix_norm_g, ffn_norm_g,
            jnp.concatenate([tiled(k_norm_g), sgu_norm_g], axis=1),
            jnp.concatenate([gmlp_out_norm_g, att_out_norm_g], axis=1),
            jnp.concatenate([tiled(q_norm_g), jnp.zeros((depth, D_MODEL - ATT_WIDTH))], axis=1)]
    rows += [jnp.zeros((depth, D_MODEL))] * (GAIN_ROWS - len(rows))
    return jnp.stack(rows, axis=1).astype(jnp.float32)


def kernel(x, mix_norm_g, w_in, q_norm_g, k_norm_g, rel_bias, sgu_norm_g, w_spatial, b_spatial,
           att_out_norm_g, gmlp_out_norm_g, w_out, ffn_norm_g, w_ffn_in, w_ffn_out):
    depth = w_in.shape[0]
    params = (
        _pack_gains(mix_norm_g, ffn_norm_g, k_norm_g, sgu_norm_g, gmlp_out_norm_g,
                    att_out_norm_g, q_norm_g),
        w_in, _reversed_rel_table(rel_bias), w_spatial,
        jnp.repeat(jnp.swapaxes(b_spatial, 1, 2), GMLP_GROUP_DIM, axis=2),
        w_out, w_ffn_in, w_ffn_out)
    for l in range(depth):
        x = _layer(l, x, *params)
    return x
```

```python
import functools

import jax
import jax.numpy as jnp
from jax import lax
from jax.experimental import pallas as pl
from jax.experimental.pallas import tpu as pltpu

D_MODEL = 1024
CHUNK = 64
ATT_HEADS = 8
HEAD_DIM = 64
ATT_WIDTH = ATT_HEADS * HEAD_DIM
LEFT_CHUNKS = 8
LEFT = LEFT_CHUNKS * CHUNK
MAX_REL = 2 * CHUNK
NEAR = MAX_REL + CHUNK
GMLP_WIDTH = 512
GMLP_GROUPS = 8
GMLP_GROUP_DIM = GMLP_WIDTH // GMLP_GROUPS
GMLP_BLOCK = 128
D_FF = 2816
KUG_WIDTH = ATT_WIDTH + 2 * GMLP_WIDTH
EPS = 1e-6
NEG_INF = -1e30

LANES = 128
BF16_ROWS = 16
MXU_COLS = 256
TQ = 256
TK = LEFT + TQ
PAD_TILES = LEFT // TQ
WIN_TILES = TK // TQ
FF_CHUNK = 256
HEADS_PER_ROUND = 2
STAGE_ROWS = {ATT_WIDTH: 128, D_MODEL: 128, 2 * D_FF: 32}
GAIN_ROWS = 8
STAGE_SLOTS = 4
VMEM_LIMIT_BYTES = 60 * 1024 * 1024


def _rms(x, g):
    ms = jnp.mean(x * x, axis=-1, keepdims=True)
    return (x * lax.rsqrt(ms + EPS)) * g


def _head_rms(z, gain):
    rows = z.shape[0]
    lo = lax.broadcasted_iota(jnp.int32, (rows, LANES), 1) < HEAD_DIM
    outs = []
    for c in range(ATT_WIDTH // LANES):
        zc = z[:, c * LANES:(c + 1) * LANES]
        sq = zc * zc
        ms_lo = jnp.sum(jnp.where(lo, sq, 0.0), axis=-1, keepdims=True) * (1.0 / HEAD_DIM)
        ms_hi = jnp.sum(jnp.where(lo, 0.0, sq), axis=-1, keepdims=True) * (1.0 / HEAD_DIM)
        r = jnp.where(lo, lax.rsqrt(ms_lo + EPS), lax.rsqrt(ms_hi + EPS))
        outs.append((zc * r) * gain[:, c * LANES:(c + 1) * LANES])
    return jnp.concatenate(outs, axis=-1)


def _build_bias(rtab_ref, bias_sc, heads):
    f32 = jnp.float32
    row = lax.broadcasted_iota(jnp.int32, (CHUNK, MXU_COLS), 0)
    lo_half = lax.broadcasted_iota(jnp.int32, (CHUNK, LANES), 1) < CHUNK
    n_kc = TK // CHUNK
    for head in heads:
        r = rtab_ref[head:head + 1, :]
        far = rtab_ref[head:head + 1, MXU_COLS - 1:MXU_COLS]
        x0 = jnp.broadcast_to(r, (CHUNK, MXU_COLS))
        for b in range(CHUNK.bit_length() - 1):
            x0 = jnp.where(((row >> b) & 1) == 1, pltpu.roll(x0, 1 << b, axis=1), x0)
        x1 = pltpu.roll(x0, CHUNK, axis=1)
        far_piece = jnp.broadcast_to(far, (CHUNK, LANES))
        neg_piece = jnp.full((CHUNK, LANES), NEG_INF, f32)

        def half(qc, kc):
            rel = kc - qc
            if rel < 0 or rel > LEFT_CHUNKS:
                return neg_piece
            if rel < LEFT_CHUNKS - 2:
                return far_piece
            col = (rel - (LEFT_CHUNKS - 2)) * CHUNK + (qc % 2) * CHUNK
            src = x1 if qc % 2 else x0
            return src[:, (col // LANES) * LANES:(col // LANES + 1) * LANES]

        row_blocks = []
        for qc in range(TQ // CHUNK):
            pieces = [jnp.where(lo_half, half(qc, 2 * vcol), half(qc, 2 * vcol + 1))
                      for vcol in range(n_kc // 2)]
            row_blocks.append(jnp.concatenate(pieces, axis=1))
        bias_sc[head] = jnp.concatenate(row_blocks, axis=0).T


def _step(x, y_prev, t, row0, mixg_ref, w_qv_ref, w_kug_ref, qg_ref, kg_ref, sgug_ref, wsp_ref,
          bsp_ref, ag_ref, gg_ref, w_out_ref, ffng_ref, w1_ref, w2_ref,
          k_sc, v_sc, km_sc, bias_sc, act_sc):
    bf16 = jnp.bfloat16
    f32 = jnp.float32
    nt = (((1,), (1,)), ((), ()))
    do_ffn = y_prev is not None
    h = _rms(x, mixg_ref[...]).astype(bf16)
    if do_ffn:
        hf = _rms(y_prev, ffng_ref[...]).astype(bf16)

    def ffn_chunk(j):
        if not do_ffn:
            return
        gate = jnp.dot(hf, w1_ref[:, j * FF_CHUNK:(j + 1) * FF_CHUNK], preferred_element_type=f32)
        up = jnp.dot(hf, w1_ref[:, D_FF + j * FF_CHUNK:D_FF + (j + 1) * FF_CHUNK],
                     preferred_element_type=f32)
        act_sc[:, j * FF_CHUNK:(j + 1) * FF_CHUNK] = (jax.nn.silu(gate) * up).astype(bf16)

    qv_t = lax.dot_general(w_qv_ref[...], h, nt, preferred_element_type=f32)
    k = jnp.dot(h, w_kug_ref[:, 0:ATT_WIDTH], preferred_element_type=f32)
    u = jnp.dot(h, w_kug_ref[:, ATT_WIDTH:ATT_WIDTH + GMLP_WIDTH], preferred_element_type=f32)
    vg = jnp.dot(h, w_kug_ref[:, ATT_WIDTH + GMLP_WIDTH:KUG_WIDTH], preferred_element_type=f32)

    q3 = qv_t[0:ATT_WIDTH].reshape(ATT_HEADS, HEAD_DIM, TQ)
    q_ms = jnp.mean(q3 * q3, axis=1, keepdims=True)
    q3 = (q3 * lax.rsqrt(q_ms + EPS)) * qg_ref[...].reshape(ATT_HEADS, HEAD_DIM, TQ)
    qn_t = (q3 * HEAD_DIM ** -0.5).reshape(ATT_WIDTH, TQ).astype(bf16)
    kn = _head_rms(k, kg_ref[...]).astype(bf16)
    k_sc[pl.ds(row0 + LEFT, TQ), :] = kn
    v_sc[t + PAD_TILES] = qv_t[ATT_WIDTH:2 * ATT_WIDTH].astype(bf16)
    kw = k_sc[pl.ds(row0, TK), :]
    km = km_sc[pl.ds(row0, TK), :]
    pad_rows = jnp.where(lax.broadcasted_iota(jnp.int32, (LANES, TQ), 0) == 0,
                         1.0, 0.0).astype(bf16)
    zero_rows = jnp.zeros((HEAD_DIM, TQ), bf16)
    ones_rows = jnp.ones((BF16_ROWS, TK), bf16)

    def scores(head):
        c = head // 2
        qh = qn_t[head * HEAD_DIM:(head + 1) * HEAD_DIM]
        pair = [qh, zero_rows] if head % 2 == 0 else [zero_rows, qh]
        rhs = jnp.concatenate(pair + [pad_rows], axis=0)
        lhs = jnp.concatenate([kw[:, c * LANES:(c + 1) * LANES], km], axis=1)
        s = jnp.dot(lhs, rhs, preferred_element_type=f32)
        return s + bias_sc[head]

    def probs(s):
        m = jnp.max(s, axis=0, keepdims=True)
        return jnp.exp(s - m).astype(bf16)

    def weighted_values(head, p):
        v_t = jnp.concatenate(
            [v_sc[t + w, head * HEAD_DIM:(head + 1) * HEAD_DIM, :] for w in range(WIN_TILES)],
            axis=1)
        pv = jnp.dot(jnp.concatenate([v_t, ones_rows], axis=0), p,
                     preferred_element_type=f32)
        return pv[0:HEAD_DIM] * (1.0 / pv[HEAD_DIM:HEAD_DIM + 1])

    rounds = [list(range(r, r + HEADS_PER_ROUND)) for r in range(0, ATT_HEADS, HEADS_PER_ROUND)]
    outs = []
    next_chunk = 0
    s_next = [scores(hd) for hd in rounds[0]]
    for ri, heads in enumerate(rounds):
        s_cur = s_next
        if ri + 1 < len(rounds):
            s_next = [scores(hd) for hd in rounds[ri + 1]]
        for _ in heads:
            ffn_chunk(next_chunk)
            next_chunk += 1
        ps = [probs(s) for s in s_cur]
        outs += [weighted_values(hd, p) for hd, p in zip(heads, ps)]
    a_t = jnp.concatenate(outs, axis=0)
    a_ms = jnp.mean(a_t * a_t, axis=0, keepdims=True)
    a_n = ((a_t * lax.rsqrt(a_ms + EPS)) * ag_ref[...]).T.astype(bf16)

    u_act = jax.nn.gelu(u)
    vgn = _rms(jax.nn.gelu(vg), sgug_ref[...]).astype(bf16)
    ti = lax.broadcasted_iota(jnp.int32, (GMLP_GROUPS, GMLP_BLOCK, GMLP_BLOCK), 1)
    si = lax.broadcasted_iota(jnp.int32, (GMLP_GROUPS, GMLP_BLOCK, GMLP_BLOCK), 2)
    causal = (ti // CHUNK) >= (si // CHUNK)
    wsp = jnp.where(causal, wsp_ref[...], 0.0).astype(bf16)
    wsp = wsp.reshape(GMLP_GROUPS * GMLP_BLOCK, GMLP_BLOCK)
    first_of_pair = lax.broadcasted_iota(jnp.int32, (GMLP_BLOCK, LANES), 1) < GMLP_GROUP_DIM
    n_blk = TQ // GMLP_BLOCK
    pair_cols = []
    for pr in range(GMLP_WIDTH // LANES):
        lhs = wsp[2 * pr * GMLP_BLOCK:2 * (pr + 1) * GMLP_BLOCK, :]
        rhs = jnp.concatenate(
            [vgn[blk * GMLP_BLOCK:(blk + 1) * GMLP_BLOCK, pr * LANES:(pr + 1) * LANES]
             for blk in range(n_blk)], axis=1)
        res = jnp.dot(lhs, rhs, preferred_element_type=f32)
        pair_cols.append(jnp.concatenate(
            [jnp.where(first_of_pair,
                       res[0:GMLP_BLOCK, blk * LANES:(blk + 1) * LANES],
                       res[GMLP_BLOCK:2 * GMLP_BLOCK, blk * LANES:(blk + 1) * LANES])
             for blk in range(n_blk)], axis=0))
        if next_chunk < D_FF // FF_CHUNK:
            ffn_chunk(next_chunk)
            next_chunk += 1
    while next_chunk < D_FF // FF_CHUNK:
        ffn_chunk(next_chunk)
        next_chunk += 1
    mixed = (jnp.concatenate(pair_cols, axis=1)
             + jnp.concatenate([bsp_ref[...]] * n_blk, axis=0))
    g_n = _rms(u_act * mixed, gg_ref[...]).astype(bf16)

    ffn_out = None
    if do_ffn:
        ffn_out = y_prev + jnp.dot(act_sc[...], w2_ref[...], preferred_element_type=f32)
    mix = jnp.concatenate([a_n, g_n], axis=-1)
    mix_out = x + jnp.dot(mix, w_out_ref[...], preferred_element_type=f32)
    return mix_out, ffn_out


def _weight_loader(src, dst, ring, src_col=0, dst_col=0, dst_row=None):
    stage, sem = ring
    rows = src.shape[0]
    _, chunk, cols = stage.shape
    assert rows % chunk == 0 and chunk == STAGE_ROWS[cols] and stage.shape[0] == STAGE_SLOTS
    n = rows // chunk
    ahead = STAGE_SLOTS - 1
    assert n >= ahead

    def copy(i, slot):
        return pltpu.make_async_copy(src.at[pl.ds(i * chunk, chunk), pl.ds(src_col, cols)],
                                     stage.at[slot], sem.at[slot])

    def prefetch():
        for i in range(ahead):
            copy(i, i).start()

    def drain_natural():
        def body(i, carry):
            slot = lax.rem(i, STAGE_SLOTS)

            @pl.when(i + ahead < n)
            def _():
                copy(i + ahead, lax.rem(i + ahead, STAGE_SLOTS)).start()

            copy(i, slot).wait()
            dst[pl.ds(pl.multiple_of(i * chunk, chunk), chunk), dst_col:dst_col + cols] = (
                stage[slot].astype(dst.dtype))
            return carry

        lax.fori_loop(0, n, body, 0)

    def drain_transposed():
        assert chunk % LANES == 0
        for i in range(n):
            if i + ahead < n:
                copy(i + ahead, (i + ahead) % STAGE_SLOTS).start()
            copy(i, i % STAGE_SLOTS).wait()
            dst[dst_row:dst_row + cols, i * chunk:(i + 1) * chunk] = (
                stage[i % STAGE_SLOTS].T.astype(dst.dtype))

    return prefetch, (drain_natural if dst_row is None else drain_transposed)


def _layer_kernel(layer, tiles_per_row, n_tiles,
                  x0_ref, x_ref, vecs_ref, w_in_hbm, rtab_ref, wsp_ref, bsp_ref,
                  w_out_hbm, w1_hbm, w2_hbm,
                  o_ref, k_sc, v_sc, km_sc, bias_sc, y_sc, act_sc, qg_sc, ag_sc,
                  w_qv_sc, w_kug_sc, w_out_sc, w1_sc, w2_sc, st_k, st_d, st_ff,
                  sem_k, sem_d, sem_ff):
    g = pl.program_id(0)
    mixg_ref = vecs_ref.at[0:1, :]
    ffng_ref = vecs_ref.at[1:2, :]
    kg_ref = vecs_ref.at[2:3, 0:ATT_WIDTH]
    sgug_ref = vecs_ref.at[2:3, ATT_WIDTH:ATT_WIDTH + GMLP_WIDTH]
    gg_ref = vecs_ref.at[3:4, 0:GMLP_WIDTH]
    t = lax.rem(jnp.minimum(g + 1, n_tiles - 1), tiles_per_row)
    row0 = pl.multiple_of(t * TQ, TQ)

    def mix_and_ffn(x, y_prev, t_x, row0_x):
        return _step(x, y_prev, t_x, row0_x, mixg_ref, w_qv_sc, w_kug_sc, qg_sc, kg_ref, sgug_ref,
                     wsp_ref, bsp_ref, ag_sc, gg_ref, w_out_sc, ffng_ref, w1_sc, w2_sc,
                     k_sc, v_sc, km_sc, bias_sc, act_sc)

    def zero_left_context():
        k_sc[0:LEFT, :] = jnp.zeros((LEFT, ATT_WIDTH), k_sc.dtype)
        v_sc[0:PAD_TILES] = jnp.zeros((PAD_TILES, ATT_WIDTH, TQ), v_sc.dtype)

    @pl.when(g == 0)
    def _():
        w_in_l = w_in_hbm.at[layer]
        ring_k, ring_d, ring_ff = (st_k, sem_k), (st_d, sem_d), (st_ff, sem_ff)
        loads = [
            _weight_loader(w_in_l, w_qv_sc, ring_k, src_col=0, dst_row=0),
            _weight_loader(w_in_l, w_kug_sc, ring_d, src_col=3 * ATT_WIDTH, dst_col=ATT_WIDTH),
            _weight_loader(w_in_l, w_qv_sc, ring_k, src_col=2 * ATT_WIDTH, dst_row=ATT_WIDTH),
            _weight_loader(w_out_hbm.at[layer], w_out_sc, ring_d),
            _weight_loader(w_in_l, w_kug_sc, ring_k, src_col=ATT_WIDTH, dst_col=0),
            _weight_loader(w1_hbm.at[layer], w1_sc, ring_ff),
            _weight_loader(w2_hbm.at[layer], w2_sc, ring_d),
        ]
        loads[0][0]()
        for i, (_, drain) in enumerate(loads):
            if i + 1 < len(loads):
                loads[i + 1][0]()
            _build_bias(rtab_ref, bias_sc, [i])
            drain()
        _build_bias(rtab_ref, bias_sc, range(len(loads), ATT_HEADS))
        qg_sc[...] = jnp.broadcast_to(vecs_ref[4:5, 0:ATT_WIDTH], (TQ, ATT_WIDTH)).T
        ag_sc[...] = jnp.broadcast_to(vecs_ref[3:4, GMLP_WIDTH:GMLP_WIDTH + ATT_WIDTH],
                                      (TQ, ATT_WIDTH)).T
        is_pad = ((lax.broadcasted_iota(jnp.int32, km_sc.shape, 0) < LEFT)
                  & (lax.broadcasted_iota(jnp.int32, km_sc.shape, 1) == 0))
        km_sc[...] = jnp.where(is_pad, NEG_INF, 0.0).astype(km_sc.dtype)
        zero_left_context()
        y_sc[...] = mix_and_ffn(x0_ref[...], None, 0, 0)[0]

    pl.when(t == 0)(zero_left_context)

    mix_out, ffn_out = mix_and_ffn(x_ref[...], y_sc[...], t, row0)
    o_ref[...] = ffn_out
    y_sc[...] = mix_out


def _layer_spec(layer, shape):
    zeros = (0,) * len(shape)
    return pl.BlockSpec((None,) + shape, lambda g: (layer,) + zeros,
                        pipeline_mode=pl.Buffered(1))


def _layer(layer, x, vecs, w_in, rtab, wsp, bsp, w_out, w1, w2):
    B, S, D = x.shape
    tiles_per_row = S // TQ
    n_tiles = B * tiles_per_row
    bf16 = jnp.bfloat16
    f32 = jnp.float32

    def next_tile(g):
        gi = jnp.minimum(g + 1, n_tiles - 1)
        return (gi // tiles_per_row, gi % tiles_per_row, 0)

    def out_tile(g):
        return (g // tiles_per_row, g % tiles_per_row, 0)

    hbm = pl.BlockSpec(memory_space=pl.ANY)
    return pl.pallas_call(
        functools.partial(_layer_kernel, layer, tiles_per_row, n_tiles),
        grid=(n_tiles,),
        in_specs=[
            pl.BlockSpec((None, TQ, D), lambda g: (0, 0, 0), pipeline_mode=pl.Buffered(1)),
            pl.BlockSpec((None, TQ, D), next_tile),
            _layer_spec(layer, (GAIN_ROWS, D)),
            hbm,
            _layer_spec(layer, (ATT_HEADS, MXU_COLS)),
            _layer_spec(layer, (GMLP_GROUPS, GMLP_BLOCK, GMLP_BLOCK)),
            _layer_spec(layer, (GMLP_BLOCK, GMLP_WIDTH)),
            hbm,
            hbm,
            hbm,
        ],
        out_specs=pl.BlockSpec((None, TQ, D), out_tile),
        out_shape=jax.ShapeDtypeStruct(x.shape, x.dtype),
        scratch_shapes=[
            pltpu.VMEM((LEFT + S, ATT_WIDTH), bf16),
            pltpu.VMEM((PAD_TILES + tiles_per_row, ATT_WIDTH, TQ), bf16),
            pltpu.VMEM((LEFT + S, LANES), bf16),
            pltpu.VMEM((ATT_HEADS, TK, TQ), f32),
            pltpu.VMEM((TQ, D), f32),
            pltpu.VMEM((TQ, D_FF), bf16),
            pltpu.VMEM((ATT_WIDTH, TQ), f32),
            pltpu.VMEM((ATT_WIDTH, TQ), f32),
            pltpu.VMEM((2 * ATT_WIDTH, D), bf16),
            pltpu.VMEM((D, KUG_WIDTH), bf16),
            pltpu.VMEM((D, D), bf16),
            pltpu.VMEM((D, 2 * D_FF), bf16),
            pltpu.VMEM((D_FF, D), bf16),
            pltpu.VMEM((STAGE_SLOTS, STAGE_ROWS[ATT_WIDTH], ATT_WIDTH), f32),
            pltpu.VMEM((STAGE_SLOTS, STAGE_ROWS[D], D), f32),
            pltpu.VMEM((STAGE_SLOTS, STAGE_ROWS[2 * D_FF], 2 * D_FF), f32),
            pltpu.SemaphoreType.DMA((STAGE_SLOTS,)),
            pltpu.SemaphoreType.DMA((STAGE_SLOTS,)),
            pltpu.SemaphoreType.DMA((STAGE_SLOTS,)),
        ],
        compiler_params=pltpu.CompilerParams(
            dimension_semantics=("arbitrary",),
            vmem_limit_bytes=VMEM_LIMIT_BYTES),
        name="layer",
    )(x, x, vecs, w_in, rtab, wsp, bsp, w_out, w1, w2)


def _reversed_rel_table(rel_bias):
    near = rel_bias[:, :, ::-1][:, :, :NEAR]
    far = jnp.broadcast_to(rel_bias[:, :, 2 * MAX_REL:], near.shape[:2] + (MXU_COLS - NEAR,))
    return jnp.concatenate([near, far], axis=2).astype(jnp.float32)


def _pack_gains(mix_norm_g, ffn_norm_g, k_norm_g, sgu_norm_g, gmlp_out_norm_g, att_out_norm_g,
                q_norm_g):
    depth = mix_norm_g.shape[0]
    tiled = lambda g: jnp.tile(g, (1, ATT_HEADS))
    rows = [mix_norm_g, ffn_norm_g,
            jnp.concatenate([tiled(k_norm_g), sgu_norm_g], axis=1),
            jnp.concatenate([gmlp_out_norm_g, att_out_norm_g], axis=1),
            jnp.concatenate([tiled(q_norm_g), jnp.zeros((depth, D_MODEL - ATT_WIDTH))], axis=1)]
    rows += [jnp.zeros((depth, D_MODEL))] * (GAIN_ROWS - len(rows))
    return jnp.stack(rows, axis=1).astype(jnp.float32)


def kernel(x, mix_norm_g, w_in, q_norm_g, k_norm_g, rel_bias, sgu_norm_g, w_spatial, b_spatial,
           att_out_norm_g, gmlp_out_norm_g, w_out, ffn_norm_g, w_ffn_in, w_ffn_out):
    depth = w_in.shape[0]
    params = (
        _pack_gains(mix_norm_g, ffn_norm_g, k_norm_g, sgu_norm_g, gmlp_out_norm_g,
                    att_out_norm_g, q_norm_g),
        w_in, _reversed_rel_table(rel_bias), w_spatial,
        jnp.repeat(jnp.swapaxes(b_spatial, 1, 2), GMLP_GROUP_DIM, axis=2),
        w_out, w_ffn_in, w_ffn_out)
    for l in range(depth):
        x = _layer(l, x, *params)
    return x
```

```python
import functools

import jax
import jax.numpy as jnp
from jax import lax
from jax.experimental import pallas as pl
from jax.experimental.pallas import tpu as pltpu

D_MODEL = 1024
CHUNK = 64
ATT_HEADS = 8
HEAD_DIM = 64
ATT_WIDTH = ATT_HEADS * HEAD_DIM
LEFT_CHUNKS = 8
LEFT = LEFT_CHUNKS * CHUNK
MAX_REL = 2 * CHUNK
NEAR = MAX_REL + CHUNK
GMLP_WIDTH = 512
GMLP_GROUPS = 8
GMLP_GROUP_DIM = GMLP_WIDTH // GMLP_GROUPS
GMLP_BLOCK = 128
D_FF = 2816
KUG_WIDTH = ATT_WIDTH + 2 * GMLP_WIDTH
EPS = 1e-6
NEG_INF = -1e30

LANES = 128
BF16_ROWS = 16
MXU_COLS = 256
TQ = 256
TK = LEFT + TQ
PAD_TILES = LEFT // TQ
WIN_TILES = TK // TQ
FF_CHUNK = 256
HEADS_PER_ROUND = 2
STAGE_ROWS = {ATT_WIDTH: 128, D_MODEL: 128, 2 * D_FF: 32}
GAIN_ROWS = 8
STAGE_SLOTS = 4
VMEM_LIMIT_BYTES = 60 * 1024 * 1024


def _rms(x, g):
    ms = jnp.mean(x * x, axis=-1, keepdims=True)
    return (x * lax.rsqrt(ms + EPS)) * g


def _head_rms(z, gain):
    rows = z.shape[0]
    lo = lax.broadcasted_iota(jnp.int32, (rows, LANES), 1) < HEAD_DIM
    outs = []
    for c in range(ATT_WIDTH // LANES):
        zc = z[:, c * LANES:(c + 1) * LANES]
        sq = zc * zc
        ms_lo = jnp.sum(jnp.where(lo, sq, 0.0), axis=-1, keepdims=True) * (1.0 / HEAD_DIM)
        ms_hi = jnp.sum(jnp.where(lo, 0.0, sq), axis=-1, keepdims=True) * (1.0 / HEAD_DIM)
        r = jnp.where(lo, lax.rsqrt(ms_lo + EPS), lax.rsqrt(ms_hi + EPS))
        outs.append((zc * r) * gain[:, c * LANES:(c + 1) * LANES])
    return jnp.concatenate(outs, axis=-1)


def _build_bias(rtab_ref, bias_sc):
    f32 = jnp.float32
    row = lax.broadcasted_iota(jnp.int32, (CHUNK, MXU_COLS), 0)
    lo_half = lax.broadcasted_iota(jnp.int32, (CHUNK, LANES), 1) < CHUNK
    n_kc = TK // CHUNK
    for head in range(ATT_HEADS):
        r = rtab_ref[head:head + 1, :]
        far = rtab_ref[head:head + 1, MXU_COLS - 1:MXU_COLS]
        x0 = jnp.broadcast_to(r, (CHUNK, MXU_COLS))
        for b in range(CHUNK.bit_length() - 1):
            x0 = jnp.where(((row >> b) & 1) == 1, pltpu.roll(x0, 1 << b, axis=1), x0)
        x1 = pltpu.roll(x0, CHUNK, axis=1)
        far_piece = jnp.broadcast_to(far, (CHUNK, LANES))
        neg_piece = jnp.full((CHUNK, LANES), NEG_INF, f32)

        def half(qc, kc):
            rel = kc - qc
            if rel < 0 or rel > LEFT_CHUNKS:
                return neg_piece
            if rel < LEFT_CHUNKS - 2:
                return far_piece
            col = (rel - (LEFT_CHUNKS - 2)) * CHUNK + (qc % 2) * CHUNK
            src = x1 if qc % 2 else x0
            return src[:, (col // LANES) * LANES:(col // LANES + 1) * LANES]

        row_blocks = []
        for qc in range(TQ // CHUNK):
            pieces = [jnp.where(lo_half, half(qc, 2 * vcol), half(qc, 2 * vcol + 1))
                      for vcol in range(n_kc // 2)]
            row_blocks.append(jnp.concatenate(pieces, axis=1))
        bias_sc[head] = jnp.concatenate(row_blocks, axis=0).T


def _step(x, y_prev, t, row0, mixg_ref, w_qv_ref, w_kug_ref, qg_ref, kg_ref, sgug_ref, wsp_ref,
          bsp_ref, ag_ref, gg_ref, w_out_ref, ffng_ref, w1_ref, w2_ref,
          k_sc, v_sc, km_sc, bias_sc, act_sc):
    bf16 = jnp.bfloat16
    f32 = jnp.float32
    nt = (((1,), (1,)), ((), ()))
    do_ffn = y_prev is not None
    if do_ffn:
        hf = _rms(y_prev, ffng_ref[...]).astype(bf16)

    def ffn_chunk(j):
        if not do_ffn:
            return
        gate = jnp.dot(hf, w1_ref[:, j * FF_CHUNK:(j + 1) * FF_CHUNK], preferred_element_type=f32)
        up = jnp.dot(hf, w1_ref[:, D_FF + j * FF_CHUNK:D_FF + (j + 1) * FF_CHUNK],
                     preferred_element_type=f32)
        act_sc[:, j * FF_CHUNK:(j + 1) * FF_CHUNK] = (jax.nn.silu(gate) * up).astype(bf16)

    if x is None:
        for j in range(D_FF // FF_CHUNK):
            ffn_chunk(j)
        return None, y_prev + jnp.dot(act_sc[...], w2_ref[...], preferred_element_type=f32)

    h = _rms(x, mixg_ref[...]).astype(bf16)

    qv_t = lax.dot_general(w_qv_ref[...], h, nt, preferred_element_type=f32)
    k = jnp.dot(h, w_kug_ref[:, 0:ATT_WIDTH], preferred_element_type=f32)
    u = jnp.dot(h, w_kug_ref[:, ATT_WIDTH:ATT_WIDTH + GMLP_WIDTH], preferred_element_type=f32)
    vg = jnp.dot(h, w_kug_ref[:, ATT_WIDTH + GMLP_WIDTH:KUG_WIDTH], preferred_element_type=f32)

    q3 = qv_t[0:ATT_WIDTH].reshape(ATT_HEADS, HEAD_DIM, TQ)
    q_ms = jnp.mean(q3 * q3, axis=1, keepdims=True)
    q3 = (q3 * lax.rsqrt(q_ms + EPS)) * qg_ref[...].reshape(ATT_HEADS, HEAD_DIM, TQ)
    qn_t = (q3 * HEAD_DIM ** -0.5).reshape(ATT_WIDTH, TQ).astype(bf16)
    kn = _head_rms(k, kg_ref[...]).astype(bf16)
    k_sc[pl.ds(row0 + LEFT, TQ), :] = kn
    v_sc[t + PAD_TILES] = qv_t[ATT_WIDTH:2 * ATT_WIDTH].astype(bf16)
    kw = k_sc[pl.ds(row0, TK), :]
    km = km_sc[pl.ds(row0, TK), :]
    pad_rows = jnp.where(lax.broadcasted_iota(jnp.int32, (LANES, TQ), 0) == 0,
                         1.0, 0.0).astype(bf16)
    zero_rows = jnp.zeros((HEAD_DIM, TQ), bf16)
    ones_rows = jnp.ones((BF16_ROWS, TK), bf16)

    def scores(head):
        c = head // 2
        qh = qn_t[head * HEAD_DIM:(head + 1) * HEAD_DIM]
        pair = [qh, zero_rows] if head % 2 == 0 else [zero_rows, qh]
        rhs = jnp.concatenate(pair + [pad_rows], axis=0)
        lhs = jnp.concatenate([kw[:, c * LANES:(c + 1) * LANES], km], axis=1)
        s = jnp.dot(lhs, rhs, preferred_element_type=f32)
        return s + bias_sc[head]

    def probs(s):
        m = jnp.max(s, axis=0, keepdims=True)
        return jnp.exp(s - m).astype(bf16)

    def weighted_values(head, p):
        v_t = jnp.concatenate(
            [v_sc[t + w, head * HEAD_DIM:(head + 1) * HEAD_DIM, :] for w in range(WIN_TILES)],
            axis=1)
        pv = jnp.dot(jnp.concatenate([v_t, ones_rows], axis=0), p,
                     preferred_element_type=f32)
        return pv[0:HEAD_DIM] * (1.0 / pv[HEAD_DIM:HEAD_DIM + 1])

    rounds = [list(range(r, r + HEADS_PER_ROUND)) for r in range(0, ATT_HEADS, HEADS_PER_ROUND)]
    outs = []
    next_chunk = 0
    s_next = [scores(hd) for hd in rounds[0]]
    for ri, heads in enumerate(rounds):
        s_cur = s_next
        if ri + 1 < len(rounds):
            s_next = [scores(hd) for hd in rounds[ri + 1]]
        for _ in heads:
            ffn_chunk(next_chunk)
            next_chunk += 1
        ps = [probs(s) for s in s_cur]
        outs += [weighted_values(hd, p) for hd, p in zip(heads, ps)]
    a_t = jnp.concatenate(outs, axis=0)
    a_ms = jnp.mean(a_t * a_t, axis=0, keepdims=True)
    a_n = ((a_t * lax.rsqrt(a_ms + EPS)) * ag_ref[...]).T.astype(bf16)

    u_act = jax.nn.gelu(u)
    vgn = _rms(jax.nn.gelu(vg), sgug_ref[...]).astype(bf16)
    ti = lax.broadcasted_iota(jnp.int32, (GMLP_GROUPS, GMLP_BLOCK, GMLP_BLOCK), 1)
    si = lax.broadcasted_iota(jnp.int32, (GMLP_GROUPS, GMLP_BLOCK, GMLP_BLOCK), 2)
    causal = (ti // CHUNK) >= (si // CHUNK)
    wsp = jnp.where(causal, wsp_ref[...], 0.0).astype(bf16)
    wsp = wsp.reshape(GMLP_GROUPS * GMLP_BLOCK, GMLP_BLOCK)
    first_of_pair = lax.broadcasted_iota(jnp.int32, (GMLP_BLOCK, LANES), 1) < GMLP_GROUP_DIM
    n_blk = TQ // GMLP_BLOCK
    pair_cols = []
    for pr in range(GMLP_WIDTH // LANES):
        lhs = wsp[2 * pr * GMLP_BLOCK:2 * (pr + 1) * GMLP_BLOCK, :]
        rhs = jnp.concatenate(
            [vgn[blk * GMLP_BLOCK:(blk + 1) * GMLP_BLOCK, pr * LANES:(pr + 1) * LANES]
             for blk in range(n_blk)], axis=1)
        res = jnp.dot(lhs, rhs, preferred_element_type=f32)
        pair_cols.append(jnp.concatenate(
            [jnp.where(first_of_pair,
                       res[0:GMLP_BLOCK, blk * LANES:(blk + 1) * LANES],
                       res[GMLP_BLOCK:2 * GMLP_BLOCK, blk * LANES:(blk + 1) * LANES])
             for blk in range(n_blk)], axis=0))
        if next_chunk < D_FF // FF_CHUNK:
            ffn_chunk(next_chunk)
            next_chunk += 1
    while next_chunk < D_FF // FF_CHUNK:
        ffn_chunk(next_chunk)
        next_chunk += 1
    mixed = (jnp.concatenate(pair_cols, axis=1)
             + jnp.concatenate([bsp_ref[...]] * n_blk, axis=0))
    g_n = _rms(u_act * mixed, gg_ref[...]).astype(bf16)

    ffn_out = None
    if do_ffn:
        ffn_out = y_prev + jnp.dot(act_sc[...], w2_ref[...], preferred_element_type=f32)
    mix = jnp.concatenate([a_n, g_n], axis=-1)
    mix_out = x + jnp.dot(mix, w_out_ref[...], preferred_element_type=f32)
    return mix_out, ffn_out


def _weight_loader(src, dst, ring, src_col=0, dst_col=0, dst_row=None):
    stage, sem = ring
    rows = src.shape[0]
    _, chunk, cols = stage.shape
    assert rows % chunk == 0 and chunk == STAGE_ROWS[cols] and stage.shape[0] == STAGE_SLOTS
    n = rows // chunk
    ahead = STAGE_SLOTS - 1
    assert n >= ahead

    def copy(i, slot):
        return pltpu.make_async_copy(src.at[pl.ds(i * chunk, chunk), pl.ds(src_col, cols)],
                                     stage.at[slot], sem.at[slot])

    def prefetch():
        for i in range(ahead):
            copy(i, i).start()

    def drain_natural():
        def body(i, carry):
            slot = lax.rem(i, STAGE_SLOTS)

            @pl.when(i + ahead < n)
            def _():
                copy(i + ahead, lax.rem(i + ahead, STAGE_SLOTS)).start()

            copy(i, slot).wait()
            dst[pl.ds(pl.multiple_of(i * chunk, chunk), chunk), dst_col:dst_col + cols] = (
                stage[slot].astype(dst.dtype))
            return carry

        lax.fori_loop(0, n, body, 0)

    def drain_transposed():
        assert chunk % LANES == 0
        for i in range(n):
            if i + ahead < n:
                copy(i + ahead, (i + ahead) % STAGE_SLOTS).start()
            copy(i, i % STAGE_SLOTS).wait()
            dst[dst_row:dst_row + cols, i * chunk:(i + 1) * chunk] = (
                stage[i % STAGE_SLOTS].T.astype(dst.dtype))

    return prefetch, (drain_natural if dst_row is None else drain_transposed)


def _layer_kernel(layer, tiles_per_row, n_tiles,
                  x0_ref, x_ref, vecs_ref, w_in_hbm, rtab_ref, wsp_ref, bsp_ref,
                  w_out_hbm, w1_hbm, w2_hbm,
                  o_ref, k_sc, v_sc, km_sc, bias_sc, y_sc, act_sc, qg_sc, ag_sc,
                  w_qv_sc, w_kug_sc, w_out_sc, w1_sc, w2_sc, st_k, st_d, st_ff,
                  sem_k, sem_d, sem_ff):
    g = pl.program_id(0)
    mixg_ref = vecs_ref.at[0:1, :]
    ffng_ref = vecs_ref.at[1:2, :]
    kg_ref = vecs_ref.at[2:3, 0:ATT_WIDTH]
    sgug_ref = vecs_ref.at[2:3, ATT_WIDTH:ATT_WIDTH + GMLP_WIDTH]
    gg_ref = vecs_ref.at[3:4, 0:GMLP_WIDTH]
    t = lax.rem(jnp.minimum(g + 1, n_tiles - 1), tiles_per_row)
    row0 = pl.multiple_of(t * TQ, TQ)

    def mix_and_ffn(x, y_prev, t_x, row0_x):
        return _step(x, y_prev, t_x, row0_x, mixg_ref, w_qv_sc, w_kug_sc, qg_sc, kg_ref, sgug_ref,
                     wsp_ref, bsp_ref, ag_sc, gg_ref, w_out_sc, ffng_ref, w1_sc, w2_sc,
                     k_sc, v_sc, km_sc, bias_sc, act_sc)

    def zero_left_context():
        k_sc[0:LEFT, :] = jnp.zeros((LEFT, ATT_WIDTH), k_sc.dtype)
        v_sc[0:PAD_TILES] = jnp.zeros((PAD_TILES, ATT_WIDTH, TQ), v_sc.dtype)

    @pl.when(g == 0)
    def _():
        w_in_l = w_in_hbm.at[layer]
        ring_k, ring_d, ring_ff = (st_k, sem_k), (st_d, sem_d), (st_ff, sem_ff)
        loads = [
            _weight_loader(w_in_l, w_qv_sc, ring_k, src_col=0, dst_row=0),
            _weight_loader(w_in_l, w_kug_sc, ring_d, src_col=3 * ATT_WIDTH, dst_col=ATT_WIDTH),
            _weight_loader(w_in_l, w_qv_sc, ring_k, src_col=2 * ATT_WIDTH, dst_row=ATT_WIDTH),
            _weight_loader(w_out_hbm.at[layer], w_out_sc, ring_d),
            _weight_loader(w_in_l, w_kug_sc, ring_k, src_col=ATT_WIDTH, dst_col=0),
            _weight_loader(w1_hbm.at[layer], w1_sc, ring_ff),
            _weight_loader(w2_hbm.at[layer], w2_sc, ring_d),
        ]
        loads[0][0]()
        for i, (_, drain) in enumerate(loads):
            if i + 1 < len(loads):
                loads[i + 1][0]()
            drain()
        _build_bias(rtab_ref, bias_sc)
        qg_sc[...] = jnp.broadcast_to(vecs_ref[4:5, 0:ATT_WIDTH], (TQ, ATT_WIDTH)).T
        ag_sc[...] = jnp.broadcast_to(vecs_ref[3:4, GMLP_WIDTH:GMLP_WIDTH + ATT_WIDTH],
                                      (TQ, ATT_WIDTH)).T
        is_pad = ((lax.broadcasted_iota(jnp.int32, km_sc.shape, 0) < LEFT)
                  & (lax.broadcasted_iota(jnp.int32, km_sc.shape, 1) == 0))
        km_sc[...] = jnp.where(is_pad, NEG_INF, 0.0).astype(km_sc.dtype)
        zero_left_context()
        y_sc[...] = mix_and_ffn(x0_ref[...], None, 0, 0)[0]

    pl.when(t == 0)(zero_left_context)

    @pl.when(g < n_tiles - 1)
    def _():
        mix_out, ffn_out = mix_and_ffn(x_ref[...], y_sc[...], t, row0)
        o_ref[...] = ffn_out
        y_sc[...] = mix_out

    @pl.when(g == n_tiles - 1)
    def _():
        o_ref[...] = mix_and_ffn(None, y_sc[...], t, row0)[1]


def _layer_spec(layer, shape):
    zeros = (0,) * len(shape)
    return pl.BlockSpec((None,) + shape, lambda g: (layer,) + zeros,
                        pipeline_mode=pl.Buffered(1))


def _layer(layer, x, vecs, w_in, rtab, wsp, bsp, w_out, w1, w2):
    B, S, D = x.shape
    tiles_per_row = S // TQ
    n_tiles = B * tiles_per_row
    bf16 = jnp.bfloat16
    f32 = jnp.float32

    def next_tile(g):
        gi = jnp.minimum(g + 1, n_tiles - 1)
        return (gi // tiles_per_row, gi % tiles_per_row, 0)

    def out_tile(g):
        return (g // tiles_per_row, g % tiles_per_row, 0)

    hbm = pl.BlockSpec(memory_space=pl.ANY)
    return pl.pallas_call(
        functools.partial(_layer_kernel, layer, tiles_per_row, n_tiles),
        grid=(n_tiles,),
        in_specs=[
            pl.BlockSpec((None, TQ, D), lambda g: (0, 0, 0), pipeline_mode=pl.Buffered(1)),
            pl.BlockSpec((None, TQ, D), next_tile),
            _layer_spec(layer, (GAIN_ROWS, D)),
            hbm,
            _layer_spec(layer, (ATT_HEADS, MXU_COLS)),
            _layer_spec(layer, (GMLP_GROUPS, GMLP_BLOCK, GMLP_BLOCK)),
            _layer_spec(layer, (GMLP_BLOCK, GMLP_WIDTH)),
            hbm,
            hbm,
            hbm,
        ],
        out_specs=pl.BlockSpec((None, TQ, D), out_tile),
        out_shape=jax.ShapeDtypeStruct(x.shape, x.dtype),
        scratch_shapes=[
            pltpu.VMEM((LEFT + S, ATT_WIDTH), bf16),
            pltpu.VMEM((PAD_TILES + tiles_per_row, ATT_WIDTH, TQ), bf16),
            pltpu.VMEM((LEFT + S, LANES), bf16),
            pltpu.VMEM((ATT_HEADS, TK, TQ), f32),
            pltpu.VMEM((TQ, D), f32),
            pltpu.VMEM((TQ, D_FF), bf16),
            pltpu.VMEM((ATT_WIDTH, TQ), f32),
            pltpu.VMEM((ATT_WIDTH, TQ), f32),
            pltpu.VMEM((2 * ATT_WIDTH, D), bf16),
            pltpu.VMEM((D, KUG_WIDTH), bf16),
            pltpu.VMEM((D, D), bf16),
            pltpu.VMEM((D, 2 * D_FF), bf16),
            pltpu.VMEM((D_FF, D), bf16),
            pltpu.VMEM((STAGE_SLOTS, STAGE_ROWS[ATT_WIDTH], ATT_WIDTH), f32),
            pltpu.VMEM((STAGE_SLOTS, STAGE_ROWS[D], D), f32),
            pltpu.VMEM((STAGE_SLOTS, STAGE_ROWS[2 * D_FF], 2 * D_FF), f32),
            pltpu.SemaphoreType.DMA((STAGE_SLOTS,)),
            pltpu.SemaphoreType.DMA((STAGE_SLOTS,)),
            pltpu.SemaphoreType.DMA((STAGE_SLOTS,)),
        ],
        compiler_params=pltpu.CompilerParams(
            dimension_semantics=("arbitrary",),
            vmem_limit_bytes=VMEM_LIMIT_BYTES),
        name="layer",
    )(x, x, vecs, w_in, rtab, wsp, bsp, w_out, w1, w2)


def _reversed_rel_table(rel_bias):
    near = rel_bias[:, :, ::-1][:, :, :NEAR]
    far = jnp.broadcast_to(rel_bias[:, :, 2 * MAX_REL:], near.shape[:2] + (MXU_COLS - NEAR,))
    return jnp.concatenate([near, far], axis=2).astype(jnp.float32)


def _pack_gains(mix_norm_g, ffn_norm_g, k_norm_g, sgu_norm_g, gmlp_out_norm_g, att_out_norm_g,
                q_norm_g):
    depth = mix_norm_g.shape[0]
    tiled = lambda g: jnp.tile(g, (1, ATT_HEADS))
    rows = [mix_norm_g, ffn_norm_g,
            jnp.concatenate([tiled(k_norm_g), sgu_norm_g], axis=1),
            jnp.concatenate([gmlp_out_norm_g, att_out_norm_g], axis=1),
            jnp.concatenate([tiled(q_norm_g), jnp.zeros((depth, D_MODEL - ATT_WIDTH))], axis=1)]
    rows += [jnp.zeros((depth, D_MODEL))] * (GAIN_ROWS - len(rows))
    return jnp.stack(rows, axis=1).astype(jnp.float32)


def kernel(x, mix_norm_g, w_in, q_norm_g, k_norm_g, rel_bias, sgu_norm_g, w_spatial, b_spatial,
           att_out_norm_g, gmlp_out_norm_g, w_out, ffn_norm_g, w_ffn_in, w_ffn_out):
    depth = w_in.shape[0]
    params = (
        _pack_gains(mix_norm_g, ffn_norm_g, k_norm_g, sgu_norm_g, gmlp_out_norm_g,
                    att_out_norm_g, q_norm_g),
        w_in, _reversed_rel_table(rel_bias), w_spatial,
        jnp.repeat(jnp.swapaxes(b_spatial, 1, 2), GMLP_GROUP_DIM, axis=2),
        w_out, w_ffn_in, w_ffn_out)
    for l in range(depth):
        x = _layer(l, x, *params)
    return x
```

```python
import functools

import jax
import jax.numpy as jnp
from jax import lax
from jax.experimental import pallas as pl
from jax.experimental.pallas import tpu as pltpu

D_MODEL = 1024
CHUNK = 64
ATT_HEADS = 8
HEAD_DIM = 64
ATT_WIDTH = ATT_HEADS * HEAD_DIM
LEFT_CHUNKS = 8
LEFT = LEFT_CHUNKS * CHUNK
MAX_REL = 2 * CHUNK
NEAR = MAX_REL + CHUNK
GMLP_WIDTH = 512
GMLP_GROUPS = 8
GMLP_GROUP_DIM = GMLP_WIDTH // GMLP_GROUPS
GMLP_BLOCK = 128
D_FF = 2816
KUG_WIDTH = ATT_WIDTH + 2 * GMLP_WIDTH
EPS = 1e-6
NEG_INF = -1e30

LANES = 128
BF16_ROWS = 16
MXU_COLS = 256
TQ = 256
TK = LEFT + TQ
PAD_TILES = LEFT // TQ
WIN_TILES = TK // TQ
FF_CHUNK = 256
HEADS_PER_ROUND = 2
STAGE_ROWS = {ATT_WIDTH: 256, D_MODEL: 128, 2 * D_FF: 32}
GAIN_ROWS = 8
STAGE_SLOTS = 4
VMEM_LIMIT_BYTES = 60 * 1024 * 1024


def _rms(x, g):
    ms = jnp.mean(x * x, axis=-1, keepdims=True)
    return (x * lax.rsqrt(ms + EPS)) * g


def _head_rms(z, gain):
    rows = z.shape[0]
    lo = lax.broadcasted_iota(jnp.int32, (rows, LANES), 1) < HEAD_DIM
    outs = []
    for c in range(ATT_WIDTH // LANES):
        zc = z[:, c * LANES:(c + 1) * LANES]
        sq = zc * zc
        ms_lo = jnp.sum(jnp.where(lo, sq, 0.0), axis=-1, keepdims=True) * (1.0 / HEAD_DIM)
        ms_hi = jnp.sum(jnp.where(lo, 0.0, sq), axis=-1, keepdims=True) * (1.0 / HEAD_DIM)
        r = jnp.where(lo, lax.rsqrt(ms_lo + EPS), lax.rsqrt(ms_hi + EPS))
        outs.append((zc * r) * gain[:, c * LANES:(c + 1) * LANES])
    return jnp.concatenate(outs, axis=-1)


def _build_bias(rtab_ref, bias_sc):
    f32 = jnp.float32
    row = lax.broadcasted_iota(jnp.int32, (CHUNK, MXU_COLS), 0)
    lo_half = lax.broadcasted_iota(jnp.int32, (CHUNK, LANES), 1) < CHUNK
    n_kc = TK // CHUNK
    for head in range(ATT_HEADS):
        r = rtab_ref[head:head + 1, :]
        far = rtab_ref[head:head + 1, MXU_COLS - 1:MXU_COLS]
        x0 = jnp.broadcast_to(r, (CHUNK, MXU_COLS))
        for b in range(CHUNK.bit_length() - 1):
            x0 = jnp.where(((row >> b) & 1) == 1, pltpu.roll(x0, 1 << b, axis=1), x0)
        x1 = pltpu.roll(x0, CHUNK, axis=1)
        far_piece = jnp.broadcast_to(far, (CHUNK, LANES))
        neg_piece = jnp.full((CHUNK, LANES), NEG_INF, f32)

        def half(qc, kc):
            rel = kc - qc
            if rel < 0 or rel > LEFT_CHUNKS:
                return neg_piece
            if rel < LEFT_CHUNKS - 2:
                return far_piece
            col = (rel - (LEFT_CHUNKS - 2)) * CHUNK + (qc % 2) * CHUNK
            src = x1 if qc % 2 else x0
            return src[:, (col // LANES) * LANES:(col // LANES + 1) * LANES]

        row_blocks = []
        for qc in range(TQ // CHUNK):
            pieces = [jnp.where(lo_half, half(qc, 2 * vcol), half(qc, 2 * vcol + 1))
                      for vcol in range(n_kc // 2)]
            row_blocks.append(jnp.concatenate(pieces, axis=1))
        bias_sc[head] = jnp.concatenate(row_blocks, axis=0).T


def _step(x, y_prev, t, row0, mixg_ref, w_qv_ref, w_kug_ref, qg_ref, kg_ref, sgug_ref, wsp_ref,
          bsp_ref, ag_ref, gg_ref, w_out_ref, ffng_ref, w1_ref, w2_ref,
          k_sc, v_sc, km_sc, bias_sc, act_sc):
    bf16 = jnp.bfloat16
    f32 = jnp.float32
    nt = (((1,), (1,)), ((), ()))
    do_ffn = y_prev is not None
    h = _rms(x, mixg_ref[...]).astype(bf16)
    if do_ffn:
        hf = _rms(y_prev, ffng_ref[...]).astype(bf16)

    def ffn_chunk(j):
        if not do_ffn:
            return
        gate = jnp.dot(hf, w1_ref[:, j * FF_CHUNK:(j + 1) * FF_CHUNK], preferred_element_type=f32)
        up = jnp.dot(hf, w1_ref[:, D_FF + j * FF_CHUNK:D_FF + (j + 1) * FF_CHUNK],
                     preferred_element_type=f32)
        act_sc[:, j * FF_CHUNK:(j + 1) * FF_CHUNK] = (jax.nn.silu(gate) * up).astype(bf16)

    qv_t = lax.dot_general(w_qv_ref[...], h, nt, preferred_element_type=f32)
    k = jnp.dot(h, w_kug_ref[:, 0:ATT_WIDTH], preferred_element_type=f32)
    u = jnp.dot(h, w_kug_ref[:, ATT_WIDTH:ATT_WIDTH + GMLP_WIDTH], preferred_element_type=f32)
    vg = jnp.dot(h, w_kug_ref[:, ATT_WIDTH + GMLP_WIDTH:KUG_WIDTH], preferred_element_type=f32)

    q3 = qv_t[0:ATT_WIDTH].reshape(ATT_HEADS, HEAD_DIM, TQ)
    q_ms = jnp.mean(q3 * q3, axis=1, keepdims=True)
    q3 = (q3 * lax.rsqrt(q_ms + EPS)) * qg_ref[...].reshape(ATT_HEADS, HEAD_DIM, TQ)
    qn_t = (q3 * HEAD_DIM ** -0.5).reshape(ATT_WIDTH, TQ).astype(bf16)
    kn = _head_rms(k, kg_ref[...]).astype(bf16)
    k_sc[pl.ds(row0 + LEFT, TQ), :] = kn
    v_sc[t + PAD_TILES] = qv_t[ATT_WIDTH:2 * ATT_WIDTH].astype(bf16)
    skip = max(0, PAD_TILES - t) if isinstance(t, int) else 0
    win0, win_rows = skip * TQ, TK - skip * TQ
    kw = k_sc[pl.ds(row0 + win0, win_rows), :]
    km = km_sc[pl.ds(row0 + win0, win_rows), :]
    pad_rows = jnp.where(lax.broadcasted_iota(jnp.int32, (LANES, TQ), 0) == 0,
                         1.0, 0.0).astype(bf16)
    zero_rows = jnp.zeros((HEAD_DIM, TQ), bf16)
    ones_rows = jnp.ones((BF16_ROWS, win_rows), bf16)

    def scores(head):
        c = head // 2
        qh = qn_t[head * HEAD_DIM:(head + 1) * HEAD_DIM]
        pair = [qh, zero_rows] if head % 2 == 0 else [zero_rows, qh]
        rhs = jnp.concatenate(pair + [pad_rows], axis=0)
        lhs = jnp.concatenate([kw[:, c * LANES:(c + 1) * LANES], km], axis=1)
        s = jnp.dot(lhs, rhs, preferred_element_type=f32)
        return s + bias_sc[head, win0:TK, :]

    def probs(s):
        m = jnp.max(s, axis=0, keepdims=True)
        return jnp.exp(s - m).astype(bf16)

    def weighted_values(head, p):
        v_t = jnp.concatenate(
            [v_sc[t + w, head * HEAD_DIM:(head + 1) * HEAD_DIM, :]
             for w in range(skip, WIN_TILES)], axis=1)
        pv = jnp.dot(jnp.concatenate([v_t, ones_rows], axis=0), p,
                     preferred_element_type=f32)
        return pv[0:HEAD_DIM] * (1.0 / pv[HEAD_DIM:HEAD_DIM + 1])

    rounds = [list(range(r, r + HEADS_PER_ROUND)) for r in range(0, ATT_HEADS, HEADS_PER_ROUND)]
    outs = []
    next_chunk = 0
    s_next = [scores(hd) for hd in rounds[0]]
    for ri, heads in enumerate(rounds):
        s_cur = s_next
        if ri + 1 < len(rounds):
            s_next = [scores(hd) for hd in rounds[ri + 1]]
        for _ in heads:
            ffn_chunk(next_chunk)
            next_chunk += 1
        ps = [probs(s) for s in s_cur]
        outs += [weighted_values(hd, p) for hd, p in zip(heads, ps)]
    a_t = jnp.concatenate(outs, axis=0)
    a_ms = jnp.mean(a_t * a_t, axis=0, keepdims=True)
    a_n = ((a_t * lax.rsqrt(a_ms + EPS)) * ag_ref[...]).T.astype(bf16)

    u_act = jax.nn.gelu(u)
    vgn = _rms(jax.nn.gelu(vg), sgug_ref[...]).astype(bf16)
    ti = lax.broadcasted_iota(jnp.int32, (GMLP_GROUPS, GMLP_BLOCK, GMLP_BLOCK), 1)
    si = lax.broadcasted_iota(jnp.int32, (GMLP_GROUPS, GMLP_BLOCK, GMLP_BLOCK), 2)
    causal = (ti // CHUNK) >= (si // CHUNK)
    wsp = jnp.where(causal, wsp_ref[...], 0.0).astype(bf16)
    wsp = wsp.reshape(GMLP_GROUPS * GMLP_BLOCK, GMLP_BLOCK)
    first_of_pair = lax.broadcasted_iota(jnp.int32, (GMLP_BLOCK, LANES), 1) < GMLP_GROUP_DIM
    n_blk = TQ // GMLP_BLOCK
    pair_cols = []
    for pr in range(GMLP_WIDTH // LANES):
        lhs = wsp[2 * pr * GMLP_BLOCK:2 * (pr + 1) * GMLP_BLOCK, :]
        rhs = jnp.concatenate(
            [vgn[blk * GMLP_BLOCK:(blk + 1) * GMLP_BLOCK, pr * LANES:(pr + 1) * LANES]
             for blk in range(n_blk)], axis=1)
        res = jnp.dot(lhs, rhs, preferred_element_type=f32)
        pair_cols.append(jnp.concatenate(
            [jnp.where(first_of_pair,
                       res[0:GMLP_BLOCK, blk * LANES:(blk + 1) * LANES],
                       res[GMLP_BLOCK:2 * GMLP_BLOCK, blk * LANES:(blk + 1) * LANES])
             for blk in range(n_blk)], axis=0))
        if next_chunk < D_FF // FF_CHUNK:
            ffn_chunk(next_chunk)
            next_chunk += 1
    while next_chunk < D_FF // FF_CHUNK:
        ffn_chunk(next_chunk)
        next_chunk += 1
    mixed = (jnp.concatenate(pair_cols, axis=1)
             + jnp.concatenate([bsp_ref[...]] * n_blk, axis=0))
    g_n = _rms(u_act * mixed, gg_ref[...]).astype(bf16)

    ffn_out = None
    if do_ffn:
        ffn_out = y_prev + jnp.dot(act_sc[...], w2_ref[...], preferred_element_type=f32)
    mix = jnp.concatenate([a_n, g_n], axis=-1)
    mix_out = x + jnp.dot(mix, w_out_ref[...], preferred_element_type=f32)
    return mix_out, ffn_out


def _weight_loader(src, dst, ring, src_col=0, dst_col=0, dst_row=None):
    stage, sem = ring
    rows = src.shape[0]
    _, chunk, cols = stage.shape
    assert rows % chunk == 0 and chunk == STAGE_ROWS[cols] and stage.shape[0] == STAGE_SLOTS
    n = rows // chunk
    ahead = STAGE_SLOTS - 1
    assert n >= ahead

    def copy(i, slot):
        return pltpu.make_async_copy(src.at[pl.ds(i * chunk, chunk), pl.ds(src_col, cols)],
                                     stage.at[slot], sem.at[slot])

    def prefetch():
        for i in range(ahead):
            copy(i, i).start()

    def drain_natural():
        def body(i, carry):
            slot = lax.rem(i, STAGE_SLOTS)

            @pl.when(i + ahead < n)
            def _():
                copy(i + ahead, lax.rem(i + ahead, STAGE_SLOTS)).start()

            copy(i, slot).wait()
            dst[pl.ds(pl.multiple_of(i * chunk, chunk), chunk), dst_col:dst_col + cols] = (
                stage[slot].astype(dst.dtype))
            return carry

        lax.fori_loop(0, n, body, 0)

    def drain_transposed():
        assert chunk % LANES == 0
        for i in range(n):
            if i + ahead < n:
                copy(i + ahead, (i + ahead) % STAGE_SLOTS).start()
            copy(i, i % STAGE_SLOTS).wait()
            dst[dst_row:dst_row + cols, i * chunk:(i + 1) * chunk] = (
                stage[i % STAGE_SLOTS].T.astype(dst.dtype))

    return prefetch, (drain_natural if dst_row is None else drain_transposed)


def _layer_kernel(layer, tiles_per_row, n_tiles,
                  x0_ref, x_ref, vecs_ref, w_in_hbm, rtab_ref, wsp_ref, bsp_ref,
                  w_out_hbm, w1_hbm, w2_hbm,
                  o_ref, k_sc, v_sc, km_sc, bias_sc, y_sc, act_sc, qg_sc, ag_sc,
                  w_qv_sc, w_kug_sc, w_out_sc, w1_sc, w2_sc, st_k, st_d, st_ff,
                  sem_k, sem_d, sem_ff):
    g = pl.program_id(0)
    mixg_ref = vecs_ref.at[0:1, :]
    ffng_ref = vecs_ref.at[1:2, :]
    kg_ref = vecs_ref.at[2:3, 0:ATT_WIDTH]
    sgug_ref = vecs_ref.at[2:3, ATT_WIDTH:ATT_WIDTH + GMLP_WIDTH]
    gg_ref = vecs_ref.at[3:4, 0:GMLP_WIDTH]
    t = lax.rem(jnp.minimum(g + 1, n_tiles - 1), tiles_per_row)
    row0 = pl.multiple_of(t * TQ, TQ)

    def mix_and_ffn(x, y_prev, t_x, row0_x):
        return _step(x, y_prev, t_x, row0_x, mixg_ref, w_qv_sc, w_kug_sc, qg_sc, kg_ref, sgug_ref,
                     wsp_ref, bsp_ref, ag_sc, gg_ref, w_out_sc, ffng_ref, w1_sc, w2_sc,
                     k_sc, v_sc, km_sc, bias_sc, act_sc)

    def zero_left_context():
        k_sc[0:LEFT, :] = jnp.zeros((LEFT, ATT_WIDTH), k_sc.dtype)
        v_sc[0:PAD_TILES] = jnp.zeros((PAD_TILES, ATT_WIDTH, TQ), v_sc.dtype)

    @pl.when(g == 0)
    def _():
        w_in_l = w_in_hbm.at[layer]
        ring_k, ring_d, ring_ff = (st_k, sem_k), (st_d, sem_d), (st_ff, sem_ff)
        loads = [
            _weight_loader(w_in_l, w_qv_sc, ring_k, src_col=0, dst_row=0),
            _weight_loader(w_in_l, w_kug_sc, ring_d, src_col=3 * ATT_WIDTH, dst_col=ATT_WIDTH),
            _weight_loader(w_in_l, w_qv_sc, ring_k, src_col=2 * ATT_WIDTH, dst_row=ATT_WIDTH),
            _weight_loader(w_out_hbm.at[layer], w_out_sc, ring_d),
            _weight_loader(w_in_l, w_kug_sc, ring_k, src_col=ATT_WIDTH, dst_col=0),
            _weight_loader(w1_hbm.at[layer], w1_sc, ring_ff),
            _weight_loader(w2_hbm.at[layer], w2_sc, ring_d),
        ]
        loads[0][0]()
        for i, (_, drain) in enumerate(loads):
            if i + 1 < len(loads):
                loads[i + 1][0]()
            drain()
        _build_bias(rtab_ref, bias_sc)
        qg_sc[...] = jnp.broadcast_to(vecs_ref[4:5, 0:ATT_WIDTH], (TQ, ATT_WIDTH)).T
        ag_sc[...] = jnp.broadcast_to(vecs_ref[3:4, GMLP_WIDTH:GMLP_WIDTH + ATT_WIDTH],
                                      (TQ, ATT_WIDTH)).T
        is_pad = ((lax.broadcasted_iota(jnp.int32, km_sc.shape, 0) < LEFT)
                  & (lax.broadcasted_iota(jnp.int32, km_sc.shape, 1) == 0))
        km_sc[...] = jnp.where(is_pad, NEG_INF, 0.0).astype(km_sc.dtype)
        zero_left_context()
        y_sc[...] = mix_and_ffn(x0_ref[...], None, 0, 0)[0]

    pl.when(t == 0)(zero_left_context)

    mix_out, ffn_out = mix_and_ffn(x_ref[...], y_sc[...], t, row0)
    o_ref[...] = ffn_out
    y_sc[...] = mix_out


def _layer_spec(layer, shape):
    zeros = (0,) * len(shape)
    return pl.BlockSpec((None,) + shape, lambda g: (layer,) + zeros,
                        pipeline_mode=pl.Buffered(1))


def _layer(layer, x, vecs, w_in, rtab, wsp, bsp, w_out, w1, w2):
    B, S, D = x.shape
    tiles_per_row = S // TQ
    n_tiles = B * tiles_per_row
    bf16 = jnp.bfloat16
    f32 = jnp.float32

    def next_tile(g):
        gi = jnp.minimum(g + 1, n_tiles - 1)
        return (gi // tiles_per_row, gi % tiles_per_row, 0)

    def out_tile(g):
        return (g // tiles_per_row, g % tiles_per_row, 0)

    hbm = pl.BlockSpec(memory_space=pl.ANY)
    return pl.pallas_call(
        functools.partial(_layer_kernel, layer, tiles_per_row, n_tiles),
        grid=(n_tiles,),
        in_specs=[
            pl.BlockSpec((None, TQ, D), lambda g: (0, 0, 0), pipeline_mode=pl.Buffered(1)),
            pl.BlockSpec((None, TQ, D), next_tile),
            _layer_spec(layer, (GAIN_ROWS, D)),
            hbm,
            _layer_spec(layer, (ATT_HEADS, MXU_COLS)),
            _layer_spec(layer, (GMLP_GROUPS, GMLP_BLOCK, GMLP_BLOCK)),
            _layer_spec(layer, (GMLP_BLOCK, GMLP_WIDTH)),
            hbm,
            hbm,
            hbm,
        ],
        out_specs=pl.BlockSpec((None, TQ, D), out_tile),
        out_shape=jax.ShapeDtypeStruct(x.shape, x.dtype),
        scratch_shapes=[
            pltpu.VMEM((LEFT + S, ATT_WIDTH), bf16),
            pltpu.VMEM((PAD_TILES + tiles_per_row, ATT_WIDTH, TQ), bf16),
            pltpu.VMEM((LEFT + S, LANES), bf16),
            pltpu.VMEM((ATT_HEADS, TK, TQ), f32),
            pltpu.VMEM((TQ, D), f32),
            pltpu.VMEM((TQ, D_FF), bf16),
            pltpu.VMEM((ATT_WIDTH, TQ), f32),
            pltpu.VMEM((ATT_WIDTH, TQ), f32),
            pltpu.VMEM((2 * ATT_WIDTH, D), bf16),
            pltpu.VMEM((D, KUG_WIDTH), bf16),
            pltpu.VMEM((D, D), bf16),
            pltpu.VMEM((D, 2 * D_FF), bf16),
            pltpu.VMEM((D_FF, D), bf16),
            pltpu.VMEM((STAGE_SLOTS, STAGE_ROWS[ATT_WIDTH], ATT_WIDTH), f32),
            pltpu.VMEM((STAGE_SLOTS, STAGE_ROWS[D], D), f32),
            pltpu.VMEM((STAGE_SLOTS, STAGE_ROWS[2 * D_FF], 2 * D_FF), f32),
            pltpu.SemaphoreType.DMA((STAGE_SLOTS,)),
            pltpu.SemaphoreType.DMA((STAGE_SLOTS,)),
            pltpu.SemaphoreType.DMA((STAGE_SLOTS,)),
        ],
        compiler_params=pltpu.CompilerParams(
            dimension_semantics=("arbitrary",),
            vmem_limit_bytes=VMEM_LIMIT_BYTES),
        name="layer",
    )(x, x, vecs, w_in, rtab, wsp, bsp, w_out, w1, w2)


def _reversed_rel_table(rel_bias):
    near = rel_bias[:, :, ::-1][:, :, :NEAR]
    far = jnp.broadcast_to(rel_bias[:, :, 2 * MAX_REL:], near.shape[:2] + (MXU_COLS - NEAR,))
    return jnp.concatenate([near, far], axis=2).astype(jnp.float32)


def _pack_gains(mix_norm_g, ffn_norm_g, k_norm_g, sgu_norm_g, gmlp_out_norm_g, att_out_norm_g,
                q_norm_g):
    depth = mix_norm_g.shape[0]
    tiled = lambda g: jnp.tile(g, (1, ATT_HEADS))
    rows = [mix_norm_g, ffn_norm_g,
            jnp.concatenate([tiled(k_norm_g), sgu_norm_g], axis=1),
            jnp.concatenate([gmlp_out_norm_g, att_out_norm_g], axis=1),
            jnp.concatenate([tiled(q_norm_g), jnp.zeros((depth, D_MODEL - ATT_WIDTH))], axis=1)]
    rows += [jnp.zeros((depth, D_MODEL))] * (GAIN_ROWS - len(rows))
    return jnp.stack(rows, axis=1).astype(jnp.float32)


def kernel(x, mix_norm_g, w_in, q_norm_g, k_norm_g, rel_bias, sgu_norm_g, w_spatial, b_spatial,
           att_out_norm_g, gmlp_out_norm_g, w_out, ffn_norm_g, w_ffn_in, w_ffn_out):
    depth = w_in.shape[0]
    params = (
        _pack_gains(mix_norm_g, ffn_norm_g, k_norm_g, sgu_norm_g, gmlp_out_norm_g,
                    att_out_norm_g, q_norm_g),
        w_in, _reversed_rel_table(rel_bias), w_spatial,
        jnp.repeat(jnp.swapaxes(b_spatial, 1, 2), GMLP_GROUP_DIM, axis=2),
        w_out, w_ffn_in, w_ffn_out)
    for l in range(depth):
        x = _layer(l, x, *params)
    return x
```

```python
import functools

import jax
import jax.numpy as jnp
from jax import lax
from jax.experimental import pallas as pl
from jax.experimental.pallas import tpu as pltpu

D_MODEL = 1024
CHUNK = 64
ATT_HEADS = 8
HEAD_DIM = 64
ATT_WIDTH = ATT_HEADS * HEAD_DIM
LEFT_CHUNKS = 8
LEFT = LEFT_CHUNKS * CHUNK
MAX_REL = 2 * CHUNK
NEAR = MAX_REL + CHUNK
GMLP_WIDTH = 512
GMLP_GROUPS = 8
GMLP_GROUP_DIM = GMLP_WIDTH // GMLP_GROUPS
GMLP_BLOCK = 128
D_FF = 2816
KUG_WIDTH = ATT_WIDTH + 2 * GMLP_WIDTH
EPS = 1e-6
NEG_INF = -1e30

LANES = 128
BF16_ROWS = 16
MXU_COLS = 256
TQ = 256
TK = LEFT + TQ
PAD_TILES = LEFT // TQ
WIN_TILES = TK // TQ
FF_CHUNK = 256
HEADS_PER_ROUND = 2
STAGE_ROWS = {ATT_WIDTH: 256, D_MODEL: 256, 2 * D_FF: 32}
GAIN_ROWS = 8
STAGE_SLOTS = 4
VMEM_LIMIT_BYTES = 61 * 1024 * 1024


def _rms(x, g):
    ms = jnp.mean(x * x, axis=-1, keepdims=True)
    return (x * lax.rsqrt(ms + EPS)) * g


def _head_rms(z, gain):
    rows = z.shape[0]
    lo = lax.broadcasted_iota(jnp.int32, (rows, LANES), 1) < HEAD_DIM
    outs = []
    for c in range(ATT_WIDTH // LANES):
        zc = z[:, c * LANES:(c + 1) * LANES]
        sq = zc * zc
        ms_lo = jnp.sum(jnp.where(lo, sq, 0.0), axis=-1, keepdims=True) * (1.0 / HEAD_DIM)
        ms_hi = jnp.sum(jnp.where(lo, 0.0, sq), axis=-1, keepdims=True) * (1.0 / HEAD_DIM)
        r = jnp.where(lo, lax.rsqrt(ms_lo + EPS), lax.rsqrt(ms_hi + EPS))
        outs.append((zc * r) * gain[:, c * LANES:(c + 1) * LANES])
    return jnp.concatenate(outs, axis=-1)


def _build_bias(rtab_ref, bias_sc):
    f32 = jnp.float32
    row = lax.broadcasted_iota(jnp.int32, (CHUNK, MXU_COLS), 0)
    lo_half = lax.broadcasted_iota(jnp.int32, (CHUNK, LANES), 1) < CHUNK
    n_kc = TK // CHUNK
    for head in range(ATT_HEADS):
        r = rtab_ref[head:head + 1, :]
        far = rtab_ref[head:head + 1, MXU_COLS - 1:MXU_COLS]
        x0 = jnp.broadcast_to(r, (CHUNK, MXU_COLS))
        for b in range(CHUNK.bit_length() - 1):
            x0 = jnp.where(((row >> b) & 1) == 1, pltpu.roll(x0, 1 << b, axis=1), x0)
        x1 = pltpu.roll(x0, CHUNK, axis=1)
        far_piece = jnp.broadcast_to(far, (CHUNK, LANES))
        neg_piece = jnp.full((CHUNK, LANES), NEG_INF, f32)

        def half(qc, kc):
            rel = kc - qc
            if rel < 0 or rel > LEFT_CHUNKS:
                return neg_piece
            if rel < LEFT_CHUNKS - 2:
                return far_piece
            col = (rel - (LEFT_CHUNKS - 2)) * CHUNK + (qc % 2) * CHUNK
            src = x1 if qc % 2 else x0
            return src[:, (col // LANES) * LANES:(col // LANES + 1) * LANES]

        row_blocks = []
        for qc in range(TQ // CHUNK):
            pieces = [jnp.where(lo_half, half(qc, 2 * vcol), half(qc, 2 * vcol + 1))
                      for vcol in range(n_kc // 2)]
            row_blocks.append(jnp.concatenate(pieces, axis=1))
        bias_sc[head] = jnp.concatenate(row_blocks, axis=0).T


def _step(x, y_prev, t, row0, mixg_ref, w_qv_ref, w_kug_ref, qg_ref, kg_ref, sgug_ref, wsp_ref,
          bsp_ref, ag_ref, gg_ref, w_out_ref, ffng_ref, w1_ref, w2_ref,
          k_sc, v_sc, km_sc, bias_sc, act_sc):
    bf16 = jnp.bfloat16
    f32 = jnp.float32
    nt = (((1,), (1,)), ((), ()))
    do_ffn = y_prev is not None
    h = _rms(x, mixg_ref[...]).astype(bf16)
    if do_ffn:
        hf = _rms(y_prev, ffng_ref[...]).astype(bf16)

    def ffn_chunk(j):
        if not do_ffn:
            return
        gate = jnp.dot(hf, w1_ref[:, j * FF_CHUNK:(j + 1) * FF_CHUNK], preferred_element_type=f32)
        up = jnp.dot(hf, w1_ref[:, D_FF + j * FF_CHUNK:D_FF + (j + 1) * FF_CHUNK],
                     preferred_element_type=f32)
        act_sc[:, j * FF_CHUNK:(j + 1) * FF_CHUNK] = (jax.nn.silu(gate) * up).astype(bf16)

    qv_t = lax.dot_general(w_qv_ref[...], h, nt, preferred_element_type=f32)
    k = jnp.dot(h, w_kug_ref[:, 0:ATT_WIDTH], preferred_element_type=f32)
    u = jnp.dot(h, w_kug_ref[:, ATT_WIDTH:ATT_WIDTH + GMLP_WIDTH], preferred_element_type=f32)
    vg = jnp.dot(h, w_kug_ref[:, ATT_WIDTH + GMLP_WIDTH:KUG_WIDTH], preferred_element_type=f32)

    q3 = qv_t[0:ATT_WIDTH].reshape(ATT_HEADS, HEAD_DIM, TQ)
    q_ms = jnp.mean(q3 * q3, axis=1, keepdims=True)
    q3 = (q3 * lax.rsqrt(q_ms + EPS)) * qg_ref[...].reshape(ATT_HEADS, HEAD_DIM, TQ)
    qn_t = (q3 * HEAD_DIM ** -0.5).reshape(ATT_WIDTH, TQ).astype(bf16)
    kn = _head_rms(k, kg_ref[...]).astype(bf16)
    k_sc[pl.ds(row0 + LEFT, TQ), :] = kn
    v_sc[t + PAD_TILES] = qv_t[ATT_WIDTH:2 * ATT_WIDTH].astype(bf16)
    skip = max(0, PAD_TILES - t) if isinstance(t, int) else 0
    win0, win_rows = skip * TQ, TK - skip * TQ
    kw = k_sc[pl.ds(row0 + win0, win_rows), :]
    km = km_sc[pl.ds(row0 + win0, win_rows), :]
    pad_rows = jnp.where(lax.broadcasted_iota(jnp.int32, (LANES, TQ), 0) == 0,
                         1.0, 0.0).astype(bf16)
    zero_rows = jnp.zeros((HEAD_DIM, TQ), bf16)
    ones_rows = jnp.ones((BF16_ROWS, win_rows), bf16)

    def scores(head):
        c = head // 2
        qh = qn_t[head * HEAD_DIM:(head + 1) * HEAD_DIM]
        pair = [qh, zero_rows] if head % 2 == 0 else [zero_rows, qh]
        rhs = jnp.concatenate(pair + [pad_rows], axis=0)
        lhs = jnp.concatenate([kw[:, c * LANES:(c + 1) * LANES], km], axis=1)
        s = jnp.dot(lhs, rhs, preferred_element_type=f32)
        return s + bias_sc[head, win0:TK, :]

    def probs(s):
        m = jnp.max(s, axis=0, keepdims=True)
        return jnp.exp(s - m).astype(bf16)

    def weighted_values(head, p):
        v_t = jnp.concatenate(
            [v_sc[t + w, head * HEAD_DIM:(head + 1) * HEAD_DIM, :]
             for w in range(skip, WIN_TILES)], axis=1)
        pv = jnp.dot(jnp.concatenate([v_t, ones_rows], axis=0), p,
                     preferred_element_type=f32)
        return pv[0:HEAD_DIM] * (1.0 / pv[HEAD_DIM:HEAD_DIM + 1])

    rounds = [list(range(r, r + HEADS_PER_ROUND)) for r in range(0, ATT_HEADS, HEADS_PER_ROUND)]
    outs = []
    next_chunk = 0
    s_next = [scores(hd) for hd in rounds[0]]
    for ri, heads in enumerate(rounds):
        s_cur = s_next
        if ri + 1 < len(rounds):
            s_next = [scores(hd) for hd in rounds[ri + 1]]
        for _ in heads:
            ffn_chunk(next_chunk)
            next_chunk += 1
        ps = [probs(s) for s in s_cur]
        outs += [weighted_values(hd, p) for hd, p in zip(heads, ps)]
    a_t = jnp.concatenate(outs, axis=0)
    a_ms = jnp.mean(a_t * a_t, axis=0, keepdims=True)
    a_n = ((a_t * lax.rsqrt(a_ms + EPS)) * ag_ref[...]).T.astype(bf16)

    u_act = jax.nn.gelu(u)
    vgn = _rms(jax.nn.gelu(vg), sgug_ref[...]).astype(bf16)
    ti = lax.broadcasted_iota(jnp.int32, (GMLP_GROUPS, GMLP_BLOCK, GMLP_BLOCK), 1)
    si = lax.broadcasted_iota(jnp.int32, (GMLP_GROUPS, GMLP_BLOCK, GMLP_BLOCK), 2)
    causal = (ti // CHUNK) >= (si // CHUNK)
    wsp = jnp.where(causal, wsp_ref[...], 0.0).astype(bf16)
    wsp = wsp.reshape(GMLP_GROUPS * GMLP_BLOCK, GMLP_BLOCK)
    first_of_pair = lax.broadcasted_iota(jnp.int32, (GMLP_BLOCK, LANES), 1) < GMLP_GROUP_DIM
    n_blk = TQ // GMLP_BLOCK
    pair_cols = []
    for pr in range(GMLP_WIDTH // LANES):
        lhs = wsp[2 * pr * GMLP_BLOCK:2 * (pr + 1) * GMLP_BLOCK, :]
        rhs = jnp.concatenate(
            [vgn[blk * GMLP_BLOCK:(blk + 1) * GMLP_BLOCK, pr * LANES:(pr + 1) * LANES]
             for blk in range(n_blk)], axis=1)
        res = jnp.dot(lhs, rhs, preferred_element_type=f32)
        pair_cols.append(jnp.concatenate(
            [jnp.where(first_of_pair,
                       res[0:GMLP_BLOCK, blk * LANES:(blk + 1) * LANES],
                       res[GMLP_BLOCK:2 * GMLP_BLOCK, blk * LANES:(blk + 1) * LANES])
             for blk in range(n_blk)], axis=0))
        if next_chunk < D_FF // FF_CHUNK:
            ffn_chunk(next_chunk)
            next_chunk += 1
    while next_chunk < D_FF // FF_CHUNK:
        ffn_chunk(next_chunk)
        next_chunk += 1
    mixed = (jnp.concatenate(pair_cols, axis=1)
             + jnp.concatenate([bsp_ref[...]] * n_blk, axis=0))
    g_n = _rms(u_act * mixed, gg_ref[...]).astype(bf16)

    ffn_out = None
    if do_ffn:
        ffn_out = y_prev + jnp.dot(act_sc[...], w2_ref[...], preferred_element_type=f32)
    mix = jnp.concatenate([a_n, g_n], axis=-1)
    mix_out = x + jnp.dot(mix, w_out_ref[...], preferred_element_type=f32)
    return mix_out, ffn_out


def _weight_loader(src, dst, ring, src_col=0, dst_col=0, dst_row=None):
    stage, sem = ring
    rows = src.shape[0]
    _, chunk, cols = stage.shape
    assert rows % chunk == 0 and chunk == STAGE_ROWS[cols] and stage.shape[0] == STAGE_SLOTS
    n = rows // chunk
    ahead = STAGE_SLOTS - 1
    assert n >= ahead

    def copy(i, slot):
        return pltpu.make_async_copy(src.at[pl.ds(i * chunk, chunk), pl.ds(src_col, cols)],
                                     stage.at[slot], sem.at[slot])

    def prefetch():
        for i in range(ahead):
            copy(i, i).start()

    def drain_natural():
        def body(i, carry):
            slot = lax.rem(i, STAGE_SLOTS)

            @pl.when(i + ahead < n)
            def _():
                copy(i + ahead, lax.rem(i + ahead, STAGE_SLOTS)).start()

            copy(i, slot).wait()
            dst[pl.ds(pl.multiple_of(i * chunk, chunk), chunk), dst_col:dst_col + cols] = (
                stage[slot].astype(dst.dtype))
            return carry

        lax.fori_loop(0, n, body, 0)

    def drain_transposed():
        assert chunk % LANES == 0
        for i in range(n):
            if i + ahead < n:
                copy(i + ahead, (i + ahead) % STAGE_SLOTS).start()
            copy(i, i % STAGE_SLOTS).wait()
            dst[dst_row:dst_row + cols, i * chunk:(i + 1) * chunk] = (
                stage[i % STAGE_SLOTS].T.astype(dst.dtype))

    return prefetch, (drain_natural if dst_row is None else drain_transposed)


def _layer_kernel(layer, tiles_per_row, n_tiles,
                  x0_ref, x_ref, vecs_ref, w_in_hbm, rtab_ref, wsp_ref, bsp_ref,
                  w_out_hbm, w1_hbm, w2_hbm,
                  o_ref, k_sc, v_sc, km_sc, bias_sc, y_sc, act_sc, qg_sc, ag_sc,
                  w_qv_sc, w_kug_sc, w_out_sc, w1_sc, w2_sc, st_k, st_d, st_ff,
                  sem_k, sem_d, sem_ff):
    g = pl.program_id(0)
    mixg_ref = vecs_ref.at[0:1, :]
    ffng_ref = vecs_ref.at[1:2, :]
    kg_ref = vecs_ref.at[2:3, 0:ATT_WIDTH]
    sgug_ref = vecs_ref.at[2:3, ATT_WIDTH:ATT_WIDTH + GMLP_WIDTH]
    gg_ref = vecs_ref.at[3:4, 0:GMLP_WIDTH]
    t = lax.rem(jnp.minimum(g + 1, n_tiles - 1), tiles_per_row)
    row0 = pl.multiple_of(t * TQ, TQ)

    def mix_and_ffn(x, y_prev, t_x, row0_x):
        return _step(x, y_prev, t_x, row0_x, mixg_ref, w_qv_sc, w_kug_sc, qg_sc, kg_ref, sgug_ref,
                     wsp_ref, bsp_ref, ag_sc, gg_ref, w_out_sc, ffng_ref, w1_sc, w2_sc,
                     k_sc, v_sc, km_sc, bias_sc, act_sc)

    def zero_left_context():
        k_sc[0:LEFT, :] = jnp.zeros((LEFT, ATT_WIDTH), k_sc.dtype)
        v_sc[0:PAD_TILES] = jnp.zeros((PAD_TILES, ATT_WIDTH, TQ), v_sc.dtype)

    @pl.when(g == 0)
    def _():
        w_in_l = w_in_hbm.at[layer]
        ring_k, ring_d, ring_ff = (st_k, sem_k), (st_d, sem_d), (st_ff, sem_ff)
        loads = [
            _weight_loader(w_in_l, w_qv_sc, ring_k, src_col=0, dst_row=0),
            _weight_loader(w_in_l, w_kug_sc, ring_d, src_col=3 * ATT_WIDTH, dst_col=ATT_WIDTH),
            _weight_loader(w_in_l, w_qv_sc, ring_k, src_col=2 * ATT_WIDTH, dst_row=ATT_WIDTH),
            _weight_loader(w_out_hbm.at[layer], w_out_sc, ring_d),
            _weight_loader(w_in_l, w_kug_sc, ring_k, src_col=ATT_WIDTH, dst_col=0),
            _weight_loader(w1_hbm.at[layer], w1_sc, ring_ff),
            _weight_loader(w2_hbm.at[layer], w2_sc, ring_d),
        ]
        loads[0][0]()
        for i, (_, drain) in enumerate(loads):
            if i + 1 < len(loads):
                loads[i + 1][0]()
            drain()
        _build_bias(rtab_ref, bias_sc)
        qg_sc[...] = jnp.broadcast_to(vecs_ref[4:5, 0:ATT_WIDTH], (TQ, ATT_WIDTH)).T
        ag_sc[...] = jnp.broadcast_to(vecs_ref[3:4, GMLP_WIDTH:GMLP_WIDTH + ATT_WIDTH],
                                      (TQ, ATT_WIDTH)).T
        is_pad = ((lax.broadcasted_iota(jnp.int32, km_sc.shape, 0) < LEFT)
                  & (lax.broadcasted_iota(jnp.int32, km_sc.shape, 1) == 0))
        km_sc[...] = jnp.where(is_pad, NEG_INF, 0.0).astype(km_sc.dtype)
        zero_left_context()
        y_sc[...] = mix_and_ffn(x0_ref[...], None, 0, 0)[0]

    pl.when(t == 0)(zero_left_context)

    mix_out, ffn_out = mix_and_ffn(x_ref[...], y_sc[...], t, row0)
    o_ref[...] = ffn_out
    y_sc[...] = mix_out


def _layer_spec(layer, shape):
    zeros = (0,) * len(shape)
    return pl.BlockSpec((None,) + shape, lambda g: (layer,) + zeros,
                        pipeline_mode=pl.Buffered(1))


def _layer(layer, x, vecs, w_in, rtab, wsp, bsp, w_out, w1, w2):
    B, S, D = x.shape
    tiles_per_row = S // TQ
    n_tiles = B * tiles_per_row
    bf16 = jnp.bfloat16
    f32 = jnp.float32

    def next_tile(g):
        gi = jnp.minimum(g + 1, n_tiles - 1)
        return (gi // tiles_per_row, gi % tiles_per_row, 0)

    def out_tile(g):
        return (g // tiles_per_row, g % tiles_per_row, 0)

    hbm = pl.BlockSpec(memory_space=pl.ANY)
    return pl.pallas_call(
        functools.partial(_layer_kernel, layer, tiles_per_row, n_tiles),
        grid=(n_tiles,),
        in_specs=[
            pl.BlockSpec((None, TQ, D), lambda g: (0, 0, 0), pipeline_mode=pl.Buffered(1)),
            pl.BlockSpec((None, TQ, D), next_tile),
            _layer_spec(layer, (GAIN_ROWS, D)),
            hbm,
            _layer_spec(layer, (ATT_HEADS, MXU_COLS)),
            _layer_spec(layer, (GMLP_GROUPS, GMLP_BLOCK, GMLP_BLOCK)),
            _layer_spec(layer, (GMLP_BLOCK, GMLP_WIDTH)),
            hbm,
            hbm,
            hbm,
        ],
        out_specs=pl.BlockSpec((None, TQ, D), out_tile),
        out_shape=jax.ShapeDtypeStruct(x.shape, x.dtype),
        scratch_shapes=[
            pltpu.VMEM((LEFT + S, ATT_WIDTH), bf16),
            pltpu.VMEM((PAD_TILES + tiles_per_row, ATT_WIDTH, TQ), bf16),
            pltpu.VMEM((LEFT + S, LANES), bf16),
            pltpu.VMEM((ATT_HEADS, TK, TQ), f32),
            pltpu.VMEM((TQ, D), f32),
            pltpu.VMEM((TQ, D_FF), bf16),
            pltpu.VMEM((ATT_WIDTH, TQ), f32),
            pltpu.VMEM((ATT_WIDTH, TQ), f32),
            pltpu.VMEM((2 * ATT_WIDTH, D), bf16),
            pltpu.VMEM((D, KUG_WIDTH), bf16),
            pltpu.VMEM((D, D), bf16),
            pltpu.VMEM((D, 2 * D_FF), bf16),
            pltpu.VMEM((D_FF, D), bf16),
            pltpu.VMEM((STAGE_SLOTS, STAGE_ROWS[ATT_WIDTH], ATT_WIDTH), f32),
            pltpu.VMEM((STAGE_SLOTS, STAGE_ROWS[D], D), f32),
            pltpu.VMEM((STAGE_SLOTS, STAGE_ROWS[2 * D_FF], 2 * D_FF), f32),
            pltpu.SemaphoreType.DMA((STAGE_SLOTS,)),
            pltpu.SemaphoreType.DMA((STAGE_SLOTS,)),
            pltpu.SemaphoreType.DMA((STAGE_SLOTS,)),
        ],
        compiler_params=pltpu.CompilerParams(
            dimension_semantics=("arbitrary",),
            vmem_limit_bytes=VMEM_LIMIT_BYTES),
        name="layer",
    )(x, x, vecs, w_in, rtab, wsp, bsp, w_out, w1, w2)


def _reversed_rel_table(rel_bias):
    near = rel_bias[:, :, ::-1][:, :, :NEAR]
    far = jnp.broadcast_to(rel_bias[:, :, 2 * MAX_REL:], near.shape[:2] + (MXU_COLS - NEAR,))
    return jnp.concatenate([near, far], axis=2).astype(jnp.float32)


def _pack_gains(mix_norm_g, ffn_norm_g, k_norm_g, sgu_norm_g, gmlp_out_norm_g, att_out_norm_g,
                q_norm_g):
    depth = mix_norm_g.shape[0]
    tiled = lambda g: jnp.tile(g, (1, ATT_HEADS))
    rows = [mix_norm_g, ffn_norm_g,
            jnp.concatenate([tiled(k_norm_g), sgu_norm_g], axis=1),
            jnp.concatenate([gmlp_out_norm_g, att_out_norm_g], axis=1),
            jnp.concatenate([tiled(q_norm_g), jnp.zeros((depth, D_MODEL - ATT_WIDTH))], axis=1)]
    rows += [jnp.zeros((depth, D_MODEL))] * (GAIN_ROWS - len(rows))
    return jnp.stack(rows, axis=1).astype(jnp.float32)


def kernel(x, mix_norm_g, w_in, q_norm_g, k_norm_g, rel_bias, sgu_norm_g, w_spatial, b_spatial,
           att_out_norm_g, gmlp_out_norm_g, w_out, ffn_norm_g, w_ffn_in, w_ffn_out):
    depth = w_in.shape[0]
    params = (
        _pack_gains(mix_norm_g, ffn_norm_g, k_norm_g, sgu_norm_g, gmlp_out_norm_g,
                    att_out_norm_g, q_norm_g),
        w_in, _reversed_rel_table(rel_bias), w_spatial,
        jnp.repeat(jnp.swapaxes(b_spatial, 1, 2), GMLP_GROUP_DIM, axis=2),
        w_out, w_ffn_in, w_ffn_out)
    for l in range(depth):
        x = _layer(l, x, *params)
    return x
```

```python
import functools

import jax
import jax.numpy as jnp
from jax import lax
from jax.experimental import pallas as pl
from jax.experimental.pallas import tpu as pltpu

D_MODEL = 1024
CHUNK = 64
ATT_HEADS = 8
HEAD_DIM = 64
ATT_WIDTH = ATT_HEADS * HEAD_DIM
LEFT_CHUNKS = 8
LEFT = LEFT_CHUNKS * CHUNK
MAX_REL = 2 * CHUNK
NEAR = MAX_REL + CHUNK
GMLP_WIDTH = 512
GMLP_GROUPS = 8
GMLP_GROUP_DIM = GMLP_WIDTH // GMLP_GROUPS
GMLP_BLOCK = 128
D_FF = 2816
KUG_WIDTH = ATT_WIDTH + 2 * GMLP_WIDTH
EPS = 1e-6
NEG_INF = -1e30

LANES = 128
BF16_ROWS = 16
MXU_COLS = 256
TQ = 256
TK = LEFT + TQ
PAD_TILES = LEFT // TQ
WIN_TILES = TK // TQ
FF_CHUNK = 256
HEADS_PER_ROUND = 2
STAGE_ROWS = {ATT_WIDTH: 256, D_MODEL: 256, 2 * D_FF: 32}
GAIN_ROWS = 8
STAGE_SLOTS = 4
VMEM_LIMIT_BYTES = 61 * 1024 * 1024


def _rms(x, g):
    ms = jnp.mean(x * x, axis=-1, keepdims=True)
    return (x * lax.rsqrt(ms + EPS)) * g


def _head_rms(z, gain):
    rows = z.shape[0]
    lo = lax.broadcasted_iota(jnp.int32, (rows, LANES), 1) < HEAD_DIM
    outs = []
    for c in range(ATT_WIDTH // LANES):
        zc = z[:, c * LANES:(c + 1) * LANES]
        sq = zc * zc
        ms_lo = jnp.sum(jnp.where(lo, sq, 0.0), axis=-1, keepdims=True) * (1.0 / HEAD_DIM)
        ms_hi = jnp.sum(jnp.where(lo, 0.0, sq), axis=-1, keepdims=True) * (1.0 / HEAD_DIM)
        r = jnp.where(lo, lax.rsqrt(ms_lo + EPS), lax.rsqrt(ms_hi + EPS))
        outs.append((zc * r) * gain[:, c * LANES:(c + 1) * LANES])
    return jnp.concatenate(outs, axis=-1)


def _build_bias(rtab_ref, bias_sc):
    f32 = jnp.float32
    row = lax.broadcasted_iota(jnp.int32, (CHUNK, MXU_COLS), 0)
    lo_half = lax.broadcasted_iota(jnp.int32, (CHUNK, LANES), 1) < CHUNK
    n_kc = TK // CHUNK
    for head in range(ATT_HEADS):
        r = rtab_ref[head:head + 1, :]
        far = rtab_ref[head:head + 1, MXU_COLS - 1:MXU_COLS]
        x0 = jnp.broadcast_to(r, (CHUNK, MXU_COLS))
        for b in range(CHUNK.bit_length() - 1):
            x0 = jnp.where(((row >> b) & 1) == 1, pltpu.roll(x0, 1 << b, axis=1), x0)
        x1 = pltpu.roll(x0, CHUNK, axis=1)
        far_piece = jnp.broadcast_to(far, (CHUNK, LANES))
        neg_piece = jnp.full((CHUNK, LANES), NEG_INF, f32)

        def half(qc, kc):
            rel = kc - qc
            if rel < 0 or rel > LEFT_CHUNKS:
                return neg_piece
            if rel < LEFT_CHUNKS - 2:
                return far_piece
            col = (rel - (LEFT_CHUNKS - 2)) * CHUNK + (qc % 2) * CHUNK
            src = x1 if qc % 2 else x0
            return src[:, (col // LANES) * LANES:(col // LANES + 1) * LANES]

        row_blocks = []
        for qc in range(TQ // CHUNK):
            pieces = [jnp.where(lo_half, half(qc, 2 * vcol), half(qc, 2 * vcol + 1))
                      for vcol in range(n_kc // 2)]
            row_blocks.append(jnp.concatenate(pieces, axis=1))
        bias_sc[head] = jnp.concatenate(row_blocks, axis=0).T


def _step(x, y_prev, t, row0, mixg_ref, w_qv_ref, w_kug_ref, qg_ref, kg_ref, sgug_ref, wsp_ref,
          bsp_ref, ag_ref, gg_ref, w_out_ref, ffng_ref, w1_ref, w2_ref,
          k_sc, v_sc, km_sc, bias_sc, act_sc):
    bf16 = jnp.bfloat16
    f32 = jnp.float32
    nt = (((1,), (1,)), ((), ()))
    do_ffn = y_prev is not None
    h = _rms(x, mixg_ref[...]).astype(bf16)
    if do_ffn:
        hf = _rms(y_prev, ffng_ref[...]).astype(bf16)

    def ffn_chunk(j):
        if not do_ffn:
            return
        gate = jnp.dot(hf, w1_ref[:, j * FF_CHUNK:(j + 1) * FF_CHUNK], preferred_element_type=f32)
        up = jnp.dot(hf, w1_ref[:, D_FF + j * FF_CHUNK:D_FF + (j + 1) * FF_CHUNK],
                     preferred_element_type=f32)
        act_sc[:, j * FF_CHUNK:(j + 1) * FF_CHUNK] = (jax.nn.silu(gate) * up).astype(bf16)

    qv_t = lax.dot_general(w_qv_ref[...], h, nt, preferred_element_type=f32)
    k = jnp.dot(h, w_kug_ref[:, 0:ATT_WIDTH], preferred_element_type=f32)
    u = jnp.dot(h, w_kug_ref[:, ATT_WIDTH:ATT_WIDTH + GMLP_WIDTH], preferred_element_type=f32)
    vg = jnp.dot(h, w_kug_ref[:, ATT_WIDTH + GMLP_WIDTH:KUG_WIDTH], preferred_element_type=f32)

    q3 = qv_t[0:ATT_WIDTH].reshape(ATT_HEADS, HEAD_DIM, TQ)
    q_ms = jnp.mean(q3 * q3, axis=1, keepdims=True)
    q3 = (q3 * lax.rsqrt(q_ms + EPS)) * qg_ref[...].reshape(ATT_HEADS, HEAD_DIM, TQ)
    qn_t = (q3 * HEAD_DIM ** -0.5).reshape(ATT_WIDTH, TQ).astype(bf16)
    kn = _head_rms(k, kg_ref[...]).astype(bf16)
    k_sc[pl.ds(row0 + LEFT, TQ), :] = kn
    v_sc[t + PAD_TILES] = qv_t[ATT_WIDTH:2 * ATT_WIDTH].astype(bf16)
    skip = max(0, PAD_TILES - t) if isinstance(t, int) else 0
    win0, win_rows = skip * TQ, TK - skip * TQ
    kw = k_sc[pl.ds(row0 + win0, win_rows), :]
    km = km_sc[pl.ds(row0 + win0, win_rows), :]
    pad_rows = jnp.where(lax.broadcasted_iota(jnp.int32, (LANES, TQ), 0) == 0,
                         1.0, 0.0).astype(bf16)
    zero_rows = jnp.zeros((HEAD_DIM, TQ), bf16)
    ones_rows = jnp.ones((BF16_ROWS, win_rows), bf16)

    def scores(head):
        c = head // 2
        qh = qn_t[head * HEAD_DIM:(head + 1) * HEAD_DIM]
        pair = [qh, zero_rows] if head % 2 == 0 else [zero_rows, qh]
        rhs = jnp.concatenate(pair + [pad_rows], axis=0)
        lhs = jnp.concatenate([kw[:, c * LANES:(c + 1) * LANES], km], axis=1)
        s = jnp.dot(lhs, rhs, preferred_element_type=f32)
        return s + bias_sc[head, win0:TK, :]

    def probs(s):
        m = jnp.max(s, axis=0, keepdims=True)
        return jnp.exp(s - m).astype(bf16)

    def weighted_values(head, p):
        v_t = jnp.concatenate(
            [v_sc[t + w, head * HEAD_DIM:(head + 1) * HEAD_DIM, :]
             for w in range(skip, WIN_TILES)], axis=1)
        pv = jnp.dot(jnp.concatenate([v_t, ones_rows], axis=0), p,
                     preferred_element_type=f32)
        return pv[0:HEAD_DIM] * (1.0 / pv[HEAD_DIM:HEAD_DIM + 1])

    rounds = [list(range(r, r + HEADS_PER_ROUND)) for r in range(0, ATT_HEADS, HEADS_PER_ROUND)]
    outs = []
    ffn_chunk(0)
    next_chunk = 1
    s_next = [scores(hd) for hd in rounds[0]]
    for ri, heads in enumerate(rounds):
        s_cur = s_next
        if ri + 1 < len(rounds):
            s_next = [scores(hd) for hd in rounds[ri + 1]]
        for _ in heads:
            ffn_chunk(next_chunk)
            next_chunk += 1
        ps = [probs(s) for s in s_cur]
        outs += [weighted_values(hd, p) for hd, p in zip(heads, ps)]
    a_t = jnp.concatenate(outs, axis=0)
    a_ms = jnp.mean(a_t * a_t, axis=0, keepdims=True)
    a_n = ((a_t * lax.rsqrt(a_ms + EPS)) * ag_ref[...]).T.astype(bf16)

    u_act = jax.nn.gelu(u)
    vgn = _rms(jax.nn.gelu(vg), sgug_ref[...]).astype(bf16)
    ti = lax.broadcasted_iota(jnp.int32, (GMLP_GROUPS, GMLP_BLOCK, GMLP_BLOCK), 1)
    si = lax.broadcasted_iota(jnp.int32, (GMLP_GROUPS, GMLP_BLOCK, GMLP_BLOCK), 2)
    causal = (ti // CHUNK) >= (si // CHUNK)
    wsp = jnp.where(causal, wsp_ref[...], 0.0).astype(bf16)
    wsp = wsp.reshape(GMLP_GROUPS * GMLP_BLOCK, GMLP_BLOCK)
    first_of_pair = lax.broadcasted_iota(jnp.int32, (GMLP_BLOCK, LANES), 1) < GMLP_GROUP_DIM
    n_blk = TQ // GMLP_BLOCK
    pair_cols = []
    for pr in range(GMLP_WIDTH // LANES):
        lhs = wsp[2 * pr * GMLP_BLOCK:2 * (pr + 1) * GMLP_BLOCK, :]
        rhs = jnp.concatenate(
            [vgn[blk * GMLP_BLOCK:(blk + 1) * GMLP_BLOCK, pr * LANES:(pr + 1) * LANES]
             for blk in range(n_blk)], axis=1)
        res = jnp.dot(lhs, rhs, preferred_element_type=f32)
        pair_cols.append(jnp.concatenate(
            [jnp.where(first_of_pair,
                       res[0:GMLP_BLOCK, blk * LANES:(blk + 1) * LANES],
                       res[GMLP_BLOCK:2 * GMLP_BLOCK, blk * LANES:(blk + 1) * LANES])
             for blk in range(n_blk)], axis=0))
        if next_chunk < D_FF // FF_CHUNK:
            ffn_chunk(next_chunk)
            next_chunk += 1
    while next_chunk < D_FF // FF_CHUNK:
        ffn_chunk(next_chunk)
        next_chunk += 1
    mixed = (jnp.concatenate(pair_cols, axis=1)
             + jnp.concatenate([bsp_ref[...]] * n_blk, axis=0))
    g_n = _rms(u_act * mixed, gg_ref[...]).astype(bf16)

    ffn_out = None
    if do_ffn:
        ffn_out = y_prev + jnp.dot(act_sc[...], w2_ref[...], preferred_element_type=f32)
    mix = jnp.concatenate([a_n, g_n], axis=-1)
    mix_out = x + jnp.dot(mix, w_out_ref[...], preferred_element_type=f32)
    return mix_out, ffn_out


def _weight_loader(src, dst, ring, src_col=0, dst_col=0, dst_row=None):
    stage, sem = ring
    rows = src.shape[0]
    _, chunk, cols = stage.shape
    assert rows % chunk == 0 and chunk == STAGE_ROWS[cols] and stage.shape[0] == STAGE_SLOTS
    n = rows // chunk
    ahead = STAGE_SLOTS - 1
    assert n >= ahead

    def copy(i, slot):
        return pltpu.make_async_copy(src.at[pl.ds(i * chunk, chunk), pl.ds(src_col, cols)],
                                     stage.at[slot], sem.at[slot])

    def prefetch():
        for i in range(ahead):
            copy(i, i).start()

    def drain_natural():
        def body(i, carry):
            slot = lax.rem(i, STAGE_SLOTS)

            @pl.when(i + ahead < n)
            def _():
                copy(i + ahead, lax.rem(i + ahead, STAGE_SLOTS)).start()

            copy(i, slot).wait()
            dst[pl.ds(pl.multiple_of(i * chunk, chunk), chunk), dst_col:dst_col + cols] = (
                stage[slot].astype(dst.dtype))
            return carry

        lax.fori_loop(0, n, body, 0)

    def drain_transposed():
        assert chunk % LANES == 0
        for i in range(n):
            if i + ahead < n:
                copy(i + ahead, (i + ahead) % STAGE_SLOTS).start()
            copy(i, i % STAGE_SLOTS).wait()
            dst[dst_row:dst_row + cols, i * chunk:(i + 1) * chunk] = (
                stage[i % STAGE_SLOTS].T.astype(dst.dtype))

    return prefetch, (drain_natural if dst_row is None else drain_transposed)


def _layer_kernel(layer, tiles_per_row, n_tiles,
                  x0_ref, x_ref, vecs_ref, w_in_hbm, rtab_ref, wsp_ref, bsp_ref,
                  w_out_hbm, w1_hbm, w2_hbm,
                  o_ref, k_sc, v_sc, km_sc, bias_sc, y_sc, act_sc, qg_sc, ag_sc,
                  w_qv_sc, w_kug_sc, w_out_sc, w1_sc, w2_sc, st_k, st_d, st_ff,
                  sem_k, sem_d, sem_ff):
    g = pl.program_id(0)
    mixg_ref = vecs_ref.at[0:1, :]
    ffng_ref = vecs_ref.at[1:2, :]
    kg_ref = vecs_ref.at[2:3, 0:ATT_WIDTH]
    sgug_ref = vecs_ref.at[2:3, ATT_WIDTH:ATT_WIDTH + GMLP_WIDTH]
    gg_ref = vecs_ref.at[3:4, 0:GMLP_WIDTH]
    t = lax.rem(jnp.minimum(g + 1, n_tiles - 1), tiles_per_row)
    row0 = pl.multiple_of(t * TQ, TQ)

    def mix_and_ffn(x, y_prev, t_x, row0_x):
        return _step(x, y_prev, t_x, row0_x, mixg_ref, w_qv_sc, w_kug_sc, qg_sc, kg_ref, sgug_ref,
                     wsp_ref, bsp_ref, ag_sc, gg_ref, w_out_sc, ffng_ref, w1_sc, w2_sc,
                     k_sc, v_sc, km_sc, bias_sc, act_sc)

    def zero_left_context():
        k_sc[0:LEFT, :] = jnp.zeros((LEFT, ATT_WIDTH), k_sc.dtype)
        v_sc[0:PAD_TILES] = jnp.zeros((PAD_TILES, ATT_WIDTH, TQ), v_sc.dtype)

    @pl.when(g == 0)
    def _():
        w_in_l = w_in_hbm.at[layer]
        ring_k, ring_d, ring_ff = (st_k, sem_k), (st_d, sem_d), (st_ff, sem_ff)
        loads = [
            _weight_loader(w_in_l, w_qv_sc, ring_k, src_col=0, dst_row=0),
            _weight_loader(w_in_l, w_kug_sc, ring_d, src_col=3 * ATT_WIDTH, dst_col=ATT_WIDTH),
            _weight_loader(w_in_l, w_qv_sc, ring_k, src_col=2 * ATT_WIDTH, dst_row=ATT_WIDTH),
            _weight_loader(w_out_hbm.at[layer], w_out_sc, ring_d),
            _weight_loader(w_in_l, w_kug_sc, ring_k, src_col=ATT_WIDTH, dst_col=0),
            _weight_loader(w1_hbm.at[layer], w1_sc, ring_ff),
            _weight_loader(w2_hbm.at[layer], w2_sc, ring_d),
        ]
        loads[0][0]()
        for i, (_, drain) in enumerate(loads):
            if i + 1 < len(loads):
                loads[i + 1][0]()
            drain()
        _build_bias(rtab_ref, bias_sc)
        qg_sc[...] = jnp.broadcast_to(vecs_ref[4:5, 0:ATT_WIDTH], (TQ, ATT_WIDTH)).T
        ag_sc[...] = jnp.broadcast_to(vecs_ref[3:4, GMLP_WIDTH:GMLP_WIDTH + ATT_WIDTH],
                                      (TQ, ATT_WIDTH)).T
        is_pad = ((lax.broadcasted_iota(jnp.int32, km_sc.shape, 0) < LEFT)
                  & (lax.broadcasted_iota(jnp.int32, km_sc.shape, 1) == 0))
        km_sc[...] = jnp.where(is_pad, NEG_INF, 0.0).astype(km_sc.dtype)
        zero_left_context()
        y_sc[...] = mix_and_ffn(x0_ref[...], None, 0, 0)[0]

    pl.when(t == 0)(zero_left_context)

    mix_out, ffn_out = mix_and_ffn(x_ref[...], y_sc[...], t, row0)
    o_ref[...] = ffn_out
    y_sc[...] = mix_out


def _layer_spec(layer, shape):
    zeros = (0,) * len(shape)
    return pl.BlockSpec((None,) + shape, lambda g: (layer,) + zeros,
                        pipeline_mode=pl.Buffered(1))


def _layer(layer, x, vecs, w_in, rtab, wsp, bsp, w_out, w1, w2):
    B, S, D = x.shape
    tiles_per_row = S // TQ
    n_tiles = B * tiles_per_row
    bf16 = jnp.bfloat16
    f32 = jnp.float32

    def next_tile(g):
        gi = jnp.minimum(g + 1, n_tiles - 1)
        return (gi // tiles_per_row, gi % tiles_per_row, 0)

    def out_tile(g):
        return (g // tiles_per_row, g % tiles_per_row, 0)

    hbm = pl.BlockSpec(memory_space=pl.ANY)
    return pl.pallas_call(
        functools.partial(_layer_kernel, layer, tiles_per_row, n_tiles),
        grid=(n_tiles,),
        in_specs=[
            pl.BlockSpec((None, TQ, D), lambda g: (0, 0, 0), pipeline_mode=pl.Buffered(1)),
            pl.BlockSpec((None, TQ, D), next_tile),
            _layer_spec(layer, (GAIN_ROWS, D)),
            hbm,
            _layer_spec(layer, (ATT_HEADS, MXU_COLS)),
            _layer_spec(layer, (GMLP_GROUPS, GMLP_BLOCK, GMLP_BLOCK)),
            _layer_spec(layer, (GMLP_BLOCK, GMLP_WIDTH)),
            hbm,
            hbm,
            hbm,
        ],
        out_specs=pl.BlockSpec((None, TQ, D), out_tile),
        out_shape=jax.ShapeDtypeStruct(x.shape, x.dtype),
        scratch_shapes=[
            pltpu.VMEM((LEFT + S, ATT_WIDTH), bf16),
            pltpu.VMEM((PAD_TILES + tiles_per_row, ATT_WIDTH, TQ), bf16),
            pltpu.VMEM((LEFT + S, LANES), bf16),
            pltpu.VMEM((ATT_HEADS, TK, TQ), f32),
            pltpu.VMEM((TQ, D), f32),
            pltpu.VMEM((TQ, D_FF), bf16),
            pltpu.VMEM((ATT_WIDTH, TQ), f32),
            pltpu.VMEM((ATT_WIDTH, TQ), f32),
            pltpu.VMEM((2 * ATT_WIDTH, D), bf16),
            pltpu.VMEM((D, KUG_WIDTH), bf16),
            pltpu.VMEM((D, D), bf16),
            pltpu.VMEM((D, 2 * D_FF), bf16),
            pltpu.VMEM((D_FF, D), bf16),
            pltpu.VMEM((STAGE_SLOTS, STAGE_ROWS[ATT_WIDTH], ATT_WIDTH), f32),
            pltpu.VMEM((STAGE_SLOTS, STAGE_ROWS[D], D), f32),
            pltpu.VMEM((STAGE_SLOTS, STAGE_ROWS[2 * D_FF], 2 * D_FF), f32),
            pltpu.SemaphoreType.DMA((STAGE_SLOTS,)),
            pltpu.SemaphoreType.DMA((STAGE_SLOTS,)),
            pltpu.SemaphoreType.DMA((STAGE_SLOTS,)),
        ],
        compiler_params=pltpu.CompilerParams(
            dimension_semantics=("arbitrary",),
            vmem_limit_bytes=VMEM_LIMIT_BYTES),
        name="layer",
    )(x, x, vecs, w_in, rtab, wsp, bsp, w_out, w1, w2)


def _reversed_rel_table(rel_bias):
    near = rel_bias[:, :, ::-1][:, :, :NEAR]
    far = jnp.broadcast_to(rel_bias[:, :, 2 * MAX_REL:], near.shape[:2] + (MXU_COLS - NEAR,))
    return jnp.concatenate([near, far], axis=2).astype(jnp.float32)


def _pack_gains(mix_norm_g, ffn_norm_g, k_norm_g, sgu_norm_g, gmlp_out_norm_g, att_out_norm_g,
                q_norm_g):
    depth = mix_norm_g.shape[0]
    tiled = lambda g: jnp.tile(g, (1, ATT_HEADS))
    rows = [mix_norm_g, ffn_norm_g,
            jnp.concatenate([tiled(k_norm_g), sgu_norm_g], axis=1),
            jnp.concatenate([gmlp_out_norm_g, att_out_norm_g], axis=1),
            jnp.concatenate([tiled(q_norm_g), jnp.zeros((depth, D_MODEL - ATT_WIDTH))], axis=1)]
    rows += [jnp.zeros((depth, D_MODEL))] * (GAIN_ROWS - len(rows))
    return jnp.stack(rows, axis=1).astype(jnp.float32)


def kernel(x, mix_norm_g, w_in, q_norm_g, k_norm_g, rel_bias, sgu_norm_g, w_spatial, b_spatial,
           att_out_norm_g, gmlp_out_norm_g, w_out, ffn_norm_g, w_ffn_in, w_ffn_out):
    depth = w_in.shape[0]
    params = (
        _pack_gains(mix_norm_g, ffn_norm_g, k_norm_g, sgu_norm_g, gmlp_out_norm_g,
                    att_out_norm_g, q_norm_g),
        w_in, _reversed_rel_table(rel_bias), w_spatial,
        jnp.repeat(jnp.swapaxes(b_spatial, 1, 2), GMLP_GROUP_DIM, axis=2),
        w_out, w_ffn_in, w_ffn_out)
    for l in range(depth):
        x = _layer(l, x, *params)
    return x
```

```python
import functools

import jax
import jax.numpy as jnp
from jax import lax
from jax.experimental import pallas as pl
from jax.experimental.pallas import tpu as pltpu

D_MODEL = 1024
CHUNK = 64
ATT_HEADS = 8
HEAD_DIM = 64
ATT_WIDTH = ATT_HEADS * HEAD_DIM
LEFT_CHUNKS = 8
LEFT = LEFT_CHUNKS * CHUNK
MAX_REL = 2 * CHUNK
NEAR = MAX_REL + CHUNK
GMLP_WIDTH = 512
GMLP_GROUPS = 8
GMLP_GROUP_DIM = GMLP_WIDTH // GMLP_GROUPS
GMLP_BLOCK = 128
D_FF = 2816
KUG_WIDTH = ATT_WIDTH + 2 * GMLP_WIDTH
EPS = 1e-6
NEG_INF = -1e30

LANES = 128
BF16_ROWS = 16
MXU_COLS = 256
TQ = 256
TK = LEFT + TQ
PAD_TILES = LEFT // TQ
WIN_TILES = TK // TQ
FF_CHUNK = 256
HEADS_PER_ROUND = 2
STAGE_ROWS = {ATT_WIDTH: 256, D_MODEL: 256, 2 * D_FF: 64}
GAIN_ROWS = 8
STAGE_SLOTS = {ATT_WIDTH: 4, D_MODEL: 4, 2 * D_FF: 3}
VMEM_LIMIT_BYTES = 125 * 512 * 1024


def _rms(x, g):
    ms = jnp.mean(x * x, axis=-1, keepdims=True)
    return (x * lax.rsqrt(ms + EPS)) * g


def _head_rms(z, gain):
    rows = z.shape[0]
    lo = lax.broadcasted_iota(jnp.int32, (rows, LANES), 1) < HEAD_DIM
    outs = []
    for c in range(ATT_WIDTH // LANES):
        zc = z[:, c * LANES:(c + 1) * LANES]
        sq = zc * zc
        ms_lo = jnp.sum(jnp.where(lo, sq, 0.0), axis=-1, keepdims=True) * (1.0 / HEAD_DIM)
        ms_hi = jnp.sum(jnp.where(lo, 0.0, sq), axis=-1, keepdims=True) * (1.0 / HEAD_DIM)
        r = jnp.where(lo, lax.rsqrt(ms_lo + EPS), lax.rsqrt(ms_hi + EPS))
        outs.append((zc * r) * gain[:, c * LANES:(c + 1) * LANES])
    return jnp.concatenate(outs, axis=-1)


def _build_bias(rtab_ref, bias_sc):
    f32 = jnp.float32
    row = lax.broadcasted_iota(jnp.int32, (CHUNK, MXU_COLS), 0)
    lo_half = lax.broadcasted_iota(jnp.int32, (CHUNK, LANES), 1) < CHUNK
    n_kc = TK // CHUNK
    for head in range(ATT_HEADS):
        r = rtab_ref[head:head + 1, :]
        far = rtab_ref[head:head + 1, MXU_COLS - 1:MXU_COLS]
        x0 = jnp.broadcast_to(r, (CHUNK, MXU_COLS))
        for b in range(CHUNK.bit_length() - 1):
            x0 = jnp.where(((row >> b) & 1) == 1, pltpu.roll(x0, 1 << b, axis=1), x0)
        x1 = pltpu.roll(x0, CHUNK, axis=1)
        far_piece = jnp.broadcast_to(far, (CHUNK, LANES))
        neg_piece = jnp.full((CHUNK, LANES), NEG_INF, f32)

        def half(qc, kc):
            rel = kc - qc
            if rel < 0 or rel > LEFT_CHUNKS:
                return neg_piece
            if rel < LEFT_CHUNKS - 2:
                return far_piece
            col = (rel - (LEFT_CHUNKS - 2)) * CHUNK + (qc % 2) * CHUNK
            src = x1 if qc % 2 else x0
            return src[:, (col // LANES) * LANES:(col // LANES + 1) * LANES]

        row_blocks = []
        for qc in range(TQ // CHUNK):
            pieces = [jnp.where(lo_half, half(qc, 2 * vcol), half(qc, 2 * vcol + 1))
                      for vcol in range(n_kc // 2)]
            row_blocks.append(jnp.concatenate(pieces, axis=1))
        bias_sc[head] = jnp.concatenate(row_blocks, axis=0).T


def _step(x, y_prev, t, row0, mixg_ref, w_qv_ref, w_kug_ref, qg_ref, kg_ref, sgug_ref, wsp_ref,
          bsp_ref, ag_ref, gg_ref, w_out_ref, ffng_ref, w1_ref, w2_ref,
          k_sc, v_sc, km_sc, bias_sc, act_sc):
    bf16 = jnp.bfloat16
    f32 = jnp.float32
    nt = (((1,), (1,)), ((), ()))
    do_ffn = y_prev is not None
    h = _rms(x, mixg_ref[...]).astype(bf16)
    if do_ffn:
        hf = _rms(y_prev, ffng_ref[...]).astype(bf16)

    def ffn_chunk(j):
        if not do_ffn:
            return
        gate = jnp.dot(hf, w1_ref[:, j * FF_CHUNK:(j + 1) * FF_CHUNK], preferred_element_type=f32)
        up = jnp.dot(hf, w1_ref[:, D_FF + j * FF_CHUNK:D_FF + (j + 1) * FF_CHUNK],
                     preferred_element_type=f32)
        act_sc[:, j * FF_CHUNK:(j + 1) * FF_CHUNK] = (jax.nn.silu(gate) * up).astype(bf16)

    qv_t = lax.dot_general(w_qv_ref[...], h, nt, preferred_element_type=f32)
    k = jnp.dot(h, w_kug_ref[:, 0:ATT_WIDTH], preferred_element_type=f32)
    u = jnp.dot(h, w_kug_ref[:, ATT_WIDTH:ATT_WIDTH + GMLP_WIDTH], preferred_element_type=f32)
    vg = jnp.dot(h, w_kug_ref[:, ATT_WIDTH + GMLP_WIDTH:KUG_WIDTH], preferred_element_type=f32)

    q3 = qv_t[0:ATT_WIDTH].reshape(ATT_HEADS, HEAD_DIM, TQ)
    q_ms = jnp.mean(q3 * q3, axis=1, keepdims=True)
    q3 = (q3 * lax.rsqrt(q_ms + EPS)) * qg_ref[...].reshape(ATT_HEADS, HEAD_DIM, TQ)
    qn_t = (q3 * HEAD_DIM ** -0.5).reshape(ATT_WIDTH, TQ).astype(bf16)
    kn = _head_rms(k, kg_ref[...]).astype(bf16)
    k_sc[pl.ds(row0 + LEFT, TQ), :] = kn
    v_sc[t + PAD_TILES] = qv_t[ATT_WIDTH:2 * ATT_WIDTH].astype(bf16)
    skip = max(0, PAD_TILES - t) if isinstance(t, int) else 0
    win0, win_rows = skip * TQ, TK - skip * TQ
    kw = k_sc[pl.ds(row0 + win0, win_rows), :]
    km = km_sc[pl.ds(row0 + win0, win_rows), :]
    pad_rows = jnp.where(lax.broadcasted_iota(jnp.int32, (LANES, TQ), 0) == 0,
                         1.0, 0.0).astype(bf16)
    zero_rows = jnp.zeros((HEAD_DIM, TQ), bf16)
    ones_rows = jnp.ones((BF16_ROWS, win_rows), bf16)

    def scores(head):
        c = head // 2
        qh = qn_t[head * HEAD_DIM:(head + 1) * HEAD_DIM]
        pair = [qh, zero_rows] if head % 2 == 0 else [zero_rows, qh]
        rhs = jnp.concatenate(pair + [pad_rows], axis=0)
        lhs = jnp.concatenate([kw[:, c * LANES:(c + 1) * LANES], km], axis=1)
        s = jnp.dot(lhs, rhs, preferred_element_type=f32)
        return s + bias_sc[head, win0:TK, :]

    def probs(s):
        m = jnp.max(s, axis=0, keepdims=True)
        return jnp.exp(s - m).astype(bf16)

    def weighted_values(head, p):
        v_t = jnp.concatenate(
            [v_sc[t + w, head * HEAD_DIM:(head + 1) * HEAD_DIM, :]
             for w in range(skip, WIN_TILES)], axis=1)
        pv = jnp.dot(jnp.concatenate([v_t, ones_rows], axis=0), p,
                     preferred_element_type=f32)
        return pv[0:HEAD_DIM] * (1.0 / pv[HEAD_DIM:HEAD_DIM + 1])

    rounds = [list(range(r, r + HEADS_PER_ROUND)) for r in range(0, ATT_HEADS, HEADS_PER_ROUND)]
    outs = []
    ffn_chunk(0)
    next_chunk = 1
    s_next = [scores(hd) for hd in rounds[0]]
    for ri, heads in enumerate(rounds):
        s_cur = s_next
        if ri + 1 < len(rounds):
            s_next = [scores(hd) for hd in rounds[ri + 1]]
        for _ in heads:
            ffn_chunk(next_chunk)
            next_chunk += 1
        ps = [probs(s) for s in s_cur]
        outs += [weighted_values(hd, p) for hd, p in zip(heads, ps)]
    a_t = jnp.concatenate(outs, axis=0)
    a_ms = jnp.mean(a_t * a_t, axis=0, keepdims=True)
    a_n = ((a_t * lax.rsqrt(a_ms + EPS)) * ag_ref[...]).T.astype(bf16)

    u_act = jax.nn.gelu(u)
    vgn = _rms(jax.nn.gelu(vg), sgug_ref[...]).astype(bf16)
    ti = lax.broadcasted_iota(jnp.int32, (GMLP_GROUPS, GMLP_BLOCK, GMLP_BLOCK), 1)
    si = lax.broadcasted_iota(jnp.int32, (GMLP_GROUPS, GMLP_BLOCK, GMLP_BLOCK), 2)
    causal = (ti // CHUNK) >= (si // CHUNK)
    wsp = jnp.where(causal, wsp_ref[...], 0.0).astype(bf16)
    wsp = wsp.reshape(GMLP_GROUPS * GMLP_BLOCK, GMLP_BLOCK)
    first_of_pair = lax.broadcasted_iota(jnp.int32, (GMLP_BLOCK, LANES), 1) < GMLP_GROUP_DIM
    n_blk = TQ // GMLP_BLOCK
    pair_cols = []
    for pr in range(GMLP_WIDTH // LANES):
        lhs = wsp[2 * pr * GMLP_BLOCK:2 * (pr + 1) * GMLP_BLOCK, :]
        rhs = jnp.concatenate(
            [vgn[blk * GMLP_BLOCK:(blk + 1) * GMLP_BLOCK, pr * LANES:(pr + 1) * LANES]
             for blk in range(n_blk)], axis=1)
        res = jnp.dot(lhs, rhs, preferred_element_type=f32)
        pair_cols.append(jnp.concatenate(
            [jnp.where(first_of_pair,
                       res[0:GMLP_BLOCK, blk * LANES:(blk + 1) * LANES],
                       res[GMLP_BLOCK:2 * GMLP_BLOCK, blk * LANES:(blk + 1) * LANES])
             for blk in range(n_blk)], axis=0))
        if next_chunk < D_FF // FF_CHUNK:
            ffn_chunk(next_chunk)
            next_chunk += 1
    while next_chunk < D_FF // FF_CHUNK:
        ffn_chunk(next_chunk)
        next_chunk += 1
    mixed = (jnp.concatenate(pair_cols, axis=1)
             + jnp.concatenate([bsp_ref[...]] * n_blk, axis=0))
    g_n = _rms(u_act * mixed, gg_ref[...]).astype(bf16)

    ffn_out = None
    if do_ffn:
        ffn_out = y_prev + jnp.dot(act_sc[...], w2_ref[...], preferred_element_type=f32)
    mix = jnp.concatenate([a_n, g_n], axis=-1)
    mix_out = x + jnp.dot(mix, w_out_ref[...], preferred_element_type=f32)
    return mix_out, ffn_out


def _weight_loader(src, dst, ring, src_col=0, dst_col=0, dst_row=None):
    stage, sem = ring
    rows = src.shape[0]
    slots, chunk, cols = stage.shape
    assert rows % chunk == 0 and chunk == STAGE_ROWS[cols] and slots == STAGE_SLOTS[cols]
    n = rows // chunk
    ahead = slots - 1
    assert n >= ahead

    def copy(i, slot):
        return pltpu.make_async_copy(src.at[pl.ds(i * chunk, chunk), pl.ds(src_col, cols)],
                                     stage.at[slot], sem.at[slot])

    def prefetch():
        for i in range(ahead):
            copy(i, i).start()

    def drain_natural():
        def body(i, carry):
            slot = lax.rem(i, slots)

            @pl.when(i + ahead < n)
            def _():
                copy(i + ahead, lax.rem(i + ahead, slots)).start()

            copy(i, slot).wait()
            dst[pl.ds(pl.multiple_of(i * chunk, chunk), chunk), dst_col:dst_col + cols] = (
                stage[slot].astype(dst.dtype))
            return carry

        lax.fori_loop(0, n, body, 0)

    def drain_transposed():
        assert chunk % LANES == 0
        for i in range(n):
            if i + ahead < n:
                copy(i + ahead, (i + ahead) % slots).start()
            copy(i, i % slots).wait()
            dst[dst_row:dst_row + cols, i * chunk:(i + 1) * chunk] = (
                stage[i % slots].T.astype(dst.dtype))

    return prefetch, (drain_natural if dst_row is None else drain_transposed)


def _layer_kernel(layer, tiles_per_row, n_tiles,
                  x0_ref, x_ref, vecs_ref, w_in_hbm, rtab_ref, wsp_ref, bsp_ref,
                  w_out_hbm, w1_hbm, w2_hbm,
                  o_ref, k_sc, v_sc, km_sc, bias_sc, y_sc, act_sc, qg_sc, ag_sc,
                  w_qv_sc, w_kug_sc, w_out_sc, w1_sc, w2_sc, st_k, st_d, st_ff,
                  sem_k, sem_d, sem_ff):
    g = pl.program_id(0)
    mixg_ref = vecs_ref.at[0:1, :]
    ffng_ref = vecs_ref.at[1:2, :]
    kg_ref = vecs_ref.at[2:3, 0:ATT_WIDTH]
    sgug_ref = vecs_ref.at[2:3, ATT_WIDTH:ATT_WIDTH + GMLP_WIDTH]
    gg_ref = vecs_ref.at[3:4, 0:GMLP_WIDTH]
    t = lax.rem(jnp.minimum(g + 1, n_tiles - 1), tiles_per_row)
    row0 = pl.multiple_of(t * TQ, TQ)

    def mix_and_ffn(x, y_prev, t_x, row0_x):
        return _step(x, y_prev, t_x, row0_x, mixg_ref, w_qv_sc, w_kug_sc, qg_sc, kg_ref, sgug_ref,
                     wsp_ref, bsp_ref, ag_sc, gg_ref, w_out_sc, ffng_ref, w1_sc, w2_sc,
                     k_sc, v_sc, km_sc, bias_sc, act_sc)

    def zero_left_context():
        k_sc[0:LEFT, :] = jnp.zeros((LEFT, ATT_WIDTH), k_sc.dtype)
        v_sc[0:PAD_TILES] = jnp.zeros((PAD_TILES, ATT_WIDTH, TQ), v_sc.dtype)

    @pl.when(g == 0)
    def _():
        w_in_l = w_in_hbm.at[layer]
        ring_k, ring_d, ring_ff = (st_k, sem_k), (st_d, sem_d), (st_ff, sem_ff)
        loads = [
            _weight_loader(w_in_l, w_qv_sc, ring_k, src_col=0, dst_row=0),
            _weight_loader(w_in_l, w_kug_sc, ring_d, src_col=3 * ATT_WIDTH, dst_col=ATT_WIDTH),
            _weight_loader(w_in_l, w_qv_sc, ring_k, src_col=2 * ATT_WIDTH, dst_row=ATT_WIDTH),
            _weight_loader(w_out_hbm.at[layer], w_out_sc, ring_d),
            _weight_loader(w_in_l, w_kug_sc, ring_k, src_col=ATT_WIDTH, dst_col=0),
            _weight_loader(w1_hbm.at[layer], w1_sc, ring_ff),
            _weight_loader(w2_hbm.at[layer], w2_sc, ring_d),
        ]
        loads[0][0]()
        for i, (_, drain) in enumerate(loads):
            if i + 1 < len(loads):
                loads[i + 1][0]()
            drain()
        _build_bias(rtab_ref, bias_sc)
        qg_sc[...] = jnp.broadcast_to(vecs_ref[4:5, 0:ATT_WIDTH], (TQ, ATT_WIDTH)).T
        ag_sc[...] = jnp.broadcast_to(vecs_ref[3:4, GMLP_WIDTH:GMLP_WIDTH + ATT_WIDTH],
                                      (TQ, ATT_WIDTH)).T
        is_pad = ((lax.broadcasted_iota(jnp.int32, km_sc.shape, 0) < LEFT)
                  & (lax.broadcasted_iota(jnp.int32, km_sc.shape, 1) == 0))
        km_sc[...] = jnp.where(is_pad, NEG_INF, 0.0).astype(km_sc.dtype)
        zero_left_context()
        y_sc[...] = mix_and_ffn(x0_ref[...], None, 0, 0)[0]

    pl.when(t == 0)(zero_left_context)

    mix_out, ffn_out = mix_and_ffn(x_ref[...], y_sc[...], t, row0)
    o_ref[...] = ffn_out
    y_sc[...] = mix_out


def _layer_spec(layer, shape):
    zeros = (0,) * len(shape)
    return pl.BlockSpec((None,) + shape, lambda g: (layer,) + zeros,
                        pipeline_mode=pl.Buffered(1))


def _layer(layer, x, vecs, w_in, rtab, wsp, bsp, w_out, w1, w2):
    B, S, D = x.shape
    tiles_per_row = S // TQ
    n_tiles = B * tiles_per_row
    bf16 = jnp.bfloat16
    f32 = jnp.float32

    def next_tile(g):
        gi = jnp.minimum(g + 1, n_tiles - 1)
        return (gi // tiles_per_row, gi % tiles_per_row, 0)

    def out_tile(g):
        return (g // tiles_per_row, g % tiles_per_row, 0)

    hbm = pl.BlockSpec(memory_space=pl.ANY)
    return pl.pallas_call(
        functools.partial(_layer_kernel, layer, tiles_per_row, n_tiles),
        grid=(n_tiles,),
        in_specs=[
            pl.BlockSpec((None, TQ, D), lambda g: (0, 0, 0), pipeline_mode=pl.Buffered(1)),
            pl.BlockSpec((None, TQ, D), next_tile),
            _layer_spec(layer, (GAIN_ROWS, D)),
            hbm,
            _layer_spec(layer, (ATT_HEADS, MXU_COLS)),
            _layer_spec(layer, (GMLP_GROUPS, GMLP_BLOCK, GMLP_BLOCK)),
            _layer_spec(layer, (GMLP_BLOCK, GMLP_WIDTH)),
            hbm,
            hbm,
            hbm,
        ],
        out_specs=pl.BlockSpec((None, TQ, D), out_tile),
        out_shape=jax.ShapeDtypeStruct(x.shape, x.dtype),
        scratch_shapes=[
            pltpu.VMEM((LEFT + S, ATT_WIDTH), bf16),
            pltpu.VMEM((PAD_TILES + tiles_per_row, ATT_WIDTH, TQ), bf16),
            pltpu.VMEM((LEFT + S, LANES), bf16),
            pltpu.VMEM((ATT_HEADS, TK, TQ), f32),
            pltpu.VMEM((TQ, D), f32),
            pltpu.VMEM((TQ, D_FF), bf16),
            pltpu.VMEM((ATT_WIDTH, TQ), f32),
            pltpu.VMEM((ATT_WIDTH, TQ), f32),
            pltpu.VMEM((2 * ATT_WIDTH, D), bf16),
            pltpu.VMEM((D, KUG_WIDTH), bf16),
            pltpu.VMEM((D, D), bf16),
            pltpu.VMEM((D, 2 * D_FF), bf16),
            pltpu.VMEM((D_FF, D), bf16),
            *[pltpu.VMEM((STAGE_SLOTS[w], STAGE_ROWS[w], w), f32)
              for w in (ATT_WIDTH, D, 2 * D_FF)],
            *[pltpu.SemaphoreType.DMA((STAGE_SLOTS[w],))
              for w in (ATT_WIDTH, D, 2 * D_FF)],
        ],
        compiler_params=pltpu.CompilerParams(
            dimension_semantics=("arbitrary",),
            vmem_limit_bytes=VMEM_LIMIT_BYTES),
        name="layer",
    )(x, x, vecs, w_in, rtab, wsp, bsp, w_out, w1, w2)


def _reversed_rel_table(rel_bias):
    near = rel_bias[:, :, ::-1][:, :, :NEAR]
    far = jnp.broadcast_to(rel_bias[:, :, 2 * MAX_REL:], near.shape[:2] + (MXU_COLS - NEAR,))
    return jnp.concatenate([near, far], axis=2).astype(jnp.float32)


def _pack_gains(mix_norm_g, ffn_norm_g, k_norm_g, sgu_norm_g, gmlp_out_norm_g, att_out_norm_g,
                q_norm_g):
    depth = mix_norm_g.shape[0]
    tiled = lambda g: jnp.tile(g, (1, ATT_HEADS))
    rows = [mix_norm_g, ffn_norm_g,
            jnp.concatenate([tiled(k_norm_g), sgu_norm_g], axis=1),
            jnp.concatenate([gmlp_out_norm_g, att_out_norm_g], axis=1),
            jnp.concatenate([tiled(q_norm_g), jnp.zeros((depth, D_MODEL - ATT_WIDTH))], axis=1)]
    rows += [jnp.zeros((depth, D_MODEL))] * (GAIN_ROWS - len(rows))
    return jnp.stack(rows, axis=1).astype(jnp.float32)


def kernel(x, mix_norm_g, w_in, q_norm_g, k_norm_g, rel_bias, sgu_norm_g, w_spatial, b_spatial,
           att_out_norm_g, gmlp_out_norm_g, w_out, ffn_norm_g, w_ffn_in, w_ffn_out):
    depth = w_in.shape[0]
    params = (
        _pack_gains(mix_norm_g, ffn_norm_g, k_norm_g, sgu_norm_g, gmlp_out_norm_g,
                    att_out_norm_g, q_norm_g),
        w_in, _reversed_rel_table(rel_bias), w_spatial,
        jnp.repeat(jnp.swapaxes(b_spatial, 1, 2), GMLP_GROUP_DIM, axis=2),
        w_out, w_ffn_in, w_ffn_out)
    for l in range(depth):
        x = _layer(l, x, *params)
    return x
```

```python
import functools

import jax
import jax.numpy as jnp
from jax import lax
from jax.experimental import pallas as pl
from jax.experimental.pallas import tpu as pltpu

D_MODEL = 1024
CHUNK = 64
ATT_HEADS = 8
HEAD_DIM = 64
ATT_WIDTH = ATT_HEADS * HEAD_DIM
LEFT_CHUNKS = 8
LEFT = LEFT_CHUNKS * CHUNK
MAX_REL = 2 * CHUNK
NEAR = MAX_REL + CHUNK
GMLP_WIDTH = 512
GMLP_GROUPS = 8
GMLP_GROUP_DIM = GMLP_WIDTH // GMLP_GROUPS
GMLP_BLOCK = 128
D_FF = 2816
KUG_WIDTH = ATT_WIDTH + 2 * GMLP_WIDTH
EPS = 1e-6
NEG_INF = -1e30

LANES = 128
BF16_ROWS = 16
MXU_COLS = 256
TQ = 256
TK = LEFT + TQ
PAD_TILES = LEFT // TQ
WIN_TILES = TK // TQ
FF_CHUNK = 256
HEADS_PER_ROUND = 2
STAGE_ROWS = {ATT_WIDTH: 256, D_MODEL: 256, 2 * D_FF: 32}
GAIN_ROWS = 8
STAGE_SLOTS = 4
VMEM_LIMIT_BYTES = 61 * 1024 * 1024


def _rms(x, g):
    ms = jnp.mean(x * x, axis=-1, keepdims=True)
    return (x * lax.rsqrt(ms + EPS)) * g


def _head_rms(z, gain):
    rows = z.shape[0]
    lo = lax.broadcasted_iota(jnp.int32, (rows, LANES), 1) < HEAD_DIM
    outs = []
    for c in range(ATT_WIDTH // LANES):
        zc = z[:, c * LANES:(c + 1) * LANES]
        sq = zc * zc
        ms_lo = jnp.sum(jnp.where(lo, sq, 0.0), axis=-1, keepdims=True) * (1.0 / HEAD_DIM)
        ms_hi = jnp.sum(jnp.where(lo, 0.0, sq), axis=-1, keepdims=True) * (1.0 / HEAD_DIM)
        r = jnp.where(lo, lax.rsqrt(ms_lo + EPS), lax.rsqrt(ms_hi + EPS))
        outs.append((zc * r) * gain[:, c * LANES:(c + 1) * LANES])
    return jnp.concatenate(outs, axis=-1)


def _build_bias(rtab_ref, bias_sc):
    f32 = jnp.float32
    row = lax.broadcasted_iota(jnp.int32, (CHUNK, MXU_COLS), 0)
    lo_half = lax.broadcasted_iota(jnp.int32, (CHUNK, LANES), 1) < CHUNK
    n_kc = TK // CHUNK
    for head in range(ATT_HEADS):
        r = rtab_ref[head:head + 1, :]
        far = rtab_ref[head:head + 1, MXU_COLS - 1:MXU_COLS]
        x0 = jnp.broadcast_to(r, (CHUNK, MXU_COLS))
        for b in range(CHUNK.bit_length() - 1):
            x0 = jnp.where(((row >> b) & 1) == 1, pltpu.roll(x0, 1 << b, axis=1), x0)
        x1 = pltpu.roll(x0, CHUNK, axis=1)
        far_piece = jnp.broadcast_to(far, (CHUNK, LANES))
        neg_piece = jnp.full((CHUNK, LANES), NEG_INF, f32)

        def half(qc, kc):
            rel = kc - qc
            if rel < 0 or rel > LEFT_CHUNKS:
                return neg_piece
            if rel < LEFT_CHUNKS - 2:
                return far_piece
            col = (rel - (LEFT_CHUNKS - 2)) * CHUNK + (qc % 2) * CHUNK
            src = x1 if qc % 2 else x0
            return src[:, (col // LANES) * LANES:(col // LANES + 1) * LANES]

        row_blocks = []
        for qc in range(TQ // CHUNK):
            pieces = [jnp.where(lo_half, half(qc, 2 * vcol), half(qc, 2 * vcol + 1))
                      for vcol in range(n_kc // 2)]
            row_blocks.append(jnp.concatenate(pieces, axis=1))
        bias_sc[head] = jnp.concatenate(row_blocks, axis=0).T


def _step(x, y_prev, t, row0, mixg_ref, w_qv_ref, w_kug_ref, qg_ref, kg_ref, sgug_ref, wsp_ref,
          bsp_ref, ag_ref, gg_ref, w_out_ref, ffng_ref, w1_ref, w2_ref,
          k_sc, v_sc, km_sc, bias_sc, act_sc):
    bf16 = jnp.bfloat16
    f32 = jnp.float32
    nt = (((1,), (1,)), ((), ()))
    do_ffn = y_prev is not None
    h = _rms(x, mixg_ref[...]).astype(bf16)
    if do_ffn:
        hf = _rms(y_prev, ffng_ref[...]).astype(bf16)

    def ffn_chunk(j):
        if not do_ffn:
            return
        gate = jnp.dot(hf, w1_ref[:, j * FF_CHUNK:(j + 1) * FF_CHUNK], preferred_element_type=f32)
        up = jnp.dot(hf, w1_ref[:, D_FF + j * FF_CHUNK:D_FF + (j + 1) * FF_CHUNK],
                     preferred_element_type=f32)
        act_sc[:, j * FF_CHUNK:(j + 1) * FF_CHUNK] = (jax.nn.silu(gate) * up).astype(bf16)

    qv_t = lax.dot_general(w_qv_ref[...], h, nt, preferred_element_type=f32)
    k = jnp.dot(h, w_kug_ref[:, 0:ATT_WIDTH], preferred_element_type=f32)
    u = jnp.dot(h, w_kug_ref[:, ATT_WIDTH:ATT_WIDTH + GMLP_WIDTH], preferred_element_type=f32)
    vg = jnp.dot(h, w_kug_ref[:, ATT_WIDTH + GMLP_WIDTH:KUG_WIDTH], preferred_element_type=f32)

    q3 = qv_t[0:ATT_WIDTH].reshape(ATT_HEADS, HEAD_DIM, TQ)
    q_ms = jnp.mean(q3 * q3, axis=1, keepdims=True)
    q3 = (q3 * lax.rsqrt(q_ms + EPS)) * qg_ref[...].reshape(ATT_HEADS, HEAD_DIM, TQ)
    qn_t = (q3 * HEAD_DIM ** -0.5).reshape(ATT_WIDTH, TQ).astype(bf16)
    kn = _head_rms(k, kg_ref[...]).astype(bf16)
    k_sc[pl.ds(row0 + LEFT, TQ), :] = kn
    v_sc[t + PAD_TILES] = qv_t[ATT_WIDTH:2 * ATT_WIDTH].astype(bf16)
    skip = max(0, PAD_TILES - t) if isinstance(t, int) else 0
    win0, win_rows = skip * TQ, TK - skip * TQ
    kw = k_sc[pl.ds(row0 + win0, win_rows), :]
    km = km_sc[pl.ds(row0 + win0, win_rows), :]
    pad_rows = jnp.where(lax.broadcasted_iota(jnp.int32, (LANES, TQ), 0) == 0,
                         1.0, 0.0).astype(bf16)
    zero_rows = jnp.zeros((HEAD_DIM, TQ), bf16)
    ones_rows = jnp.ones((BF16_ROWS, win_rows), bf16)

    def scores(head):
        c = head // 2
        qh = qn_t[head * HEAD_DIM:(head + 1) * HEAD_DIM]
        pair = [qh, zero_rows] if head % 2 == 0 else [zero_rows, qh]
        rhs = jnp.concatenate(pair + [pad_rows], axis=0)
        lhs = jnp.concatenate([kw[:, c * LANES:(c + 1) * LANES], km], axis=1)
        s = jnp.dot(lhs, rhs, preferred_element_type=f32)
        return s + bias_sc[head, win0:TK, :]

    def probs(s):
        m = jnp.max(s, axis=0, keepdims=True)
        return jnp.exp(s - m).astype(bf16)

    def weighted_values(head, p):
        v_t = jnp.concatenate(
            [v_sc[t + w, head * HEAD_DIM:(head + 1) * HEAD_DIM, :]
             for w in range(skip, WIN_TILES)], axis=1)
        pv = jnp.dot(jnp.concatenate([v_t, ones_rows], axis=0), p,
                     preferred_element_type=f32)
        return pv[0:HEAD_DIM] * (1.0 / pv[HEAD_DIM:HEAD_DIM + 1])

    rounds = [list(range(r, r + HEADS_PER_ROUND)) for r in range(0, ATT_HEADS, HEADS_PER_ROUND)]
    outs = []
    ffn_chunk(0)
    next_chunk = 1
    s_next = [scores(hd) for hd in rounds[0]]
    for ri, heads in enumerate(rounds):
        s_cur = s_next
        if ri + 1 < len(rounds):
            s_next = [scores(hd) for hd in rounds[ri + 1]]
        for _ in heads:
            ffn_chunk(next_chunk)
            next_chunk += 1
        ps = [probs(s) for s in s_cur]
        outs += [weighted_values(hd, p) for hd, p in zip(heads, ps)]
    a_t = jnp.concatenate(outs, axis=0)
    a_ms = jnp.mean(a_t * a_t, axis=0, keepdims=True)
    a_n = ((a_t * lax.rsqrt(a_ms + EPS)) * ag_ref[...]).T.astype(bf16)

    u_act = jax.nn.gelu(u)
    vgn = _rms(jax.nn.gelu(vg), sgug_ref[...]).astype(bf16)
    ti = lax.broadcasted_iota(jnp.int32, (GMLP_GROUPS, GMLP_BLOCK, GMLP_BLOCK), 1)
    si = lax.broadcasted_iota(jnp.int32, (GMLP_GROUPS, GMLP_BLOCK, GMLP_BLOCK), 2)
    causal = (ti // CHUNK) >= (si // CHUNK)
    wsp = jnp.where(causal, wsp_ref[...], 0.0).astype(bf16)
    wsp = wsp.reshape(GMLP_GROUPS * GMLP_BLOCK, GMLP_BLOCK)
    first_of_pair = lax.broadcasted_iota(jnp.int32, (GMLP_BLOCK, LANES), 1) < GMLP_GROUP_DIM
    n_blk = TQ // GMLP_BLOCK
    pair_cols = []
    for pr in range(GMLP_WIDTH // LANES):
        lhs = wsp[2 * pr * GMLP_BLOCK:2 * (pr + 1) * GMLP_BLOCK, :]
        rhs = jnp.concatenate(
            [vgn[blk * GMLP_BLOCK:(blk + 1) * GMLP_BLOCK, pr * LANES:(pr + 1) * LANES]
             for blk in range(n_blk)], axis=1)
        res = jnp.dot(lhs, rhs, preferred_element_type=f32)
        pair_cols.append(jnp.concatenate(
            [jnp.where(first_of_pair,
                       res[0:GMLP_BLOCK, blk * LANES:(blk + 1) * LANES],
                       res[GMLP_BLOCK:2 * GMLP_BLOCK, blk * LANES:(blk + 1) * LANES])
             for blk in range(n_blk)], axis=0))
        if next_chunk < D_FF // FF_CHUNK:
            ffn_chunk(next_chunk)
            next_chunk += 1
    while next_chunk < D_FF // FF_CHUNK:
        ffn_chunk(next_chunk)
        next_chunk += 1
    mixed = (jnp.concatenate(pair_cols, axis=1)
             + jnp.concatenate([bsp_ref[...]] * n_blk, axis=0))
    g_n = _rms(u_act * mixed, gg_ref[...]).astype(bf16)

    ffn_out = None
    if do_ffn:
        ffn_out = y_prev + jnp.dot(act_sc[...], w2_ref[...], preferred_element_type=f32)
    mix = jnp.concatenate([a_n, g_n], axis=-1)
    mix_out = x + jnp.dot(mix, w_out_ref[...], preferred_element_type=f32)
    return mix_out, ffn_out


def _weight_loader(src, dst, ring, src_col=0, dst_col=0, dst_row=None):
    stage, sem, priority = ring
    rows = src.shape[0]
    _, chunk, cols = stage.shape
    assert rows % chunk == 0 and chunk == STAGE_ROWS[cols] and stage.shape[0] == STAGE_SLOTS
    n = rows // chunk
    ahead = STAGE_SLOTS - 1
    assert n >= ahead

    def copy(i, slot):
        return pltpu.make_async_copy(src.at[pl.ds(i * chunk, chunk), pl.ds(src_col, cols)],
                                     stage.at[slot], sem.at[slot])

    def prefetch():
        for i in range(ahead):
            copy(i, i).start(priority=priority)

    def drain_natural():
        def body(i, carry):
            slot = lax.rem(i, STAGE_SLOTS)

            @pl.when(i + ahead < n)
            def _():
                copy(i + ahead, lax.rem(i + ahead, STAGE_SLOTS)).start(priority=priority)

            copy(i, slot).wait()
            dst[pl.ds(pl.multiple_of(i * chunk, chunk), chunk), dst_col:dst_col + cols] = (
                stage[slot].astype(dst.dtype))
            return carry

        lax.fori_loop(0, n, body, 0)

    def drain_transposed():
        assert chunk % LANES == 0
        for i in range(n):
            if i + ahead < n:
                copy(i + ahead, (i + ahead) % STAGE_SLOTS).start(priority=priority)
            copy(i, i % STAGE_SLOTS).wait()
            dst[dst_row:dst_row + cols, i * chunk:(i + 1) * chunk] = (
                stage[i % STAGE_SLOTS].T.astype(dst.dtype))

    return prefetch, (drain_natural if dst_row is None else drain_transposed)


def _layer_kernel(layer, tiles_per_row, n_tiles,
                  x0_ref, x_ref, vecs_ref, w_in_hbm, rtab_ref, wsp_ref, bsp_ref,
                  w_out_hbm, w1_hbm, w2_hbm,
                  o_ref, k_sc, v_sc, km_sc, bias_sc, y_sc, act_sc, qg_sc, ag_sc,
                  w_qv_sc, w_kug_sc, w_out_sc, w1_sc, w2_sc, st_k, st_d, st_ff,
                  sem_k, sem_d, sem_ff):
    g = pl.program_id(0)
    mixg_ref = vecs_ref.at[0:1, :]
    ffng_ref = vecs_ref.at[1:2, :]
    kg_ref = vecs_ref.at[2:3, 0:ATT_WIDTH]
    sgug_ref = vecs_ref.at[2:3, ATT_WIDTH:ATT_WIDTH + GMLP_WIDTH]
    gg_ref = vecs_ref.at[3:4, 0:GMLP_WIDTH]
    t = lax.rem(jnp.minimum(g + 1, n_tiles - 1), tiles_per_row)
    row0 = pl.multiple_of(t * TQ, TQ)

    def mix_and_ffn(x, y_prev, t_x, row0_x):
        return _step(x, y_prev, t_x, row0_x, mixg_ref, w_qv_sc, w_kug_sc, qg_sc, kg_ref, sgug_ref,
                     wsp_ref, bsp_ref, ag_sc, gg_ref, w_out_sc, ffng_ref, w1_sc, w2_sc,
                     k_sc, v_sc, km_sc, bias_sc, act_sc)

    def zero_left_context():
        k_sc[0:LEFT, :] = jnp.zeros((LEFT, ATT_WIDTH), k_sc.dtype)
        v_sc[0:PAD_TILES] = jnp.zeros((PAD_TILES, ATT_WIDTH, TQ), v_sc.dtype)

    @pl.when(g == 0)
    def _():
        w_in_l = w_in_hbm.at[layer]
        ring_k, ring_d, ring_ff = (st_k, sem_k, 0), (st_d, sem_d, 1), (st_ff, sem_ff, 0)
        loads = [
            _weight_loader(w_in_l, w_qv_sc, ring_k, src_col=0, dst_row=0),
            _weight_loader(w_in_l, w_kug_sc, ring_d, src_col=3 * ATT_WIDTH, dst_col=ATT_WIDTH),
            _weight_loader(w_in_l, w_qv_sc, ring_k, src_col=2 * ATT_WIDTH, dst_row=ATT_WIDTH),
            _weight_loader(w_out_hbm.at[layer], w_out_sc, ring_d),
            _weight_loader(w_in_l, w_kug_sc, ring_k, src_col=ATT_WIDTH, dst_col=0),
            _weight_loader(w1_hbm.at[layer], w1_sc, ring_ff),
            _weight_loader(w2_hbm.at[layer], w2_sc, ring_d),
        ]
        loads[0][0]()
        for i, (_, drain) in enumerate(loads):
            if i + 1 < len(loads):
                loads[i + 1][0]()
            drain()
        _build_bias(rtab_ref, bias_sc)
        qg_sc[...] = jnp.broadcast_to(vecs_ref[4:5, 0:ATT_WIDTH], (TQ, ATT_WIDTH)).T
        ag_sc[...] = jnp.broadcast_to(vecs_ref[3:4, GMLP_WIDTH:GMLP_WIDTH + ATT_WIDTH],
                                      (TQ, ATT_WIDTH)).T
        is_pad = ((lax.broadcasted_iota(jnp.int32, km_sc.shape, 0) < LEFT)
                  & (lax.broadcasted_iota(jnp.int32, km_sc.shape, 1) == 0))
        km_sc[...] = jnp.where(is_pad, NEG_INF, 0.0).astype(km_sc.dtype)
        zero_left_context()
        y_sc[...] = mix_and_ffn(x0_ref[...], None, 0, 0)[0]

    pl.when(t == 0)(zero_left_context)

    mix_out, ffn_out = mix_and_ffn(x_ref[...], y_sc[...], t, row0)
    o_ref[...] = ffn_out
    y_sc[...] = mix_out


def _layer_spec(layer, shape):
    zeros = (0,) * len(shape)
    return pl.BlockSpec((None,) + shape, lambda g: (layer,) + zeros,
                        pipeline_mode=pl.Buffered(1))


def _layer(layer, x, vecs, w_in, rtab, wsp, bsp, w_out, w1, w2):
    B, S, D = x.shape
    tiles_per_row = S // TQ
    n_tiles = B * tiles_per_row
    bf16 = jnp.bfloat16
    f32 = jnp.float32

    def next_tile(g):
        gi = jnp.minimum(g + 1, n_tiles - 1)
        return (gi // tiles_per_row, gi % tiles_per_row, 0)

    def out_tile(g):
        return (g // tiles_per_row, g % tiles_per_row, 0)

    hbm = pl.BlockSpec(memory_space=pl.ANY)
    return pl.pallas_call(
        functools.partial(_layer_kernel, layer, tiles_per_row, n_tiles),
        grid=(n_tiles,),
        in_specs=[
            pl.BlockSpec((None, TQ, D), lambda g: (0, 0, 0), pipeline_mode=pl.Buffered(1)),
            pl.BlockSpec((None, TQ, D), next_tile),
            _layer_spec(layer, (GAIN_ROWS, D)),
            hbm,
            _layer_spec(layer, (ATT_HEADS, MXU_COLS)),
            _layer_spec(layer, (GMLP_GROUPS, GMLP_BLOCK, GMLP_BLOCK)),
            _layer_spec(layer, (GMLP_BLOCK, GMLP_WIDTH)),
            hbm,
            hbm,
            hbm,
        ],
        out_specs=pl.BlockSpec((None, TQ, D), out_tile),
        out_shape=jax.ShapeDtypeStruct(x.shape, x.dtype),
        scratch_shapes=[
            pltpu.VMEM((LEFT + S, ATT_WIDTH), bf16),
            pltpu.VMEM((PAD_TILES + tiles_per_row, ATT_WIDTH, TQ), bf16),
            pltpu.VMEM((LEFT + S, LANES), bf16),
            pltpu.VMEM((ATT_HEADS, TK, TQ), f32),
            pltpu.VMEM((TQ, D), f32),
            pltpu.VMEM((TQ, D_FF), bf16),
            pltpu.VMEM((ATT_WIDTH, TQ), f32),
            pltpu.VMEM((ATT_WIDTH, TQ), f32),
            pltpu.VMEM((2 * ATT_WIDTH, D), bf16),
            pltpu.VMEM((D, KUG_WIDTH), bf16),
            pltpu.VMEM((D, D), bf16),
            pltpu.VMEM((D, 2 * D_FF), bf16),
            pltpu.VMEM((D_FF, D), bf16),
            pltpu.VMEM((STAGE_SLOTS, STAGE_ROWS[ATT_WIDTH], ATT_WIDTH), f32),
            pltpu.VMEM((STAGE_SLOTS, STAGE_ROWS[D], D), f32),
            pltpu.VMEM((STAGE_SLOTS, STAGE_ROWS[2 * D_FF], 2 * D_FF), f32),
            pltpu.SemaphoreType.DMA((STAGE_SLOTS,)),
            pltpu.SemaphoreType.DMA((STAGE_SLOTS,)),
            pltpu.SemaphoreType.DMA((STAGE_SLOTS,)),
        ],
        compiler_params=pltpu.CompilerParams(
            dimension_semantics=("arbitrary",),
            vmem_limit_bytes=VMEM_LIMIT_BYTES),
        name="layer",
    )(x, x, vecs, w_in, rtab, wsp, bsp, w_out, w1, w2)


def _reversed_rel_table(rel_bias):
    near = rel_bias[:, :, ::-1][:, :, :NEAR]
    far = jnp.broadcast_to(rel_bias[:, :, 2 * MAX_REL:], near.shape[:2] + (MXU_COLS - NEAR,))
    return jnp.concatenate([near, far], axis=2).astype(jnp.float32)


def _pack_gains(mix_norm_g, ffn_norm_g, k_norm_g, sgu_norm_g, gmlp_out_norm_g, att_out_norm_g,
                q_norm_g):
    depth = mix_norm_g.shape[0]
    tiled = lambda g: jnp.tile(g, (1, ATT_HEADS))
    rows = [mix_norm_g, ffn_norm_g,
            jnp.concatenate([tiled(k_norm_g), sgu_norm_g], axis=1),
            jnp.concatenate([gmlp_out_norm_g, att_out_norm_g], axis=1),
            jnp.concatenate([tiled(q_norm_g), jnp.zeros((depth, D_MODEL - ATT_WIDTH))], axis=1)]
    rows += [jnp.zeros((depth, D_MODEL))] * (GAIN_ROWS - len(rows))
    return jnp.stack(rows, axis=1).astype(jnp.float32)


def kernel(x, mix_norm_g, w_in, q_norm_g, k_norm_g, rel_bias, sgu_norm_g, w_spatial, b_spatial,
           att_out_norm_g, gmlp_out_norm_g, w_out, ffn_norm_g, w_ffn_in, w_ffn_out):
    depth = w_in.shape[0]
    params = (
        _pack_gains(mix_norm_g, ffn_norm_g, k_norm_g, sgu_norm_g, gmlp_out_norm_g,
                    att_out_norm_g, q_norm_g),
        w_in, _reversed_rel_table(rel_bias), w_spatial,
        jnp.repeat(jnp.swapaxes(b_spatial, 1, 2), GMLP_GROUP_DIM, axis=2),
        w_out, w_ffn_in, w_ffn_out)
    for l in range(depth):
        x = _layer(l, x, *params)
    return x
```

```python
import functools

import jax
import jax.numpy as jnp
from jax import lax
from jax.experimental import pallas as pl
from jax.experimental.pallas import tpu as pltpu

D_MODEL = 1024
CHUNK = 64
ATT_HEADS = 8
HEAD_DIM = 64
ATT_WIDTH = ATT_HEADS * HEAD_DIM
LEFT_CHUNKS = 8
LEFT = LEFT_CHUNKS * CHUNK
MAX_REL = 2 * CHUNK
NEAR = MAX_REL + CHUNK
GMLP_WIDTH = 512
GMLP_GROUPS = 8
GMLP_GROUP_DIM = GMLP_WIDTH // GMLP_GROUPS
GMLP_BLOCK = 128
D_FF = 2816
KUG_WIDTH = ATT_WIDTH + 2 * GMLP_WIDTH
EPS = 1e-6
NEG_INF = -1e30

LANES = 128
BF16_ROWS = 16
MXU_COLS = 256
TQ = 256
TK = LEFT + TQ
PAD_TILES = LEFT // TQ
WIN_TILES = TK // TQ
FF_CHUNK = 256
HEADS_PER_ROUND = 2
STAGE_ROWS = {ATT_WIDTH: 256, D_MODEL: 256, 2 * D_FF: 32}
GAIN_ROWS = 8
STAGE_SLOTS = 4
VMEM_LIMIT_BYTES = 61 * 1024 * 1024


def _rms(x, g):
    ms = jnp.mean(x * x, axis=-1, keepdims=True)
    return (x * lax.rsqrt(ms + EPS)) * g


def _head_rms(z, gain):
    rows = z.shape[0]
    lo = lax.broadcasted_iota(jnp.int32, (rows, LANES), 1) < HEAD_DIM
    outs = []
    for c in range(ATT_WIDTH // LANES):
        zc = z[:, c * LANES:(c + 1) * LANES]
        sq = zc * zc
        ms_lo = jnp.sum(jnp.where(lo, sq, 0.0), axis=-1, keepdims=True) * (1.0 / HEAD_DIM)
        ms_hi = jnp.sum(jnp.where(lo, 0.0, sq), axis=-1, keepdims=True) * (1.0 / HEAD_DIM)
        r = jnp.where(lo, lax.rsqrt(ms_lo + EPS), lax.rsqrt(ms_hi + EPS))
        outs.append((zc * r) * gain[:, c * LANES:(c + 1) * LANES])
    return jnp.concatenate(outs, axis=-1)


def _build_bias(rtab_ref, bias_sc):
    f32 = jnp.float32
    row = lax.broadcasted_iota(jnp.int32, (CHUNK, MXU_COLS), 0)
    lo_half = lax.broadcasted_iota(jnp.int32, (CHUNK, LANES), 1) < CHUNK
    n_kc = TK // CHUNK
    for head in range(ATT_HEADS):
        r = rtab_ref[head:head + 1, :]
        far = rtab_ref[head:head + 1, MXU_COLS - 1:MXU_COLS]
        x0 = jnp.broadcast_to(r, (CHUNK, MXU_COLS))
        for b in range(CHUNK.bit_length() - 1):
            x0 = jnp.where(((row >> b) & 1) == 1, pltpu.roll(x0, 1 << b, axis=1), x0)
        x1 = pltpu.roll(x0, CHUNK, axis=1)
        far_piece = jnp.broadcast_to(far, (CHUNK, LANES))
        neg_piece = jnp.full((CHUNK, LANES), NEG_INF, f32)

        def half(qc, kc):
            rel = kc - qc
            if rel < 0 or rel > LEFT_CHUNKS:
                return neg_piece
            if rel < LEFT_CHUNKS - 2:
                return far_piece
            col = (rel - (LEFT_CHUNKS - 2)) * CHUNK + (qc % 2) * CHUNK
            src = x1 if qc % 2 else x0
            return src[:, (col // LANES) * LANES:(col // LANES + 1) * LANES]

        row_blocks = []
        for qc in range(TQ // CHUNK):
            pieces = [jnp.where(lo_half, half(qc, 2 * vcol), half(qc, 2 * vcol + 1))
                      for vcol in range(n_kc // 2)]
            row_blocks.append(jnp.concatenate(pieces, axis=1))
        bias_sc[head] = jnp.concatenate(row_blocks, axis=0).T


def _step(x, y_prev, t, row0, mixg_ref, w_qv_ref, w_kug_ref, qg_ref, kg_ref, sgug_ref, wsp_ref,
          bsp_ref, ag_ref, gg_ref, w_out_ref, ffng_ref, w1_ref, w2_ref,
          k_sc, v_sc, km_sc, bias_sc, act_sc):
    bf16 = jnp.bfloat16
    f32 = jnp.float32
    nt = (((1,), (1,)), ((), ()))
    do_ffn = y_prev is not None
    h = _rms(x, mixg_ref[...]).astype(bf16)
    if do_ffn:
        hf = _rms(y_prev, ffng_ref[...]).astype(bf16)

    def ffn_chunk(j):
        if not do_ffn:
            return
        gate = jnp.dot(hf, w1_ref[:, j * FF_CHUNK:(j + 1) * FF_CHUNK], preferred_element_type=f32)
        up = jnp.dot(hf, w1_ref[:, D_FF + j * FF_CHUNK:D_FF + (j + 1) * FF_CHUNK],
                     preferred_element_type=f32)
        act_sc[:, j * FF_CHUNK:(j + 1) * FF_CHUNK] = (jax.nn.silu(gate) * up).astype(bf16)

    qv_t = lax.dot_general(w_qv_ref[...], h, nt, preferred_element_type=f32)
    k = jnp.dot(h, w_kug_ref[:, 0:ATT_WIDTH], preferred_element_type=f32)
    u = jnp.dot(h, w_kug_ref[:, ATT_WIDTH:ATT_WIDTH + GMLP_WIDTH], preferred_element_type=f32)
    vg = jnp.dot(h, w_kug_ref[:, ATT_WIDTH + GMLP_WIDTH:KUG_WIDTH], preferred_element_type=f32)

    q3 = qv_t[0:ATT_WIDTH].reshape(ATT_HEADS, HEAD_DIM, TQ)
    q_ms = jnp.mean(q3 * q3, axis=1, keepdims=True)
    q3 = (q3 * lax.rsqrt(q_ms + EPS)) * qg_ref[...].reshape(ATT_HEADS, HEAD_DIM, TQ)
    qn_t = (q3 * HEAD_DIM ** -0.5).reshape(ATT_WIDTH, TQ).astype(bf16)
    kn = _head_rms(k, kg_ref[...]).astype(bf16)
    k_sc[pl.ds(row0 + LEFT, TQ), :] = kn
    v_sc[t + PAD_TILES] = qv_t[ATT_WIDTH:2 * ATT_WIDTH].astype(bf16)
    skip = max(0, PAD_TILES - t) if isinstance(t, int) else 0
    win0, win_rows = skip * TQ, TK - skip * TQ
    kw = k_sc[pl.ds(row0 + win0, win_rows), :]
    km = km_sc[pl.ds(row0 + win0, win_rows), :]
    pad_rows = jnp.where(lax.broadcasted_iota(jnp.int32, (LANES, TQ), 0) == 0,
                         1.0, 0.0).astype(bf16)
    zero_rows = jnp.zeros((HEAD_DIM, TQ), bf16)
    ones_rows = jnp.ones((BF16_ROWS, win_rows), bf16)

    def scores(head):
        c = head // 2
        qh = qn_t[head * HEAD_DIM:(head + 1) * HEAD_DIM]
        pair = [qh, zero_rows] if head % 2 == 0 else [zero_rows, qh]
        rhs = jnp.concatenate(pair + [pad_rows], axis=0)
        lhs = jnp.concatenate([kw[:, c * LANES:(c + 1) * LANES], km], axis=1)
        s = jnp.dot(lhs, rhs, preferred_element_type=f32)
        return s + bias_sc[head, win0:TK, :]

    def probs(s):
        m = jnp.max(s, axis=0, keepdims=True)
        return jnp.exp(s - m).astype(bf16)

    def weighted_values(head, p):
        v_t = jnp.concatenate(
            [v_sc[t + w, head * HEAD_DIM:(head + 1) * HEAD_DIM, :]
             for w in range(skip, WIN_TILES)], axis=1)
        pv = jnp.dot(jnp.concatenate([v_t, ones_rows], axis=0), p,
                     preferred_element_type=f32)
        return pv[0:HEAD_DIM] * (1.0 / pv[HEAD_DIM:HEAD_DIM + 1])

    rounds = [list(range(r, r + HEADS_PER_ROUND)) for r in range(0, ATT_HEADS, HEADS_PER_ROUND)]
    outs = []
    ffn_chunk(0)
    next_chunk = 1
    s_next = [scores(hd) for hd in rounds[0]]
    for ri, heads in enumerate(rounds):
        s_cur = s_next
        if ri + 1 < len(rounds):
            s_next = [scores(hd) for hd in rounds[ri + 1]]
        for _ in heads:
            ffn_chunk(next_chunk)
            next_chunk += 1
        ps = [probs(s) for s in s_cur]
        outs += [weighted_values(hd, p) for hd, p in zip(heads, ps)]
    a_t = jnp.concatenate(outs, axis=0)
    a_ms = jnp.mean(a_t * a_t, axis=0, keepdims=True)
    a_n = ((a_t * lax.rsqrt(a_ms + EPS)) * ag_ref[...]).T.astype(bf16)

    u_act = jax.nn.gelu(u)
    vgn = _rms(jax.nn.gelu(vg), sgug_ref[...]).astype(bf16)
    ti = lax.broadcasted_iota(jnp.int32, (GMLP_GROUPS, GMLP_BLOCK, GMLP_BLOCK), 1)
    si = lax.broadcasted_iota(jnp.int32, (GMLP_GROUPS, GMLP_BLOCK, GMLP_BLOCK), 2)
    causal = (ti // CHUNK) >= (si // CHUNK)
    wsp = jnp.where(causal, wsp_ref[...], 0.0).astype(bf16)
    wsp = wsp.reshape(GMLP_GROUPS * GMLP_BLOCK, GMLP_BLOCK)
    first_of_pair = lax.broadcasted_iota(jnp.int32, (GMLP_BLOCK, LANES), 1) < GMLP_GROUP_DIM
    n_blk = TQ // GMLP_BLOCK
    pair_cols = []
    for pr in range(GMLP_WIDTH // LANES):
        lhs = wsp[2 * pr * GMLP_BLOCK:2 * (pr + 1) * GMLP_BLOCK, :]
        rhs = jnp.concatenate(
            [vgn[blk * GMLP_BLOCK:(blk + 1) * GMLP_BLOCK, pr * LANES:(pr + 1) * LANES]
             for blk in range(n_blk)], axis=1)
        res = jnp.dot(lhs, rhs, preferred_element_type=f32)
        pair_cols.append(jnp.concatenate(
            [jnp.where(first_of_pair,
                       res[0:GMLP_BLOCK, blk * LANES:(blk + 1) * LANES],
                       res[GMLP_BLOCK:2 * GMLP_BLOCK, blk * LANES:(blk + 1) * LANES])
             for blk in range(n_blk)], axis=0))
        if next_chunk < D_FF // FF_CHUNK:
            ffn_chunk(next_chunk)
            next_chunk += 1
    while next_chunk < D_FF // FF_CHUNK:
        ffn_chunk(next_chunk)
        next_chunk += 1
    mixed = (jnp.concatenate(pair_cols, axis=1)
             + jnp.concatenate([bsp_ref[...]] * n_blk, axis=0))
    g_n = _rms(u_act * mixed, gg_ref[...]).astype(bf16)

    ffn_out = None
    if do_ffn:
        ffn_out = y_prev + jnp.dot(act_sc[...], w2_ref[...], preferred_element_type=f32)
    mix = jnp.concatenate([a_n, g_n], axis=-1)
    mix_out = x + jnp.dot(mix, w_out_ref[...], preferred_element_type=f32)
    return mix_out, ffn_out


def _weight_loader(src, dst, ring, src_col=0, dst_col=0, dst_row=None):
    stage, sem, priority = ring
    rows = src.shape[0]
    _, chunk, cols = stage.shape
    assert rows % chunk == 0 and chunk == STAGE_ROWS[cols] and stage.shape[0] == STAGE_SLOTS
    n = rows // chunk
    ahead = STAGE_SLOTS - 1
    assert n >= ahead

    def copy(i, slot):
        return pltpu.make_async_copy(src.at[pl.ds(i * chunk, chunk), pl.ds(src_col, cols)],
                                     stage.at[slot], sem.at[slot])

    def prefetch():
        for i in range(ahead):
            copy(i, i).start(priority=priority)

    def drain_natural():
        def body(i, carry):
            slot = lax.rem(i, STAGE_SLOTS)

            @pl.when(i + ahead < n)
            def _():
                copy(i + ahead, lax.rem(i + ahead, STAGE_SLOTS)).start(priority=priority)

            copy(i, slot).wait()
            dst[pl.ds(pl.multiple_of(i * chunk, chunk), chunk), dst_col:dst_col + cols] = (
                stage[slot].astype(dst.dtype))
            return carry

        lax.fori_loop(0, n, body, 0)

    def drain_transposed():
        assert chunk % LANES == 0
        for i in range(n):
            if i + ahead < n:
                copy(i + ahead, (i + ahead) % STAGE_SLOTS).start(priority=priority)
            copy(i, i % STAGE_SLOTS).wait()
            dst[dst_row:dst_row + cols, i * chunk:(i + 1) * chunk] = (
                stage[i % STAGE_SLOTS].T.astype(dst.dtype))

    return prefetch, (drain_natural if dst_row is None else drain_transposed)


def _layer_kernel(layer, tiles_per_row, n_tiles,
                  x0_ref, x_ref, vecs_ref, w_in_hbm, rtab_ref, wsp_ref, bsp_ref,
                  w_out_hbm, w1_hbm, w2_hbm,
                  o_ref, k_sc, v_sc, km_sc, bias_sc, y_sc, act_sc, qg_sc, ag_sc,
                  w_qv_sc, w_kug_sc, w_out_sc, w1_sc, w2_sc, st_k, st_d, st_ff,
                  sem_k, sem_d, sem_ff):
    g = pl.program_id(0)
    mixg_ref = vecs_ref.at[0:1, :]
    ffng_ref = vecs_ref.at[1:2, :]
    kg_ref = vecs_ref.at[2:3, 0:ATT_WIDTH]
    sgug_ref = vecs_ref.at[2:3, ATT_WIDTH:ATT_WIDTH + GMLP_WIDTH]
    gg_ref = vecs_ref.at[3:4, 0:GMLP_WIDTH]
    t = lax.rem(jnp.minimum(g + 1, n_tiles - 1), tiles_per_row)
    row0 = pl.multiple_of(t * TQ, TQ)

    def mix_and_ffn(x, y_prev, t_x, row0_x):
        return _step(x, y_prev, t_x, row0_x, mixg_ref, w_qv_sc, w_kug_sc, qg_sc, kg_ref, sgug_ref,
                     wsp_ref, bsp_ref, ag_sc, gg_ref, w_out_sc, ffng_ref, w1_sc, w2_sc,
                     k_sc, v_sc, km_sc, bias_sc, act_sc)

    def zero_left_context():
        k_sc[0:LEFT, :] = jnp.zeros((LEFT, ATT_WIDTH), k_sc.dtype)
        v_sc[0:PAD_TILES] = jnp.zeros((PAD_TILES, ATT_WIDTH, TQ), v_sc.dtype)

    @pl.when(g == 0)
    def _():
        w_in_l = w_in_hbm.at[layer]
        ring_k, ring_d, ring_ff = (st_k, sem_k, 0), (st_d, sem_d, 1), (st_ff, sem_ff, 0)
        loads = [
            _weight_loader(w_in_l, w_qv_sc, ring_k, src_col=0, dst_row=0),
            _weight_loader(w_in_l, w_kug_sc, ring_d, src_col=3 * ATT_WIDTH, dst_col=ATT_WIDTH),
            _weight_loader(w_in_l, w_qv_sc, ring_k, src_col=2 * ATT_WIDTH, dst_row=ATT_WIDTH),
            _weight_loader(w_out_hbm.at[layer], w_out_sc, ring_d),
            _weight_loader(w_in_l, w_kug_sc, ring_k, src_col=ATT_WIDTH, dst_col=0),
            _weight_loader(w2_hbm.at[layer], w2_sc, ring_d),
            _weight_loader(w1_hbm.at[layer], w1_sc, ring_ff),
        ]
        loads[0][0]()
        for i, (_, drain) in enumerate(loads):
            if i + 1 < len(loads):
                loads[i + 1][0]()
            drain()
        _build_bias(rtab_ref, bias_sc)
        qg_sc[...] = jnp.broadcast_to(vecs_ref[4:5, 0:ATT_WIDTH], (TQ, ATT_WIDTH)).T
        ag_sc[...] = jnp.broadcast_to(vecs_ref[3:4, GMLP_WIDTH:GMLP_WIDTH + ATT_WIDTH],
                                      (TQ, ATT_WIDTH)).T
        is_pad = ((lax.broadcasted_iota(jnp.int32, km_sc.shape, 0) < LEFT)
                  & (lax.broadcasted_iota(jnp.int32, km_sc.shape, 1) == 0))
        km_sc[...] = jnp.where(is_pad, NEG_INF, 0.0).astype(km_sc.dtype)
        zero_left_context()
        y_sc[...] = mix_and_ffn(x0_ref[...], None, 0, 0)[0]

    pl.when(t == 0)(zero_left_context)

    mix_out, ffn_out = mix_and_ffn(x_ref[...], y_sc[...], t, row0)
    o_ref[...] = ffn_out
    y_sc[...] = mix_out


def _layer_spec(layer, shape):
    zeros = (0,) * len(shape)
    return pl.BlockSpec((None,) + shape, lambda g: (layer,) + zeros,
                        pipeline_mode=pl.Buffered(1))


def _layer(layer, x, vecs, w_in, rtab, wsp, bsp, w_out, w1, w2):
    B, S, D = x.shape
    tiles_per_row = S // TQ
    n_tiles = B * tiles_per_row
    bf16 = jnp.bfloat16
    f32 = jnp.float32

    def next_tile(g):
        gi = jnp.minimum(g + 1, n_tiles - 1)
        return (gi // tiles_per_row, gi % tiles_per_row, 0)

    def out_tile(g):
        return (g // tiles_per_row, g % tiles_per_row, 0)

    hbm = pl.BlockSpec(memory_space=pl.ANY)
    return pl.pallas_call(
        functools.partial(_layer_kernel, layer, tiles_per_row, n_tiles),
        grid=(n_tiles,),
        in_specs=[
            pl.BlockSpec((None, TQ, D), lambda g: (0, 0, 0), pipeline_mode=pl.Buffered(1)),
            pl.BlockSpec((None, TQ, D), next_tile),
            _layer_spec(layer, (GAIN_ROWS, D)),
            hbm,
            _layer_spec(layer, (ATT_HEADS, MXU_COLS)),
            _layer_spec(layer, (GMLP_GROUPS, GMLP_BLOCK, GMLP_BLOCK)),
            _layer_spec(layer, (GMLP_BLOCK, GMLP_WIDTH)),
            hbm,
            hbm,
            hbm,
        ],
        out_specs=pl.BlockSpec((None, TQ, D), out_tile),
        out_shape=jax.ShapeDtypeStruct(x.shape, x.dtype),
        scratch_shapes=[
            pltpu.VMEM((LEFT + S, ATT_WIDTH), bf16),
            pltpu.VMEM((PAD_TILES + tiles_per_row, ATT_WIDTH, TQ), bf16),
            pltpu.VMEM((LEFT + S, LANES), bf16),
            pltpu.VMEM((ATT_HEADS, TK, TQ), f32),
            pltpu.VMEM((TQ, D), f32),
            pltpu.VMEM((TQ, D_FF), bf16),
            pltpu.VMEM((ATT_WIDTH, TQ), f32),
            pltpu.VMEM((ATT_WIDTH, TQ), f32),
            pltpu.VMEM((2 * ATT_WIDTH, D), bf16),
            pltpu.VMEM((D, KUG_WIDTH), bf16),
            pltpu.VMEM((D, D), bf16),
            pltpu.VMEM((D, 2 * D_FF), bf16),
            pltpu.VMEM((D_FF, D), bf16),
            pltpu.VMEM((STAGE_SLOTS, STAGE_ROWS[ATT_WIDTH], ATT_WIDTH), f32),
            pltpu.VMEM((STAGE_SLOTS, STAGE_ROWS[D], D), f32),
            pltpu.VMEM((STAGE_SLOTS, STAGE_ROWS[2 * D_FF], 2 * D_FF), f32),
            pltpu.SemaphoreType.DMA((STAGE_SLOTS,)),
            pltpu.SemaphoreType.DMA((STAGE_SLOTS,)),
            pltpu.SemaphoreType.DMA((STAGE_SLOTS,)),
        ],
        compiler_params=pltpu.CompilerParams(
            dimension_semantics=("arbitrary",),
            vmem_limit_bytes=VMEM_LIMIT_BYTES),
        name="layer",
    )(x, x, vecs, w_in, rtab, wsp, bsp, w_out, w1, w2)


def _reversed_rel_table(rel_bias):
    near = rel_bias[:, :, ::-1][:, :, :NEAR]
    far = jnp.broadcast_to(rel_bias[:, :, 2 * MAX_REL:], near.shape[:2] + (MXU_COLS - NEAR,))
    return jnp.concatenate([near, far], axis=2).astype(jnp.float32)


def _pack_gains(mix_norm_g, ffn_norm_g, k_norm_g, sgu_norm_g, gmlp_out_norm_g, att_out_norm_g,
                q_norm_g):
    depth = mix_norm_g.shape[0]
    tiled = lambda g: jnp.tile(g, (1, ATT_HEADS))
    rows = [mix_norm_g, ffn_norm_g,
            jnp.concatenate([tiled(k_norm_g), sgu_norm_g], axis=1),
            jnp.concatenate([gmlp_out_norm_g, att_out_norm_g], axis=1),
            jnp.concatenate([tiled(q_norm_g), jnp.zeros((depth, D_MODEL - ATT_WIDTH))], axis=1)]
    rows += [jnp.zeros((depth, D_MODEL))] * (GAIN_ROWS - len(rows))
    return jnp.stack(rows, axis=1).astype(jnp.float32)


def kernel(x, mix_norm_g, w_in, q_norm_g, k_norm_g, rel_bias, sgu_norm_g, w_spatial, b_spatial,
           att_out_norm_g, gmlp_out_norm_g, w_out, ffn_norm_g, w_ffn_in, w_ffn_out):
    depth = w_in.shape[0]
    params = (
        _pack_gains(mix_norm_g, ffn_norm_g, k_norm_g, sgu_norm_g, gmlp_out_norm_g,
                    att_out_norm_g, q_norm_g),
        w_in, _reversed_rel_table(rel_bias), w_spatial,
        jnp.repeat(jnp.swapaxes(b_spatial, 1, 2), GMLP_GROUP_DIM, axis=2),
        w_out, w_ffn_in, w_ffn_out)
    for l in range(depth):
        x = _layer(l, x, *params)
    return x
```
